```python
import math
import jax, jax.numpy as jnp
from jax import lax
import numpy as np

D_MODEL = 2048
BATCH = 8
SEQ = 2048
DEPTH = 2

HEAD_DIM = 64
D_MIX = D_MODEL
N_MIXERS = 4
GROUP_WIDTH = D_MIX // N_MIXERS
N_HEADS_GROUP = GROUP_WIDTH // HEAD_DIM

DILATED_PATTERNS = ((128, 1), (512, 4), (2048, 16))
DIL_BLOCK = 128

SGU_CHUNK = 128
SGU_LN_EPS = 1e-5

MOBA_BLOCK = 256
MOBA_TOPK = 3
MOBA_Q_CHUNK = 16

RWKV_W_LORA = 96
RWKV_A_LORA = 96
RWKV_G_LORA = 256
RWKV_LN_EPS = 64e-5
RWKV_WIDTH = 3 * GROUP_WIDTH + RWKV_W_LORA + RWKV_A_LORA + RWKV_G_LORA
RWKV_SPLITS = (GROUP_WIDTH, GROUP_WIDTH + RWKV_W_LORA, 2 * GROUP_WIDTH + RWKV_W_LORA,
               3 * GROUP_WIDTH + RWKV_W_LORA, 3 * GROUP_WIDTH + RWKV_W_LORA + RWKV_A_LORA)

NUM_BUCKETS = 32
MAX_DISTANCE = 2048

D_FF = -(-8 * D_MODEL // (3 * 256)) * 256

QKV_WIDTH = 3 * GROUP_WIDTH
SGU_WIDTH = 2 * GROUP_WIDTH
IN_SPLITS = (QKV_WIDTH, QKV_WIDTH + SGU_WIDTH, 2 * QKV_WIDTH + SGU_WIDTH)
D_IN_PROJ = 2 * QKV_WIDTH + SGU_WIDTH + RWKV_WIDTH

NORM_EPS = 1e-6
NEG_INF = -1e30

kernel_name = 'hymba_style_dilated_sgu_moba_rwkv7_hybrid'


def rmsnorm(x, g):
    xf = x.astype(jnp.float32)
    y = xf * lax.rsqrt(jnp.mean(xf * xf, axis=-1, keepdims=True) + NORM_EPS)
    return (y * g.astype(jnp.float32)).astype(x.dtype)


def t5_bucket(dist):
    dist = jnp.maximum(dist, 0)
    max_exact = NUM_BUCKETS // 2
    d = jnp.maximum(dist, 1).astype(jnp.float32)
    large = max_exact + (jnp.log(d / max_exact) / math.log(MAX_DISTANCE / max_exact)
                         * (NUM_BUCKETS - max_exact)).astype(jnp.int32)
    large = jnp.minimum(large, NUM_BUCKETS - 1)
    return jnp.where(dist < max_exact, dist, large)


def split_qkv_heads(p):
    B, S, _ = p.shape
    t = p.reshape(B, S, 3, N_HEADS_GROUP, HEAD_DIM).transpose(2, 0, 3, 1, 4)
    return t[0], t[1], t[2]


def merge_heads(y):
    B, H, S, Dh = y.shape
    return y.transpose(0, 2, 1, 3).reshape(B, S, H * Dh)


def dilated_window_attention(q, k, v, bias_hb, window, dilation):
    B, H, S, Dh = q.shape
    steps = window // dilation
    C = DIL_BLOCK
    L = S // dilation
    nb = -(-L // C)
    Lp = nb * C

    def to_sub(t):
        return t.reshape(B, H, L, dilation, Dh).transpose(0, 1, 3, 2, 4)

    qb = jnp.pad(to_sub(q), ((0, 0), (0, 0), (0, 0), (0, Lp - L), (0, 0))).reshape(B, H, dilation, nb, C, Dh)

    def band(t):
        tp = jnp.pad(to_sub(t), ((0, 0), (0, 0), (0, 0), (C, Lp - L), (0, 0))).reshape(B, H, dilation, nb + 1, C, Dh)
        return jnp.concatenate([tp[:, :, :, :-1], tp[:, :, :, 1:]], axis=4)

    kb, vb = band(k), band(v)
    logits = jnp.einsum('bhrnqd,bhrnkd->bhrnqk', qb, kb).astype(jnp.float32) * (Dh ** -0.5)
    qa = jnp.arange(C)[:, None]
    kbi = jnp.arange(2 * C)[None, :]
    delta = qa + C - kbi
    blk = jnp.arange(nb)[:, None, None]
    valid = (delta >= 0) & (delta <= steps) & ((blk > 0) | (kbi >= C))
    bias = bias_hb[:, t5_bucket(delta * dilation)].astype(jnp.float32)
    logits = logits + bias[None, :, None, None]
    logits = jnp.where(valid[None, None, None], logits, NEG_INF)
    m = jnp.max(logits, axis=-1, keepdims=True)
    p = jnp.exp(logits - m)
    den = jnp.sum(p, axis=-1, keepdims=True)
    out = jnp.einsum('bhrnqk,bhrnkd->bhrnqd', (p / den).astype(v.dtype), vb)
    lse = (m + jnp.log(den))[..., 0]

    def from_sub(t):
        t = t[:, :, :, :L]
        return jnp.moveaxis(t, 2, 3).reshape((B, H, S) + t.shape[4:])

    return from_sub(out.reshape(B, H, dilation, Lp, Dh)), from_sub(lse.reshape(B, H, dilation, Lp))


def mixer_dilated(q, k, v, bias_hb):
    results = [dilated_window_attention(q, k, v, bias_hb, w, d) for (w, d) in DILATED_PATTERNS]
    outs = jnp.stack([r[0] for r in results], axis=0)
    lses = jnp.stack([r[1] for r in results], axis=0)
    wts = jax.nn.softmax(lses, axis=0)
    return jnp.einsum('pbhs,pbhsd->bhsd', wts.astype(outs.dtype), outs)


def mixer_sgu(z, ln_g, w_s, b_s):
    B, S, _ = z.shape
    z = jax.nn.gelu(z)
    u, vv = jnp.split(z, 2, axis=-1)
    vf = vv.astype(jnp.float32)
    mu = jnp.mean(vf, axis=-1, keepdims=True)
    var = jnp.mean(jnp.square(vf - mu), axis=-1, keepdims=True)
    vv = ((vf - mu) * lax.rsqrt(var + SGU_LN_EPS) * ln_g.astype(jnp.float32)).astype(z.dtype)
    nc = S // SGU_CHUNK
    vv = vv.reshape(B, nc, SGU_CHUNK, N_HEADS_GROUP, HEAD_DIM)
    w = jnp.tril(w_s)
    mixed = jnp.einsum('gts,bnsgc->bntgc', w, vv) + b_s.T[None, None, :, :, None]
    return u * mixed.reshape(B, S, GROUP_WIDTH)


def mixer_moba(q, k, v, bias_hb):
    B, H, S, Dh = q.shape
    BS = MOBA_BLOCK
    nblk = -(-S // BS)
    Sp = nblk * BS
    pad = ((0, 0), (0, 0), (0, Sp - S), (0, 0))
    qp, kp, vp = jnp.pad(q, pad), jnp.pad(k, pad), jnp.pad(v, pad)
    kb = kp.reshape(B, H, nblk, BS, Dh)
    vb = vp.reshape(B, H, nblk, BS, Dh)
    kbar = jnp.mean(kb.astype(jnp.float32), axis=3)
    topk = min(MOBA_TOPK, nblk)
    QC = MOBA_Q_CHUNK
    n_chunks = Sp // QC
    scale = Dh ** -0.5
    gather_blocks = jax.vmap(jax.vmap(lambda blocks, ids: blocks[ids]))
    head_bias = jax.vmap(lambda tbl, bk: tbl[bk], in_axes=(0, 1), out_axes=1)
    key_off = jnp.arange(BS)

    def chunk(c):
        s0 = c * QC
        ob = s0 // BS
        qc = lax.dynamic_slice_in_dim(qp, s0, QC, axis=2)
        qpos = s0 + jnp.arange(QC)
        gate = jnp.einsum('bhqd,bhnd->bhqn', qc.astype(jnp.float32), kbar)
        gate = jnp.where(jnp.arange(nblk) < ob, gate, NEG_INF)
        _, idx = lax.top_k(gate, topk)
        sel_valid = idx < ob
        ksel = gather_blocks(kb, idx)
        vsel = gather_blocks(vb, idx)
        kpos_sel = idx[..., None] * BS + key_off
        dist_sel = qpos[None, None, :, None, None] - kpos_sel
        l_sel = jnp.einsum('bhqd,bhqjkd->bhqjk', qc, ksel).astype(jnp.float32) * scale
        l_sel = l_sel + head_bias(bias_hb, t5_bucket(dist_sel)).astype(jnp.float32)
        l_sel = jnp.where(sel_valid[..., None], l_sel, NEG_INF)
        kown = lax.dynamic_index_in_dim(kb, ob, axis=2, keepdims=False)
        vown = lax.dynamic_index_in_dim(vb, ob, axis=2, keepdims=False)
        dist_own = qpos[:, None] - (ob * BS + key_off)[None, :]
        l_own = jnp.einsum('bhqd,bhkd->bhqk', qc, kown).astype(jnp.float32) * scale
        l_own = l_own + bias_hb[:, t5_bucket(dist_own)].astype(jnp.float32)[None]
        l_own = jnp.where((dist_own >= 0)[None, None], l_own, NEG_INF)
        logits = jnp.concatenate([l_sel.reshape(B, H, QC, topk * BS), l_own], axis=-1)
        p = jax.nn.softmax(logits, axis=-1).astype(v.dtype)
        p_sel = p[..., :topk * BS].reshape(B, H, QC, topk, BS)
        p_own = p[..., topk * BS:]
        return (jnp.einsum('bhqjk,bhqjkd->bhqd', p_sel, vsel)
                + jnp.einsum('bhqk,bhkd->bhqd', p_own, vown))

    outs = lax.map(chunk, jnp.arange(n_chunks))
    return outs.transpose(1, 2, 0, 3, 4).reshape(B, H, Sp, Dh)[:, :, :S]


def token_shift(y, mu):
    y_prev = jnp.pad(y, ((0, 0), (1, 0), (0, 0)))[:, :-1]
    return y + (y_prev - y) * mu


def rwkv7_scan(r, w, k, v, a, b):
    def step(state, inp):
        r_t, w_t, k_t, v_t, a_t, b_t = inp
        sa = jnp.einsum('bhvk,bhk->bhv', state, a_t)
        state = (state * w_t[:, :, None, :] + sa[..., None] * b_t[:, :, None, :]
                 + v_t[..., None] * k_t[:, :, None, :])
        return state, jnp.einsum('bhvk,bhk->bhv', state, r_t)

    B, S, H, N = r.shape
    xs = tuple(jnp.moveaxis(t, 1, 0) for t in (r, w, k, v, a, b))
    _, ys = lax.scan(step, jnp.zeros((B, H, N, N), jnp.float32), xs)
    return jnp.moveaxis(ys, 0, 1)


def mixer_rwkv7(p, mu, w0, w2, a0, a2, g2, k_k, k_a, r_k, lnx_g, lnx_b):
    B, S, _ = p.shape
    H, N = N_HEADS_GROUP, HEAD_DIM
    p = token_shift(p, mu)
    r, wd, k, v, ad, gd = jnp.split(p, RWKV_SPLITS, axis=-1)
    w_log = -jax.nn.softplus(-(w0 + jnp.tanh(wd) @ w2).astype(jnp.float32)) - 0.5
    decay = jnp.exp(-jnp.exp(w_log))
    a = jax.nn.sigmoid(a0 + ad @ a2)
    g = jax.nn.sigmoid(gd) @ g2
    heads = lambda t: t.reshape(B, S, H, N).astype(jnp.float32)
    kk = heads(k * k_k)
    kk = kk / jnp.maximum(jnp.sqrt(jnp.sum(kk * kk, axis=-1, keepdims=True)), 1e-12)
    k = k * (1 + (a - 1) * k_a)
    rh, kh, vh, ah = heads(r), heads(k), heads(v), heads(a)
    y = rwkv7_scan(rh, heads(decay), kh, vh, -kk, kk * ah)
    ym = jnp.mean(y, axis=-1, keepdims=True)
    yv = jnp.mean(jnp.square(y - ym), axis=-1, keepdims=True)
    y = ((y - ym) * lax.rsqrt(yv + RWKV_LN_EPS)).reshape(B, S, GROUP_WIDTH)
    y = y * lnx_g.astype(jnp.float32) + lnx_b.astype(jnp.float32)
    bonus = jnp.sum(rh * kh * r_k.astype(jnp.float32), axis=-1, keepdims=True) * vh
    y = y + bonus.reshape(B, S, GROUP_WIDTH)
    return (y * g.astype(jnp.float32)).astype(p.dtype)


def swiglu(h, w_gate, w_up, w_down):
    return (jax.nn.silu(h @ w_gate) * (h @ w_up)) @ w_down


def setup_inputs(seed: int = 0) -> dict:
    key = jax.random.key(seed)
    ks = iter(jax.random.split(key, 32))
    nrm = lambda shape, scale: jax.random.normal(next(ks), shape, jnp.float32) * scale
    uni = lambda shape, lo, hi: jax.random.uniform(next(ks), shape, jnp.float32, lo, hi)
    H, N, GW, T = N_HEADS_GROUP, HEAD_DIM, GROUP_WIDTH, SGU_CHUNK
    return {
        'x': nrm((BATCH, SEQ, D_MODEL), 1.0),
        'norm_mix_g': 1.0 + nrm((DEPTH, D_MODEL), 0.02),
        'w_in': nrm((DEPTH, D_MODEL, D_IN_PROJ), D_MODEL ** -0.5),
        'pos_bias': nrm((NUM_BUCKETS, 2 * H), 0.3),
        'sgu_ln_g': 1.0 + nrm((DEPTH, GW), 0.02),
        'sgu_w': nrm((DEPTH, H, T, T), T ** -0.5),
        'sgu_b': 1.0 + nrm((DEPTH, H, T), 0.1),
        'rwkv_mu': uni((DEPTH, RWKV_WIDTH), 0.0, 1.0),
        'rwkv_w0': uni((DEPTH, GW), -6.0, -1.0),
        'rwkv_w2': nrm((DEPTH, RWKV_W_LORA, GW), 0.1),
        'rwkv_a0': nrm((DEPTH, GW), 0.1),
        'rwkv_a2': nrm((DEPTH, RWKV_A_LORA, GW), 0.1),
        'rwkv_g2': nrm((DEPTH, RWKV_G_LORA, GW), RWKV_G_LORA ** -0.5),
        'rwkv_k_k': 0.85 + nrm((DEPTH, GW), 0.05),
        'rwkv_k_a': 1.0 + nrm((DEPTH, GW), 0.05),
        'rwkv_r_k': nrm((DEPTH, H, N), 0.1),
        'rwkv_lnx_g': 1.0 + nrm((DEPTH, GW), 0.02),
        'rwkv_lnx_b': nrm((DEPTH, GW), 0.02),
        'branch_norm_g': 1.0 + nrm((DEPTH, D_MIX), 0.02),
        'w_out': nrm((DEPTH, D_MIX, D_MODEL), D_MIX ** -0.5),
        'norm_ffn_g': 1.0 + nrm((DEPTH, D_MODEL), 0.02),
        'w_gate': nrm((DEPTH, D_MODEL, D_FF), D_MODEL ** -0.5),
        'w_up': nrm((DEPTH, D_MODEL, D_FF), D_MODEL ** -0.5),
        'w_down': nrm((DEPTH, D_FF, D_MODEL), D_FF ** -0.5),
        'norm_final_g': 1.0 + nrm((D_MODEL,), 0.02),
    }


def reference(x, norm_mix_g, w_in, pos_bias, sgu_ln_g, sgu_w, sgu_b, rwkv_mu, rwkv_w0,
              rwkv_w2, rwkv_a0, rwkv_a2, rwkv_g2, rwkv_k_k, rwkv_k_a, rwkv_r_k, rwkv_lnx_g,
              rwkv_lnx_b, branch_norm_g, w_out, norm_ffn_g, w_gate, w_up, w_down, norm_final_g):
    B, S, _ = x.shape
    bias_a = pos_bias[:, :N_HEADS_GROUP].T
    bias_c = pos_bias[:, N_HEADS_GROUP:].T
    for l in range(DEPTH):
        h = rmsnorm(x, norm_mix_g[l])
        proj = h @ w_in[l]
        pa, pb, pc, pd = jnp.split(proj, IN_SPLITS, axis=-1)
        ya = merge_heads(mixer_dilated(*split_qkv_heads(pa), bias_a))
        yb = mixer_sgu(pb, sgu_ln_g[l], sgu_w[l], sgu_b[l])
        yc = merge_heads(mixer_moba(*split_qkv_heads(pc), bias_c))
        yd = mixer_rwkv7(pd, rwkv_mu[l], rwkv_w0[l], rwkv_w2[l], rwkv_a0[l], rwkv_a2[l],
                         rwkv_g2[l], rwkv_k_k[l], rwkv_k_a[l], rwkv_r_k[l],
                         rwkv_lnx_g[l], rwkv_lnx_b[l])
        ycat = jnp.stack([ya, yb, yc, yd], axis=2)
        ycat = rmsnorm(ycat, branch_norm_g[l].reshape(N_MIXERS, GROUP_WIDTH))
        x = x + ycat.reshape(B, S, D_MIX) @ w_out[l]
        h = rmsnorm(x, norm_ffn_g[l])
        x = x + swiglu(h, w_gate[l], w_up[l], w_down[l])
    return rmsnorm(x, norm_final_g)
```

```python
import functools
import math

import jax
import jax.numpy as jnp
import numpy as np
from jax import lax
from jax.experimental import pallas as pl
from jax.experimental.pallas import tpu as pltpu

F32 = jnp.float32
BF16 = jnp.bfloat16

HEAD_DIM = 64
N_HEADS = 8
GW = N_HEADS * HEAD_DIM
LANES = 128
N_PAIRS = GW // LANES
DIL_PATTERNS = ((128, 1), (512, 4), (2048, 16))
DIL_BLOCK = 128
SGU_CHUNK = 128
SGU_LN_EPS = 1e-5
MOBA_BLOCK = 256
MOBA_TOPK = 3
W_LORA = 96
A_LORA = 96
G_LORA = 256
LORA_PAD = 128
RWKV_LN_EPS = 64e-5
RWKV_CHUNK = 64
RWKV_SUB = 16
NUM_BUCKETS = 32
MAX_DISTANCE = 2048
NORM_EPS = 1e-6
NEG_INF = -1e30
ATT_SCALE = HEAD_DIM ** -0.5

COL_A = 0
COL_B = 3 * GW
COL_C = COL_B + 2 * GW
COL_D = COL_C + 3 * GW
RWKV_W = 3 * GW + 2 * LORA_PAD + G_LORA
D_PROJ = COL_D + RWKV_W

VMEM_LIMIT = 56 * 1024 * 1024

HIGHEST = lax.Precision.HIGHEST


def _cparams(sem):
    return pltpu.CompilerParams(dimension_semantics=sem, vmem_limit_bytes=VMEM_LIMIT)


def _dot(a, b, precision=None):
    return lax.dot_general(a, b, (((1,), (0,)), ((), ())), precision=precision,
                           preferred_element_type=F32)


def _dot_nt(a, b, precision=None):
    return lax.dot_general(a, b, (((1,), (1,)), ((), ())), precision=precision,
                           preferred_element_type=F32)


def _dot_tn(a, b, precision=None):
    return lax.dot_general(a, b, (((0,), (0,)), ((), ())), precision=precision,
                           preferred_element_type=F32)


def _rms(x, g):
    return x * lax.rsqrt(jnp.mean(x * x, axis=-1, keepdims=True) + NORM_EPS) * g


def _t5_bucket_np(dist):
    dist = np.maximum(dist, 0)
    max_exact = NUM_BUCKETS // 2
    d = np.maximum(dist, 1).astype(np.float32)
    large = max_exact + (np.log(d / np.float32(max_exact)) / np.float32(math.log(MAX_DISTANCE / max_exact))
                         * np.float32(NUM_BUCKETS - max_exact)).astype(np.int32)
    large = np.minimum(large, NUM_BUCKETS - 1)
    return np.where(dist < max_exact, dist, large).astype(np.int32)


def _dil_bucket_tiles():
    qa = np.arange(DIL_BLOCK)[:, None]
    kj = np.arange(DIL_BLOCK)[None, :]
    tiles = []
    for _, dil in DIL_PATTERNS:
        tiles.append(_t5_bucket_np((qa - kj) * dil))
        tiles.append(_t5_bucket_np((qa + DIL_BLOCK - kj) * dil))
    return np.stack(tiles)


def _moba_bucket_tiles(nblk):
    ki = np.arange(MOBA_BLOCK)[:, None]
    qi = np.arange(MOBA_BLOCK)[None, :]
    return np.stack([_t5_bucket_np(db * MOBA_BLOCK + qi - ki) for db in range(nblk)])


def _bias_tile_kernel(tbl_ref, idx_ref, o_ref, *, head_offset):
    h = pl.program_id(0) + head_offset
    idx = idx_ref[0]
    acc = jnp.zeros(idx.shape, F32)
    for b in range(NUM_BUCKETS):
        acc = jnp.where(idx == b, tbl_ref[b, h], acc)
    o_ref[0, 0] = acc


def _bias_tiles(pos_bias, idx_np, head_offset):
    nt, r, c = idx_np.shape
    return pl.pallas_call(
        functools.partial(_bias_tile_kernel, head_offset=head_offset),
        out_shape=jax.ShapeDtypeStruct((N_HEADS, nt, r, c), F32),
        grid=(N_HEADS, nt),
        in_specs=[pl.BlockSpec(memory_space=pltpu.SMEM),
                  pl.BlockSpec((1, r, c), lambda h, t: (t, 0, 0))],
        out_specs=pl.BlockSpec((1, 1, r, c), lambda h, t: (h, t, 0, 0)),
        compiler_params=_cparams(("arbitrary", "arbitrary")),
        name="bias_tiles",
    )(pos_bias, jnp.asarray(idx_np))


def _inproj_kernel(x_ref, g_ref, w_ref, o_ref, h_scr):
    @pl.when(pl.program_id(1) == 0)
    def _():
        h_scr[...] = _rms(x_ref[...], g_ref[...]).astype(BF16)

    o_ref[...] = _dot(h_scr[...], w_ref[...])


def _inproj(x2d, g, w, tm=1024, tn=1024):
    t, d = x2d.shape
    n = w.shape[1]
    tm = min(tm, t)
    return pl.pallas_call(
        _inproj_kernel,
        out_shape=jax.ShapeDtypeStruct((t, n), F32),
        grid=(t // tm, n // tn),
        in_specs=[pl.BlockSpec((tm, d), lambda i, j: (i, 0)),
                  pl.BlockSpec((1, d), lambda i, j: (0, 0)),
                  pl.BlockSpec((d, tn), lambda i, j: (0, j))],
        out_specs=pl.BlockSpec((tm, tn), lambda i, j: (i, j)),
        scratch_shapes=[pltpu.VMEM((tm, d), BF16)],
        compiler_params=_cparams(("arbitrary", "arbitrary")),
        name="inproj",
    )(x2d, g.reshape(1, d), w)


def _dilated_kernel(q_ref, k_ref, v_ref, bias_ref, o_ref, m0, m1, l0, l1, acc, *, seq):
    c = DIL_BLOCK
    lane = lax.broadcasted_iota(jnp.int32, (c, LANES), 1)
    head0 = lane < HEAD_DIM
    row = lax.broadcasted_iota(jnp.int32, (c, c), 0)
    col = lax.broadcasted_iota(jnp.int32, (c, c), 1)
    cur_valid = col <= row
    prev_valid = col >= row

    m0[...] = jnp.full((seq, LANES), NEG_INF, F32)
    m1[...] = jnp.full((seq, LANES), NEG_INF, F32)
    l0[...] = jnp.zeros((seq, LANES), F32)
    l1[...] = jnp.zeros((seq, LANES), F32)
    acc[...] = jnp.zeros((seq, LANES), F32)

    def tile(qsl, ksl, bias_idx, valid):
        q = q_ref[0, qsl, :] * ATT_SCALE
        kb = k_ref[0, ksl, :].astype(BF16)
        vb = v_ref[0, ksl, :].astype(BF16)
        zero = jnp.zeros_like(q)
        outs = []
        alphas = []
        for h, (m_ref, l_ref) in enumerate(((m0, l0), (m1, l1))):
            qh = jnp.where(head0 if h == 0 else ~head0, q, zero).astype(BF16)
            s = _dot_nt(qh, kb) + bias_ref[h, bias_idx]
            s = jnp.where(valid, s, NEG_INF)
            m_old = m_ref[qsl, :]
            m_new = jnp.maximum(m_old, jnp.max(s, axis=1, keepdims=True))
            alpha = jnp.exp(m_old - m_new)
            p = jnp.exp(s - m_new)
            l_ref[qsl, :] = alpha * l_ref[qsl, :] + jnp.sum(p, axis=1, keepdims=True)
            m_ref[qsl, :] = m_new
            outs.append(_dot(p.astype(BF16), vb))
            alphas.append(alpha)
        acc[qsl, :] = (acc[qsl, :] * jnp.where(head0, alphas[0], alphas[1])
                       + jnp.where(head0, outs[0], outs[1]))

    for pi, (_, dil) in enumerate(DIL_PATTERNS):
        sub_len = seq // dil
        nb = sub_len // c

        def sl(r, n, dil=dil):
            start = r + n * (c * dil)
            if dil == 1:
                return pl.ds(pl.multiple_of(start, c), c)
            return pl.ds(start, c, stride=dil)

        def cur_body(i, carry, nb=nb, sl=sl, pi=pi):
            r, n = i // nb, i % nb
            tile(sl(r, n), sl(r, n), 2 * pi, cur_valid)
            return carry

        lax.fori_loop(0, dil * nb, cur_body, 0)
        if nb > 1:
            def prev_body(i, carry, nb=nb, sl=sl, pi=pi):
                r, n = i // (nb - 1), i % (nb - 1) + 1
                tile(sl(r, n), sl(r, n - 1), 2 * pi + 1, prev_valid)
                return carry

            lax.fori_loop(0, dil * (nb - 1), prev_body, 0)

    o_ref[0] = acc[...] / jnp.where(head0[:1], l0[...], l1[...])


def _mixer_dilated(proj3, bias_a):
    b, s, _ = proj3.shape
    blk = lambda off: pl.BlockSpec((1, s, LANES), lambda p, i, off=off: (i, 0, off + p))
    return pl.pallas_call(
        functools.partial(_dilated_kernel, seq=s),
        out_shape=jax.ShapeDtypeStruct((b, s, GW), F32),
        grid=(N_PAIRS, b),
        in_specs=[blk(COL_A // LANES), blk((COL_A + GW) // LANES), blk((COL_A + 2 * GW) // LANES),
                  pl.BlockSpec((2, 2 * len(DIL_PATTERNS), DIL_BLOCK, DIL_BLOCK),
                               lambda p, i: (p, 0, 0, 0))],
        out_specs=pl.BlockSpec((1, s, LANES), lambda p, i: (i, 0, p)),
        scratch_shapes=[pltpu.VMEM((s, LANES), F32)] * 5,
        compiler_params=_cparams(("arbitrary", "arbitrary")),
        name="mixer_dilated",
    )(proj3, proj3, proj3, bias_a)


def _gelu_tanh(x):
    return 0.5 * x * (1.0 + jnp.tanh(math.sqrt(2.0 / math.pi) * (x + 0.044715 * (x * x * x))))


def _sgu_kernel(u_ref, v_ref, lng_ref, w_ref, bias_ref, o_ref, *, rows):
    t = SGU_CHUNK
    u = _gelu_tanh(u_ref[0])
    v = _gelu_tanh(v_ref[0])
    mu = jnp.mean(v, axis=-1, keepdims=True)
    vc = v - mu
    var = jnp.mean(vc * vc, axis=-1, keepdims=True)
    vn = (vc * lax.rsqrt(var + SGU_LN_EPS) * lng_ref[...]).astype(BF16)
    r2 = lax.broadcasted_iota(jnp.int32, (2 * t, t), 0)
    c2 = lax.broadcasted_iota(jnp.int32, (2 * t, t), 1)
    causal = c2 <= jnp.where(r2 >= t, r2 - t, r2)
    first_group = lax.broadcasted_iota(jnp.int32, (t, LANES), 1) < HEAD_DIM
    for p in range(N_PAIRS):
        wp = jnp.where(causal, w_ref[p], 0.0).astype(BF16)
        for ci in range(rows // t):
            rs = slice(ci * t, (ci + 1) * t)
            cs = slice(p * LANES, (p + 1) * LANES)
            res = _dot(wp, vn[rs, cs])
            mixed = jnp.where(first_group, res[:t], res[t:]) + bias_ref[:, cs]
            o_ref[0, rs, cs] = u[rs, cs] * mixed


def _mixer_sgu(proj3, ln_g, w_s, b_s, rows=512):
    b, s, _ = proj3.shape
    rows = min(rows, s)
    t = SGU_CHUNK
    bias_full = jnp.repeat(b_s.T, HEAD_DIM, axis=1)
    w_pairs = w_s.reshape(N_PAIRS, 2 * t, t)
    return pl.pallas_call(
        functools.partial(_sgu_kernel, rows=rows),
        out_shape=jax.ShapeDtypeStruct((b, s, GW), F32),
        grid=(b, s // rows),
        in_specs=[pl.BlockSpec((1, rows, GW), lambda i, j: (i, j, COL_B // GW)),
                  pl.BlockSpec((1, rows, GW), lambda i, j: (i, j, COL_B // GW + 1)),
                  pl.BlockSpec((1, GW), lambda i, j: (0, 0)),
                  pl.BlockSpec((N_PAIRS, 2 * t, t), lambda i, j: (0, 0, 0)),
                  pl.BlockSpec((t, GW), lambda i, j: (0, 0))],
        out_specs=pl.BlockSpec((1, rows, GW), lambda i, j: (i, j, 0)),
        compiler_params=_cparams(("arbitrary", "arbitrary")),
        name="mixer_sgu",
    )(proj3, proj3, ln_g.reshape(1, GW), w_pairs, bias_full)


def _moba_kernel(q_ref, k_ref, v_ref, bias_ref, o_ref, sel_scr, vt_scr, ot_scr, *, seq):
    bs = MOBA_BLOCK
    nblk = seq // bs
    lane = lax.broadcasted_iota(jnp.int32, (1, LANES), 1)
    head_lanes = (lane < HEAD_DIM, lane >= HEAD_DIM)

    q_all = q_ref[0]
    kbar = jnp.concatenate(
        [jnp.mean(k_ref[0, j * bs:(j + 1) * bs, :], axis=0, keepdims=True) for j in range(nblk)], axis=0)
    for j in range(nblk):
        vt_scr[j] = v_ref[0, j * bs:(j + 1) * bs, :].T.astype(BF16)

    jrow = lax.broadcasted_iota(jnp.int32, (nblk, seq), 0)
    own = lax.broadcasted_iota(jnp.int32, (nblk, seq), 1) // bs
    krow = lax.broadcasted_iota(jnp.int32, (bs, bs), 0)
    qcol = lax.broadcasted_iota(jnp.int32, (bs, bs), 1)
    causal = krow <= qcol

    for h in range(2):
        gate = _dot_nt(jnp.where(head_lanes[h], kbar, 0.0), q_all, precision=HIGHEST)
        gate = jnp.where(jrow < own, gate, NEG_INF)
        rank = jnp.zeros((nblk, seq), jnp.int32)
        for j2 in range(nblk):
            gj = gate[j2:j2 + 1, :]
            ahead = (gj > gate) | ((gj == gate) & (j2 < jrow))
            rank = rank + ahead.astype(jnp.int32)
        sel = ((rank < MOBA_TOPK) & (jrow < own)).astype(F32)
        for qb in range(nblk):
            sel_scr[qb] = sel[:, qb * bs:(qb + 1) * bs]

        def q_block(qb, carry, h=h):
            qs = pl.ds(pl.multiple_of(qb * bs, bs), bs)
            qh = (q_ref[0, qs, :] * ATT_SCALE).astype(BF16)

            def scores(j):
                ks = pl.ds(pl.multiple_of(j * bs, bs), bs)
                kh = jnp.where(head_lanes[h], k_ref[0, ks, :], 0.0).astype(BF16)
                return _dot_nt(kh, qh) + bias_ref[h, qb - j]

            vrows = slice(h * HEAD_DIM, (h + 1) * HEAD_DIM)
            s = jnp.where(causal, scores(qb), NEG_INF)
            m = jnp.max(s, axis=0, keepdims=True)
            p = jnp.exp(s - m)
            l = jnp.sum(p, axis=0, keepdims=True)
            acc = _dot(vt_scr[qb, vrows, :], p.astype(BF16))

            def past(j, mla):
                m, l, acc = mla
                s = jnp.where(sel_scr[qb, pl.ds(j, 1), :] > 0.0, scores(j), NEG_INF)
                m_new = jnp.maximum(m, jnp.max(s, axis=0, keepdims=True))
                alpha = jnp.exp(m - m_new)
                p = jnp.exp(s - m_new)
                l = alpha * l + jnp.sum(p, axis=0, keepdims=True)
                acc = alpha * acc + _dot(vt_scr[j, vrows, :], p.astype(BF16))
                return m_new, l, acc

            m, l, acc = lax.fori_loop(0, qb, past, (m, l, acc))
            ot_scr[qb, vrows, :] = acc / l
            return carry

        lax.fori_loop(0, nblk, q_block, 0)

    for qb in range(nblk):
        o_ref[0, qb * bs:(qb + 1) * bs, :] = ot_scr[qb].T


def _mixer_moba(proj3, bias_c):
    b, s, _ = proj3.shape
    nblk = s // MOBA_BLOCK
    blk = lambda off: pl.BlockSpec((1, s, LANES), lambda p, i, off=off: (i, 0, off + p))
    return pl.pallas_call(
        functools.partial(_moba_kernel, seq=s),
        out_shape=jax.ShapeDtypeStruct((b, s, GW), F32),
        grid=(N_PAIRS, b),
        in_specs=[blk(COL_C // LANES), blk((COL_C + GW) // LANES), blk((COL_C + 2 * GW) // LANES),
                  pl.BlockSpec((2, nblk, MOBA_BLOCK, MOBA_BLOCK), lambda p, i: (p, 0, 0, 0))],
        out_specs=pl.BlockSpec((1, s, LANES), lambda p, i: (i, 0, p)),
        scratch_shapes=[pltpu.VMEM((nblk, nblk, MOBA_BLOCK), F32),
                        pltpu.VMEM((nblk, LANES, MOBA_BLOCK), BF16),
                        pltpu.VMEM((nblk, LANES, MOBA_BLOCK), F32)],
        compiler_params=_cparams(("arbitrary", "arbitrary")),
        name="mixer_moba",
    )(proj3, proj3, proj3, bias_c)


def _sigmoid(x):
    return 1.0 / (1.0 + jnp.exp(-x))


def _rwkv_prep_kernel(p_ref, prev_ref, mu_ref, w0_ref, w2_ref, a0_ref, a2_ref, g2_ref, kk_ref, ka_ref,
                      ones_ref, r_o, lw_o, k_o, v_o, a_o, b_o, g_o, *, rows):
    p = p_ref[0]
    prev_row = jnp.where(pl.program_id(1) == 0, 0.0, prev_ref[0, 7:8, :])
    first_row = lax.broadcasted_iota(jnp.int32, (rows, 1), 0) == 0
    y_prev = jnp.where(first_row, prev_row, pltpu.roll(p, 1, axis=0))
    xs = p + (y_prev - p) * mu_ref[...]
    r = xs[:, 0:GW]
    k = xs[:, GW:2 * GW]
    v = xs[:, 2 * GW:3 * GW]
    wd = xs[:, 3 * GW:3 * GW + LORA_PAD]
    ad = xs[:, 3 * GW + LORA_PAD:3 * GW + 2 * LORA_PAD]
    gd = xs[:, 3 * GW + 2 * LORA_PAD:]
    nz = -(w0_ref[...] + _dot(jnp.tanh(wd), w2_ref[...], HIGHEST))
    softplus = jnp.maximum(nz, 0.0) + jnp.log(1.0 + jnp.exp(-jnp.abs(nz)))
    log_decay = -jnp.exp(-softplus - 0.5)
    a_sig = _sigmoid(a0_ref[...] + _dot(ad, a2_ref[...], HIGHEST))
    g = _dot(_sigmoid(gd), g2_ref[...], HIGHEST)
    kk = k * kk_ref[...]
    ss = _dot(kk * kk, ones_ref[...], HIGHEST)
    kk = kk / jnp.maximum(jnp.sqrt(ss), 1e-12)
    k_mod = k * (1.0 + (a_sig - 1.0) * ka_ref[...])
    outs = ((r_o, r), (lw_o, log_decay), (k_o, k_mod), (v_o, v), (a_o, -kk), (b_o, kk * a_sig), (g_o, g))
    for ref, val in outs:
        for h in range(N_HEADS):
            ref[0, h] = val[:, h * HEAD_DIM:(h + 1) * HEAD_DIM]


def _rwkv_prep(proj3, mu, w0, w2, a0, a2, g2, k_k, k_a, rows=256):
    b, s, _ = proj3.shape
    rows = min(rows, s)
    pad = lambda w, n: jnp.concatenate([w, jnp.zeros((n - w.shape[0],) + w.shape[1:], w.dtype)], axis=0)
    head_of = np.arange(GW) // HEAD_DIM
    ones_bd = jnp.asarray((head_of[:, None] == head_of[None, :]).astype(np.float32))
    vec = lambda a: a.reshape(1, -1)
    full = lambda shape: pl.BlockSpec(shape, lambda i, j: (0,) * len(shape))
    col = COL_D // RWKV_W
    sub = rows // 8
    out_sd = jax.ShapeDtypeStruct((b, N_HEADS, s, HEAD_DIM), F32)
    return pl.pallas_call(
        functools.partial(_rwkv_prep_kernel, rows=rows),
        out_shape=[out_sd] * 7,
        grid=(b, s // rows),
        in_specs=[pl.BlockSpec((1, rows, RWKV_W), lambda i, j: (i, j, col)),
                  pl.BlockSpec((1, 8, RWKV_W), lambda i, j: (i, jnp.maximum(j * sub - 1, 0), col)),
                  full((1, RWKV_W)), full((1, GW)), full((LORA_PAD, GW)), full((1, GW)),
                  full((LORA_PAD, GW)), full((G_LORA, GW)), full((1, GW)), full((1, GW)), full((GW, GW))],
        out_specs=[pl.BlockSpec((1, N_HEADS, rows, HEAD_DIM), lambda i, j: (i, 0, j, 0))] * 7,
        compiler_params=_cparams(("arbitrary", "arbitrary")),
        name="rwkv_prep",
    )(proj3, proj3, vec(mu), vec(w0), pad(w2, LORA_PAD), vec(a0), pad(a2, LORA_PAD), g2, vec(k_k), vec(k_a),
      ones_bd)


def _rwkv_chunk_terms(r, lw, k, v, a, b, masks):
    tri, strict, incl, same_sub, eye = masks
    c = RWKV_CHUNK
    hp = functools.partial
    dot, dot_nt, dot_tn = hp(_dot, precision=HIGHEST), hp(_dot_nt, precision=HIGHEST), hp(_dot_tn, precision=HIGHEST)
    cum = dot(tri, lw)
    e_cum = jnp.exp(cum)
    e_inv = jnp.exp(-cum)
    at = a * jnp.exp(cum - lw)
    rt = r * e_cum
    bt = b * e_inv
    kt = k * e_inv
    g_end = e_cum[c - 1:c, :]
    b_end = bt * g_end
    k_end = kt * g_end
    ar = jnp.concatenate([at, rt], axis=0)
    gb = dot_nt(ar, bt)
    gk = dot_nt(ar, kt)
    a_ab = jnp.where(strict, gb[:c], 0.0)
    a_rb = jnp.where(incl, gb[c:], 0.0)
    a_ak = jnp.where(strict, gk[:c], 0.0)
    a_rk = jnp.where(incl, gk[c:], 0.0)
    ident = eye.astype(F32)
    ad = jnp.where(same_sub, a_ab, 0.0)
    an = a_ab - ad
    p2 = dot(ad, ad)
    p4 = dot(p2, p2)
    p8 = dot(p4, p4)
    td = dot(dot(ident + ad, ident + p2), dot(ident + p4, ident + p8))
    m1 = dot(td, an)
    m2 = dot(m1, m1)
    t_inv = dot(dot(ident + m1, ident + m2), td)
    av = dot(jnp.concatenate([a_ak, a_rk], axis=0), v)
    a_hat = dot(t_inv, at)
    u0 = dot(t_inv, av[:c])
    r_hat = rt + dot(a_rb, a_hat)
    o0 = dot(a_rb, u0) + av[c:]
    p_mat = jnp.where(eye, g_end, 0.0) + dot_tn(a_hat, b_end)
    z_mat = dot_tn(u0, b_end) + dot_tn(v, k_end)
    return p_mat, z_mat, r_hat, o0


def _rwkv_scan_kernel(r_ref, lw_ref, k_ref, v_ref, a_ref, b_ref, g_ref, rk_ref, lng_ref, lnb_ref, o_ref,
                      state, *, rows):
    c = RWKV_CHUNK
    ri = lax.broadcasted_iota(jnp.int32, (c, c), 0)
    ci = lax.broadcasted_iota(jnp.int32, (c, c), 1)
    masks = ((ci <= ri).astype(F32), ci < ri, ci <= ri, (ri // RWKV_SUB) == (ci // RWKV_SUB), ri == ci)

    @pl.when(pl.program_id(2) == 0)
    def _():
        state[...] = jnp.zeros_like(state)

    def chunk(idx, carry):
        rs = pl.ds(pl.multiple_of(idx * c, c), c)
        ys = []
        for h in range(2):
            r, lw, k, v = r_ref[0, h, rs, :], lw_ref[0, h, rs, :], k_ref[0, h, rs, :], v_ref[0, h, rs, :]
            a, b, g = a_ref[0, h, rs, :], b_ref[0, h, rs, :], g_ref[0, h, rs, :]
            p_mat, z_mat, r_hat, o0 = _rwkv_chunk_terms(r, lw, k, v, a, b, masks)
            s0 = state[h]
            o = _dot_nt(r_hat, s0, HIGHEST) + o0
            state[h] = _dot(s0, p_mat, HIGHEST) + z_mat
            mean = jnp.mean(o, axis=-1, keepdims=True)
            oc = o - mean
            var = jnp.mean(oc * oc, axis=-1, keepdims=True)
            y = oc * lax.rsqrt(var + RWKV_LN_EPS) * lng_ref[0, h:h + 1, :] + lnb_ref[0, h:h + 1, :]
            bonus = jnp.sum(r * k * rk_ref[0, h:h + 1, :], axis=-1, keepdims=True) * v
            ys.append((y + bonus) * g)
        o_ref[0, rs, :] = jnp.concatenate(ys, axis=-1)
        return carry

    lax.fori_loop(0, rows // c, chunk, 0)


def _rwkv_scan(prep, r_k, lnx_g, lnx_b, rows=512):
    b, _, s, _ = prep[0].shape
    rows = min(rows, s)
    seq_spec = pl.BlockSpec((1, 2, rows, HEAD_DIM), lambda i, p, t: (i, p, t, 0))
    par_spec = pl.BlockSpec((1, 2, HEAD_DIM), lambda i, p, t: (p, 0, 0))
    par = lambda a: a.reshape(N_PAIRS, 2, HEAD_DIM)
    return pl.pallas_call(
        functools.partial(_rwkv_scan_kernel, rows=rows),
        out_shape=jax.ShapeDtypeStruct((b, s, GW), F32),
        grid=(b, N_PAIRS, s // rows),
        in_specs=[seq_spec] * 7 + [par_spec] * 3,
        out_specs=pl.BlockSpec((1, rows, LANES), lambda i, p, t: (i, t, p)),
        scratch_shapes=[pltpu.VMEM((2, HEAD_DIM, HEAD_DIM), F32)],
        compiler_params=_cparams(("arbitrary", "arbitrary", "arbitrary")),
        name="rwkv_scan",
    )(*prep, par(r_k), par(lnx_g), par(lnx_b))


def _outproj_kernel(ya_ref, yb_ref, yc_ref, yd_ref, g_ref, w_ref, x_ref, o_ref):
    acc = x_ref[...]
    for i, y_ref in enumerate((ya_ref, yb_ref, yc_ref, yd_ref)):
        yn = _rms(y_ref[...], g_ref[i:i + 1, :]).astype(BF16)
        acc = acc + _dot(yn, w_ref[i * GW:(i + 1) * GW, :])
    o_ref[...] = acc


def _outproj(ys, g, w, x2d, tm=512):
    t, d = x2d.shape
    tm = min(tm, t)
    y_spec = pl.BlockSpec((tm, GW), lambda i: (i, 0))
    return pl.pallas_call(
        _outproj_kernel,
        out_shape=jax.ShapeDtypeStruct((t, d), F32),
        grid=(t // tm,),
        in_specs=[y_spec] * 4 + [pl.BlockSpec((4, GW), lambda i: (0, 0)),
                                 pl.BlockSpec((4 * GW, d), lambda i: (0, 0)),
                                 pl.BlockSpec((tm, d), lambda i: (i, 0))],
        out_specs=pl.BlockSpec((tm, d), lambda i: (i, 0)),
        compiler_params=_cparams(("arbitrary",)),
        name="outproj",
    )(*[y.reshape(t, GW) for y in ys], g.reshape(4, GW), w, x2d)


def _ffn_kernel(x_ref, g_ref, wg_ref, wu_ref, wd_ref, gf_ref, o_ref, h_scr, acc_scr, *, final_norm):
    j = pl.program_id(1)

    @pl.when(j == 0)
    def _():
        h_scr[...] = _rms(x_ref[...], g_ref[...]).astype(BF16)
        acc_scr[...] = x_ref[...]

    h = h_scr[...]
    gate = _dot(h, wg_ref[...])
    up = _dot(h, wu_ref[...])
    act = (gate * _sigmoid(gate) * up).astype(BF16)
    acc_scr[...] += _dot(act, wd_ref[...])

    @pl.when(j == pl.num_programs(1) - 1)
    def _():
        y = acc_scr[...]
        o_ref[...] = _rms(y, gf_ref[...]) if final_norm else y


def _ffn(x2d, g, wg, wu, wd, g_final, final_norm, tm=512, tf=512):
    t, d = x2d.shape
    f = wg.shape[1]
    tm = min(tm, t)
    return pl.pallas_call(
        functools.partial(_ffn_kernel, final_norm=final_norm),
        out_shape=jax.ShapeDtypeStruct((t, d), F32),
        grid=(t // tm, f // tf),
        in_specs=[pl.BlockSpec((tm, d), lambda i, j: (i, 0)),
                  pl.BlockSpec((1, d), lambda i, j: (0, 0)),
                  pl.BlockSpec((d, tf), lambda i, j: (0, j)),
                  pl.BlockSpec((d, tf), lambda i, j: (0, j)),
                  pl.BlockSpec((tf, d), lambda i, j: (j, 0)),
                  pl.BlockSpec((1, d), lambda i, j: (0, 0))],
        out_specs=pl.BlockSpec((tm, d), lambda i, j: (i, 0)),
        scratch_shapes=[pltpu.VMEM((tm, d), BF16), pltpu.VMEM((tm, d), F32)],
        compiler_params=_cparams(("arbitrary", "arbitrary")),
        name="ffn",
    )(x2d, g.reshape(1, d), wg, wu, wd, g_final.reshape(1, d))


def _reorder_in_proj(w_in, mu):
    d0 = 3 * GW + 2 * GW + 3 * GW
    sizes = (GW, W_LORA, GW, GW, A_LORA, G_LORA)
    offs = np.concatenate([[0], np.cumsum(sizes)])
    piece = lambda a, i: a[..., int(offs[i]):int(offs[i + 1])]
    zeros = lambda a, n: jnp.zeros(a.shape[:-1] + (n,), a.dtype)

    def reorder(a):
        return jnp.concatenate([piece(a, 0), piece(a, 2), piece(a, 3),
                                piece(a, 1), zeros(a, LORA_PAD - W_LORA),
                                piece(a, 4), zeros(a, LORA_PAD - A_LORA), piece(a, 5)], axis=-1)

    w = jnp.concatenate([w_in[:, :d0], reorder(w_in[:, d0:])], axis=-1)
    return w.astype(BF16), reorder(mu)


def kernel(x, norm_mix_g, w_in, pos_bias, sgu_ln_g, sgu_w, sgu_b, rwkv_mu, rwkv_w0, rwkv_w2, rwkv_a0, rwkv_a2,
           rwkv_g2, rwkv_k_k, rwkv_k_a, rwkv_r_k, rwkv_lnx_g, rwkv_lnx_b, branch_norm_g, w_out, norm_ffn_g,
           w_gate, w_up, w_down, norm_final_g):
    b, s, d = x.shape
    depth = w_in.shape[0]
    assert s % (DIL_BLOCK * DIL_PATTERNS[-1][1]) == 0 and s % MOBA_BLOCK == 0
    bias_a = _bias_tiles(pos_bias, _dil_bucket_tiles(), 0)
    bias_c = _bias_tiles(pos_bias, _moba_bucket_tiles(s // MOBA_BLOCK), N_HEADS)
    x2d = x.reshape(b * s, d)
    for l in range(depth):
        w_l, mu_l = _reorder_in_proj(w_in[l], rwkv_mu[l])
        proj3 = _inproj(x2d, norm_mix_g[l], w_l).reshape(b, s, D_PROJ)
        ya = _mixer_dilated(proj3, bias_a)
        yb = _mixer_sgu(proj3, sgu_ln_g[l], sgu_w[l], sgu_b[l])
        yc = _mixer_moba(proj3, bias_c)
        prep = _rwkv_prep(proj3, mu_l, rwkv_w0[l], rwkv_w2[l], rwkv_a0[l], rwkv_a2[l], rwkv_g2[l],
                          rwkv_k_k[l], rwkv_k_a[l])
        yd = _rwkv_scan(prep, rwkv_r_k[l], rwkv_lnx_g[l], rwkv_lnx_b[l])
        x2d = _outproj((ya, yb, yc, yd), branch_norm_g[l], w_out[l].astype(BF16), x2d)
        x2d = _ffn(x2d, norm_ffn_g[l], w_gate[l].astype(BF16), w_up[l].astype(BF16), w_down[l].astype(BF16),
                   norm_final_g, final_norm=(l == depth - 1))
    return x2d.reshape(b, s, d)
```

```python
import functools
import math

import jax
import jax.numpy as jnp
import numpy as np
from jax import lax
from jax.experimental import pallas as pl
from jax.experimental.pallas import tpu as pltpu

F32 = jnp.float32
BF16 = jnp.bfloat16

HEAD_DIM = 64
N_HEADS = 8
GW = N_HEADS * HEAD_DIM
LANES = 128
N_PAIRS = GW // LANES
DIL_PATTERNS = ((128, 1), (512, 4), (2048, 16))
DIL_BLOCK = 128
SGU_CHUNK = 128
SGU_LN_EPS = 1e-5
MOBA_BLOCK = 256
MOBA_TOPK = 3
W_LORA = 96
A_LORA = 96
G_LORA = 256
LORA_PAD = 128
RWKV_LN_EPS = 64e-5
RWKV_CHUNK = 64
RWKV_SUB = 16
NUM_BUCKETS = 32
MAX_DISTANCE = 2048
NORM_EPS = 1e-6
NEG_INF = -1e30
ATT_SCALE = HEAD_DIM ** -0.5

COL_A = 0
COL_B = 3 * GW
COL_C = COL_B + 2 * GW
COL_D = COL_C + 3 * GW
RWKV_W = 3 * GW + 2 * LORA_PAD + G_LORA
D_PROJ = COL_D + RWKV_W

VMEM_LIMIT = 56 * 1024 * 1024

HIGHEST = lax.Precision.HIGHEST


def _cparams(sem):
    return pltpu.CompilerParams(dimension_semantics=sem, vmem_limit_bytes=VMEM_LIMIT)


def _dot(a, b, precision=None):
    return lax.dot_general(a, b, (((1,), (0,)), ((), ())), precision=precision,
                           preferred_element_type=F32)


def _dot_nt(a, b, precision=None):
    return lax.dot_general(a, b, (((1,), (1,)), ((), ())), precision=precision,
                           preferred_element_type=F32)


def _dot_tn(a, b, precision=None):
    return lax.dot_general(a, b, (((0,), (0,)), ((), ())), precision=precision,
                           preferred_element_type=F32)


def _rms(x, g):
    return x * lax.rsqrt(jnp.mean(x * x, axis=-1, keepdims=True) + NORM_EPS) * g


def _t5_bucket_np(dist):
    dist = np.maximum(dist, 0)
    max_exact = NUM_BUCKETS // 2
    d = np.maximum(dist, 1).astype(np.float32)
    large = max_exact + (np.log(d / np.float32(max_exact)) / np.float32(math.log(MAX_DISTANCE / max_exact))
                         * np.float32(NUM_BUCKETS - max_exact)).astype(np.int32)
    large = np.minimum(large, NUM_BUCKETS - 1)
    return np.where(dist < max_exact, dist, large).astype(np.int32)


def _dil_bucket_tiles():
    qa = np.arange(DIL_BLOCK)[:, None]
    kj = np.arange(DIL_BLOCK)[None, :]
    tiles = []
    for _, dil in DIL_PATTERNS:
        tiles.append(_t5_bucket_np((qa - kj) * dil))
        tiles.append(_t5_bucket_np((qa + DIL_BLOCK - kj) * dil))
    return np.stack(tiles)


def _moba_bucket_tiles(nblk):
    ki = np.arange(MOBA_BLOCK)[:, None]
    qi = np.arange(MOBA_BLOCK)[None, :]
    return np.stack([_t5_bucket_np(db * MOBA_BLOCK + qi - ki) for db in range(nblk)])


def _bias_tile_kernel(tbl_ref, idx_ref, o_ref, *, head_offset):
    h = pl.program_id(0) + head_offset
    idx = idx_ref[0]
    acc = jnp.zeros(idx.shape, F32)
    for b in range(NUM_BUCKETS):
        acc = jnp.where(idx == b, tbl_ref[b, h], acc)
    o_ref[0, 0] = acc


def _bias_tiles(pos_bias, idx_np, head_offset):
    nt, r, c = idx_np.shape
    return pl.pallas_call(
        functools.partial(_bias_tile_kernel, head_offset=head_offset),
        out_shape=jax.ShapeDtypeStruct((N_HEADS, nt, r, c), F32),
        grid=(N_HEADS, nt),
        in_specs=[pl.BlockSpec(memory_space=pltpu.SMEM),
                  pl.BlockSpec((1, r, c), lambda h, t: (t, 0, 0))],
        out_specs=pl.BlockSpec((1, 1, r, c), lambda h, t: (h, t, 0, 0)),
        compiler_params=_cparams(("arbitrary", "arbitrary")),
        name="bias_tiles",
    )(pos_bias, jnp.asarray(idx_np))


def _inproj_kernel(x_ref, g_ref, w_ref, o_ref, h_scr):
    @pl.when(pl.program_id(1) == 0)
    def _():
        h_scr[...] = _rms(x_ref[...], g_ref[...]).astype(BF16)

    o_ref[...] = _dot(h_scr[...], w_ref[...])


def _inproj(x2d, g, w, tm=1024, tn=1024):
    t, d = x2d.shape
    n = w.shape[1]
    tm = min(tm, t)
    return pl.pallas_call(
        _inproj_kernel,
        out_shape=jax.ShapeDtypeStruct((t, n), F32),
        grid=(t // tm, n // tn),
        in_specs=[pl.BlockSpec((tm, d), lambda i, j: (i, 0)),
                  pl.BlockSpec((1, d), lambda i, j: (0, 0)),
                  pl.BlockSpec((d, tn), lambda i, j: (0, j))],
        out_specs=pl.BlockSpec((tm, tn), lambda i, j: (i, j)),
        scratch_shapes=[pltpu.VMEM((tm, d), BF16)],
        compiler_params=_cparams(("arbitrary", "arbitrary")),
        name="inproj",
    )(x2d, g.reshape(1, d), w)


def _dilated_kernel(q_ref, k_ref, v_ref, bias_ref, o_ref, m0, m1, l0, l1, acc, *, seq):
    c = DIL_BLOCK
    lane = lax.broadcasted_iota(jnp.int32, (c, LANES), 1)
    head0 = lane < HEAD_DIM
    row = lax.broadcasted_iota(jnp.int32, (c, c), 0)
    col = lax.broadcasted_iota(jnp.int32, (c, c), 1)
    cur_valid = col <= row
    prev_valid = col >= row

    m0[...] = jnp.full((seq, LANES), NEG_INF, F32)
    m1[...] = jnp.full((seq, LANES), NEG_INF, F32)
    l0[...] = jnp.zeros((seq, LANES), F32)
    l1[...] = jnp.zeros((seq, LANES), F32)
    acc[...] = jnp.zeros((seq, LANES), F32)

    def tile(qsl, ksl, bias_idx, valid):
        q = q_ref[0, qsl, :] * ATT_SCALE
        kb = k_ref[0, ksl, :].astype(BF16)
        vb = v_ref[0, ksl, :].astype(BF16)
        zero = jnp.zeros_like(q)
        outs = []
        alphas = []
        for h, (m_ref, l_ref) in enumerate(((m0, l0), (m1, l1))):
            qh = jnp.where(head0 if h == 0 else ~head0, q, zero).astype(BF16)
            s = _dot_nt(qh, kb) + bias_ref[h, bias_idx]
            s = jnp.where(valid, s, NEG_INF)
            m_old = m_ref[qsl, :]
            m_new = jnp.maximum(m_old, jnp.max(s, axis=1, keepdims=True))
            alpha = jnp.exp(m_old - m_new)
            p = jnp.exp(s - m_new)
            l_ref[qsl, :] = alpha * l_ref[qsl, :] + jnp.sum(p, axis=1, keepdims=True)
            m_ref[qsl, :] = m_new
            outs.append(_dot(p.astype(BF16), vb))
            alphas.append(alpha)
        acc[qsl, :] = (acc[qsl, :] * jnp.where(head0, alphas[0], alphas[1])
                       + jnp.where(head0, outs[0], outs[1]))

    for pi, (_, dil) in enumerate(DIL_PATTERNS):
        sub_len = seq // dil
        nb = sub_len // c

        def sl(r, n, dil=dil):
            start = r + n * (c * dil)
            if dil == 1:
                return pl.ds(pl.multiple_of(start, c), c)
            return pl.ds(start, c, stride=dil)

        def cur_body(i, carry, nb=nb, sl=sl, pi=pi):
            r, n = i // nb, i % nb
            tile(sl(r, n), sl(r, n), 2 * pi, cur_valid)
            return carry

        lax.fori_loop(0, dil * nb, cur_body, 0)
        if nb > 1:
            def prev_body(i, carry, nb=nb, sl=sl, pi=pi):
                r, n = i // (nb - 1), i % (nb - 1) + 1
                tile(sl(r, n), sl(r, n - 1), 2 * pi + 1, prev_valid)
                return carry

            lax.fori_loop(0, dil * (nb - 1), prev_body, 0)

    o_ref[0] = acc[...] / jnp.where(head0[:1], l0[...], l1[...])


def _mixer_dilated(proj3, bias_a):
    b, s, _ = proj3.shape
    blk = lambda off: pl.BlockSpec((1, s, LANES), lambda p, i, off=off: (i, 0, off + p))
    return pl.pallas_call(
        functools.partial(_dilated_kernel, seq=s),
        out_shape=jax.ShapeDtypeStruct((b, s, GW), F32),
        grid=(N_PAIRS, b),
        in_specs=[blk(COL_A // LANES), blk((COL_A + GW) // LANES), blk((COL_A + 2 * GW) // LANES),
                  pl.BlockSpec((2, 2 * len(DIL_PATTERNS), DIL_BLOCK, DIL_BLOCK),
                               lambda p, i: (p, 0, 0, 0))],
        out_specs=pl.BlockSpec((1, s, LANES), lambda p, i: (i, 0, p)),
        scratch_shapes=[pltpu.VMEM((s, LANES), F32)] * 5,
        compiler_params=_cparams(("arbitrary", "arbitrary")),
        name="mixer_dilated",
    )(proj3, proj3, proj3, bias_a)


def _gelu_tanh(x):
    return 0.5 * x * (1.0 + jnp.tanh(math.sqrt(2.0 / math.pi) * (x + 0.044715 * (x * x * x))))


def _sgu_kernel(u_ref, v_ref, lng_ref, w_ref, bias_ref, o_ref, *, rows):
    t = SGU_CHUNK
    u = _gelu_tanh(u_ref[0])
    v = _gelu_tanh(v_ref[0])
    mu = jnp.mean(v, axis=-1, keepdims=True)
    vc = v - mu
    var = jnp.mean(vc * vc, axis=-1, keepdims=True)
    vn = (vc * lax.rsqrt(var + SGU_LN_EPS) * lng_ref[...]).astype(BF16)
    r2 = lax.broadcasted_iota(jnp.int32, (2 * t, t), 0)
    c2 = lax.broadcasted_iota(jnp.int32, (2 * t, t), 1)
    causal = c2 <= jnp.where(r2 >= t, r2 - t, r2)
    first_group = lax.broadcasted_iota(jnp.int32, (t, LANES), 1) < HEAD_DIM
    for p in range(N_PAIRS):
        wp = jnp.where(causal, w_ref[p], 0.0).astype(BF16)
        for ci in range(rows // t):
            rs = slice(ci * t, (ci + 1) * t)
            cs = slice(p * LANES, (p + 1) * LANES)
            res = _dot(wp, vn[rs, cs])
            mixed = jnp.where(first_group, res[:t], res[t:]) + bias_ref[:, cs]
            o_ref[0, rs, cs] = u[rs, cs] * mixed


def _mixer_sgu(proj3, ln_g, w_s, b_s, rows=512):
    b, s, _ = proj3.shape
    rows = min(rows, s)
    t = SGU_CHUNK
    bias_full = jnp.repeat(b_s.T, HEAD_DIM, axis=1)
    w_pairs = w_s.reshape(N_PAIRS, 2 * t, t)
    return pl.pallas_call(
        functools.partial(_sgu_kernel, rows=rows),
        out_shape=jax.ShapeDtypeStruct((b, s, GW), F32),
        grid=(b, s // rows),
        in_specs=[pl.BlockSpec((1, rows, GW), lambda i, j: (i, j, COL_B // GW)),
                  pl.BlockSpec((1, rows, GW), lambda i, j: (i, j, COL_B // GW + 1)),
                  pl.BlockSpec((1, GW), lambda i, j: (0, 0)),
                  pl.BlockSpec((N_PAIRS, 2 * t, t), lambda i, j: (0, 0, 0)),
                  pl.BlockSpec((t, GW), lambda i, j: (0, 0))],
        out_specs=pl.BlockSpec((1, rows, GW), lambda i, j: (i, j, 0)),
        compiler_params=_cparams(("arbitrary", "arbitrary")),
        name="mixer_sgu",
    )(proj3, proj3, ln_g.reshape(1, GW), w_pairs, bias_full)


def _moba_kernel(q_ref, k_ref, v_ref, bias_ref, o_ref, sel_scr, vt_scr, ot_scr, *, seq):
    bs = MOBA_BLOCK
    nblk = seq // bs
    lane = lax.broadcasted_iota(jnp.int32, (1, LANES), 1)
    head_lanes = (lane < HEAD_DIM, lane >= HEAD_DIM)

    q_all = q_ref[0]
    kbar = jnp.concatenate(
        [jnp.mean(k_ref[0, j * bs:(j + 1) * bs, :], axis=0, keepdims=True) for j in range(nblk)], axis=0)
    for j in range(nblk):
        vt_scr[j] = v_ref[0, j * bs:(j + 1) * bs, :].T.astype(BF16)

    jrow = lax.broadcasted_iota(jnp.int32, (nblk, seq), 0)
    own = lax.broadcasted_iota(jnp.int32, (nblk, seq), 1) // bs
    krow = lax.broadcasted_iota(jnp.int32, (bs, bs), 0)
    qcol = lax.broadcasted_iota(jnp.int32, (bs, bs), 1)
    causal = krow <= qcol

    for h in range(2):
        gate = _dot_nt(jnp.where(head_lanes[h], kbar, 0.0), q_all, precision=HIGHEST)
        gate = jnp.where(jrow < own, gate, NEG_INF)
        rank = jnp.zeros((nblk, seq), jnp.int32)
        for j2 in range(nblk):
            gj = gate[j2:j2 + 1, :]
            ahead = (gj > gate) | ((gj == gate) & (j2 < jrow))
            rank = rank + ahead.astype(jnp.int32)
        sel = ((rank < MOBA_TOPK) & (jrow < own)).astype(F32)
        for qb in range(nblk):
            sel_scr[qb] = sel[:, qb * bs:(qb + 1) * bs]

        def q_block(qb, carry, h=h):
            qs = pl.ds(pl.multiple_of(qb * bs, bs), bs)
            qh = (q_ref[0, qs, :] * ATT_SCALE).astype(BF16)

            def scores(j):
                ks = pl.ds(pl.multiple_of(j * bs, bs), bs)
                kh = jnp.where(head_lanes[h], k_ref[0, ks, :], 0.0).astype(BF16)
                return _dot_nt(kh, qh) + bias_ref[h, qb - j]

            vrows = slice(h * HEAD_DIM, (h + 1) * HEAD_DIM)
            s = jnp.where(causal, scores(qb), NEG_INF)
            m = jnp.max(s, axis=0, keepdims=True)
            p = jnp.exp(s - m)
            l = jnp.sum(p, axis=0, keepdims=True)
            acc = _dot(vt_scr[qb, vrows, :], p.astype(BF16))

            def past(j, mla):
                m, l, acc = mla
                s = jnp.where(sel_scr[qb, pl.ds(j, 1), :] > 0.0, scores(j), NEG_INF)
                m_new = jnp.maximum(m, jnp.max(s, axis=0, keepdims=True))
                alpha = jnp.exp(m - m_new)
                p = jnp.exp(s - m_new)
                l = alpha * l + jnp.sum(p, axis=0, keepdims=True)
                acc = alpha * acc + _dot(vt_scr[j, vrows, :], p.astype(BF16))
                return m_new, l, acc

            m, l, acc = lax.fori_loop(0, qb, past, (m, l, acc))
            ot_scr[qb, vrows, :] = acc / l
            return carry

        lax.fori_loop(0, nblk, q_block, 0)

    for qb in range(nblk):
        o_ref[0, qb * bs:(qb + 1) * bs, :] = ot_scr[qb].T


def _mixer_moba(proj3, bias_c):
    b, s, _ = proj3.shape
    nblk = s // MOBA_BLOCK
    blk = lambda off: pl.BlockSpec((1, s, LANES), lambda p, i, off=off: (i, 0, off + p))
    return pl.pallas_call(
        functools.partial(_moba_kernel, seq=s),
        out_shape=jax.ShapeDtypeStruct((b, s, GW), F32),
        grid=(N_PAIRS, b),
        in_specs=[blk(COL_C // LANES), blk((COL_C + GW) // LANES), blk((COL_C + 2 * GW) // LANES),
                  pl.BlockSpec((2, nblk, MOBA_BLOCK, MOBA_BLOCK), lambda p, i: (p, 0, 0, 0))],
        out_specs=pl.BlockSpec((1, s, LANES), lambda p, i: (i, 0, p)),
        scratch_shapes=[pltpu.VMEM((nblk, nblk, MOBA_BLOCK), F32),
                        pltpu.VMEM((nblk, LANES, MOBA_BLOCK), BF16),
                        pltpu.VMEM((nblk, LANES, MOBA_BLOCK), F32)],
        compiler_params=_cparams(("arbitrary", "arbitrary")),
        name="mixer_moba",
    )(proj3, proj3, proj3, bias_c)


def _sigmoid(x):
    return 1.0 / (1.0 + jnp.exp(-x))


def _rwkv_prep_kernel(p_ref, prev_ref, mu_ref, w0_ref, w2_ref, a0_ref, a2_ref, g2_ref, kk_ref, ka_ref, rk_ref,
                      ones_ref, tri_ref, at_o, rt_o, bt_o, kt_o, be_o, ke_o, v_o, ee_o, g_o, bg_o, *, rows):
    c = RWKV_CHUNK
    p = p_ref[0]
    prev_row = jnp.where(pl.program_id(1) == 0, 0.0, prev_ref[0, 7:8, :])
    first_row = lax.broadcasted_iota(jnp.int32, (rows, 1), 0) == 0
    y_prev = jnp.where(first_row, prev_row, pltpu.roll(p, 1, axis=0))
    xs = p + (y_prev - p) * mu_ref[...]
    r = xs[:, 0:GW]
    k = xs[:, GW:2 * GW]
    v = xs[:, 2 * GW:3 * GW]
    wd = xs[:, 3 * GW:3 * GW + LORA_PAD]
    ad = xs[:, 3 * GW + LORA_PAD:3 * GW + 2 * LORA_PAD]
    gd = xs[:, 3 * GW + 2 * LORA_PAD:]
    nz = -(w0_ref[...] + _dot(jnp.tanh(wd), w2_ref[...], HIGHEST))
    softplus = jnp.maximum(nz, 0.0) + jnp.log(1.0 + jnp.exp(-jnp.abs(nz)))
    log_decay = -jnp.exp(-softplus - 0.5)
    a_sig = _sigmoid(a0_ref[...] + _dot(ad, a2_ref[...], HIGHEST))
    g = _dot(_sigmoid(gd), g2_ref[...], HIGHEST)
    kk = k * kk_ref[...]
    ss = _dot(kk * kk, ones_ref[...], HIGHEST)
    kk = kk / jnp.maximum(jnp.sqrt(ss), 1e-12)
    k_mod = k * (1.0 + (a_sig - 1.0) * ka_ref[...])
    kb = kk * a_sig
    hi = log_decay.astype(BF16)
    rem = log_decay - hi.astype(F32)
    mid = rem.astype(BF16)
    lo = (rem - mid.astype(F32)).astype(BF16)
    tri = tri_ref[...]
    cum = _dot(tri, hi) + _dot(tri, mid) + _dot(tri, lo)
    cum_end = jnp.concatenate(
        [jnp.broadcast_to(cum[(i + 1) * c - 1:(i + 1) * c, :], (c, GW)) for i in range(rows // c)], axis=0)
    e_cum = jnp.exp(cum)
    e_inv = jnp.exp(-cum)
    e_rem = jnp.exp(cum_end - cum)
    coef = _dot(r * k_mod * rk_ref[...], ones_ref[...], HIGHEST)
    outs = ((at_o, -kk * jnp.exp(cum - log_decay)), (rt_o, r * e_cum), (bt_o, kb * e_inv), (kt_o, k_mod * e_inv),
            (be_o, kb * e_rem), (ke_o, k_mod * e_rem), (v_o, v), (ee_o, jnp.exp(cum_end)), (g_o, g),
            (bg_o, coef * v * g))
    for ref, val in outs:
        for h in range(N_HEADS):
            ref[0, h] = val[:, h * HEAD_DIM:(h + 1) * HEAD_DIM].astype(ref.dtype)


def _rwkv_prep(proj3, mu, w0, w2, a0, a2, g2, k_k, k_a, r_k, rows=256):
    b, s, _ = proj3.shape
    rows = min(rows, s)
    pad = lambda w, n: jnp.concatenate([w, jnp.zeros((n - w.shape[0],) + w.shape[1:], w.dtype)], axis=0)
    head_of = np.arange(GW) // HEAD_DIM
    ones_bd = jnp.asarray((head_of[:, None] == head_of[None, :]).astype(np.float32))
    tok = np.arange(rows)
    tri_bd = jnp.asarray(((tok[:, None] // RWKV_CHUNK == tok[None, :] // RWKV_CHUNK)
                          & (tok[None, :] <= tok[:, None])).astype(np.float32), dtype=BF16)
    vec = lambda a: a.reshape(1, -1)
    full = lambda shape: pl.BlockSpec(shape, lambda i, j: (0,) * len(shape))
    col = COL_D // RWKV_W
    sub = rows // 8
    sd = lambda dt: jax.ShapeDtypeStruct((b, N_HEADS, s, HEAD_DIM), dt)
    return pl.pallas_call(
        functools.partial(_rwkv_prep_kernel, rows=rows),
        out_shape=[sd(BF16)] * 7 + [sd(F32)] * 3,
        grid=(b, s // rows),
        in_specs=[pl.BlockSpec((1, rows, RWKV_W), lambda i, j: (i, j, col)),
                  pl.BlockSpec((1, 8, RWKV_W), lambda i, j: (i, jnp.maximum(j * sub - 1, 0), col)),
                  full((1, RWKV_W)), full((1, GW)), full((LORA_PAD, GW)), full((1, GW)),
                  full((LORA_PAD, GW)), full((G_LORA, GW)), full((1, GW)), full((1, GW)), full((1, GW)),
                  full((GW, GW)), full((rows, rows))],
        out_specs=[pl.BlockSpec((1, N_HEADS, rows, HEAD_DIM), lambda i, j: (i, 0, j, 0))] * 10,
        compiler_params=_cparams(("arbitrary", "arbitrary")),
        name="rwkv_prep",
    )(proj3, proj3, vec(mu), vec(w0), pad(w2, LORA_PAD), vec(a0), pad(a2, LORA_PAD), g2, vec(k_k), vec(k_a),
      vec(r_k), ones_bd, tri_bd)


def _mm(a, b):
    return _dot(a.astype(BF16), b.astype(BF16))


def _mm_tn(a, b):
    return _dot_tn(a.astype(BF16), b.astype(BF16))


def _split2(x):
    hi = x.astype(BF16)
    return hi, (x - hi.astype(F32)).astype(BF16)


def _rwkv_chunk_terms(probs, masks):
    strict, incl, same_sub, eye = masks
    c = RWKV_CHUNK
    ident = eye.astype(F32)
    each = lambda f, *ls: [f(*xs) for xs in zip(*ls)]
    at, rt, bt, kt, b_end, k_end, v, e_end = [list(x) for x in zip(*probs)]
    ar = each(lambda a, r: jnp.concatenate([a, r], axis=0), at, rt)
    gb = each(_dot_nt, ar, bt)
    gk = each(_dot_nt, ar, kt)
    a_ab = [jnp.where(strict, g[:c], 0.0) for g in gb]
    a_rb = [jnp.where(incl, g[c:], 0.0).astype(BF16) for g in gb]
    a_kr = [jnp.concatenate([jnp.where(strict, g[:c], 0.0), jnp.where(incl, g[c:], 0.0)], axis=0).astype(BF16)
            for g in gk]
    ad = [jnp.where(same_sub, a, 0.0) for a in a_ab]
    an = [(a - d).astype(BF16) for a, d in zip(a_ab, ad)]
    adb = [d.astype(BF16) for d in ad]
    p2 = each(_dot, adb, adb)
    av = each(_dot, a_kr, v)
    p2b = [p.astype(BF16) for p in p2]
    p4 = each(_dot, p2b, p2b)
    x1 = each(lambda d, p: _mm(ident + d, ident + p), ad, p2)
    p4b = [p.astype(BF16) for p in p4]
    p8 = each(_dot, p4b, p4b)
    x2 = each(lambda p, q: _mm(ident + p, ident + q), p4, p8)
    td = [t.astype(BF16) for t in each(_mm, x1, x2)]
    m1 = each(_dot, td, an)
    m1b = [m.astype(BF16) for m in m1]
    m2 = each(_dot, m1b, m1b)
    x3 = each(lambda m, n: _mm(ident + m, ident + n), m1, m2)
    t_inv = [t.astype(BF16) for t in each(lambda x, t: _dot(x.astype(BF16), t), x3, td)]
    a_hat = [a.astype(BF16) for a in each(_dot, t_inv, at)]
    u0 = [u.astype(BF16) for u in each(lambda t, a: _dot(t, a[:c].astype(BF16)), t_inv, av)]
    r_hat = each(lambda r, a, h: r.astype(F32) + _dot(a, h), rt, a_rb, a_hat)
    o0 = each(lambda a, u, w: _dot(a, u) + w[c:], a_rb, u0, av)
    p_mat = each(lambda e, h, b: jnp.where(eye, e, 0.0) + _dot_tn(h, b), e_end, a_hat, b_end)
    z_mat = each(lambda u, b, w, k: _dot_tn(u, b) + _dot_tn(w, k), u0, b_end, v, k_end)
    return list(zip(p_mat, z_mat, r_hat, o0))


def _rwkv_scan_kernel(at_ref, rt_ref, bt_ref, kt_ref, be_ref, ke_ref, v_ref, ee_ref, g_ref, bg_ref, lng_ref,
                      lnb_ref, o_ref, state, *, rows):
    c = RWKV_CHUNK
    ri = lax.broadcasted_iota(jnp.int32, (c, c), 0)
    ci = lax.broadcasted_iota(jnp.int32, (c, c), 1)
    masks = (ci < ri, ci <= ri, (ri // RWKV_SUB) == (ci // RWKV_SUB), ri == ci)

    @pl.when(pl.program_id(1) == 0)
    def _():
        state[...] = jnp.zeros_like(state)

    chunks = [slice(i * c, (i + 1) * c) for i in range(rows // c)]
    seq_refs = (at_ref, rt_ref, bt_ref, kt_ref, be_ref, ke_ref, v_ref)
    probs = [tuple(ref[0, h, rs, :] for ref in seq_refs) + (ee_ref[0, h, rs.start:rs.start + 1, :],)
             for rs in chunks for h in range(N_HEADS)]
    terms = _rwkv_chunk_terms(probs, masks)
    s_cur = [state[h] for h in range(N_HEADS)]
    outs = []
    for i, rs in enumerate(chunks):
        row = []
        for h in range(N_HEADS):
            p_mat, z_mat, r_hat, o0 = terms[i * N_HEADS + h]
            s_hi, s_lo = _split2(s_cur[h])
            p_hi, p_lo = _split2(p_mat)
            o = _dot_nt(r_hat.astype(BF16), s_hi) + o0
            s_cur[h] = _dot(s_hi, p_hi) + _dot(s_hi, p_lo) + _dot(s_lo, p_hi) + z_mat
            mean = jnp.mean(o, axis=-1, keepdims=True)
            oc = o - mean
            var = jnp.mean(oc * oc, axis=-1, keepdims=True)
            y = oc * lax.rsqrt(var + RWKV_LN_EPS) * lng_ref[h:h + 1, :] + lnb_ref[h:h + 1, :]
            row.append(y * g_ref[0, h, rs, :] + bg_ref[0, h, rs, :])
        outs.append(jnp.concatenate(row, axis=-1))
    for h in range(N_HEADS):
        state[h] = s_cur[h]
    o_ref[0] = jnp.concatenate(outs, axis=0)


def _rwkv_scan(prep, lnx_g, lnx_b, rows=256):
    b, _, s, _ = prep[0].shape
    rows = min(rows, s)
    seq_spec = pl.BlockSpec((1, N_HEADS, rows, HEAD_DIM), lambda i, t: (i, 0, t, 0))
    par_spec = pl.BlockSpec((N_HEADS, HEAD_DIM), lambda i, t: (0, 0))
    par = lambda a: a.reshape(N_HEADS, HEAD_DIM)
    return pl.pallas_call(
        functools.partial(_rwkv_scan_kernel, rows=rows),
        out_shape=jax.ShapeDtypeStruct((b, s, GW), F32),
        grid=(b, s // rows),
        in_specs=[seq_spec] * 10 + [par_spec] * 2,
        out_specs=pl.BlockSpec((1, rows, GW), lambda i, t: (i, t, 0)),
        scratch_shapes=[pltpu.VMEM((N_HEADS, HEAD_DIM, HEAD_DIM), F32)],
        compiler_params=_cparams(("arbitrary", "arbitrary")),
        name="rwkv_scan",
    )(*prep, par(lnx_g), par(lnx_b))


def _outproj_kernel(ya_ref, yb_ref, yc_ref, yd_ref, g_ref, w_ref, x_ref, o_ref):
    acc = x_ref[...]
    for i, y_ref in enumerate((ya_ref, yb_ref, yc_ref, yd_ref)):
        yn = _rms(y_ref[...], g_ref[i:i + 1, :]).astype(BF16)
        acc = acc + _dot(yn, w_ref[i * GW:(i + 1) * GW, :])
    o_ref[...] = acc


def _outproj(ys, g, w, x2d, tm=512):
    t, d = x2d.shape
    tm = min(tm, t)
    y_spec = pl.BlockSpec((tm, GW), lambda i: (i, 0))
    return pl.pallas_call(
        _outproj_kernel,
        out_shape=jax.ShapeDtypeStruct((t, d), F32),
        grid=(t // tm,),
        in_specs=[y_spec] * 4 + [pl.BlockSpec((4, GW), lambda i: (0, 0)),
                                 pl.BlockSpec((4 * GW, d), lambda i: (0, 0)),
                                 pl.BlockSpec((tm, d), lambda i: (i, 0))],
        out_specs=pl.BlockSpec((tm, d), lambda i: (i, 0)),
        compiler_params=_cparams(("arbitrary",)),
        name="outproj",
    )(*[y.reshape(t, GW) for y in ys], g.reshape(4, GW), w, x2d)


def _ffn_kernel(x_ref, g_ref, wg_ref, wu_ref, wd_ref, gf_ref, o_ref, h_scr, acc_scr, *, final_norm):
    j = pl.program_id(1)

    @pl.when(j == 0)
    def _():
        h_scr[...] = _rms(x_ref[...], g_ref[...]).astype(BF16)
        acc_scr[...] = x_ref[...]

    h = h_scr[...]
    gate = _dot(h, wg_ref[...])
    up = _dot(h, wu_ref[...])
    act = (gate * _sigmoid(gate) * up).astype(BF16)
    acc_scr[...] += _dot(act, wd_ref[...])

    @pl.when(j == pl.num_programs(1) - 1)
    def _():
        y = acc_scr[...]
        o_ref[...] = _rms(y, gf_ref[...]) if final_norm else y


def _ffn(x2d, g, wg, wu, wd, g_final, final_norm, tm=512, tf=512):
    t, d = x2d.shape
    f = wg.shape[1]
    tm = min(tm, t)
    return pl.pallas_call(
        functools.partial(_ffn_kernel, final_norm=final_norm),
        out_shape=jax.ShapeDtypeStruct((t, d), F32),
        grid=(t // tm, f // tf),
        in_specs=[pl.BlockSpec((tm, d), lambda i, j: (i, 0)),
                  pl.BlockSpec((1, d), lambda i, j: (0, 0)),
                  pl.BlockSpec((d, tf), lambda i, j: (0, j)),
                  pl.BlockSpec((d, tf), lambda i, j: (0, j)),
                  pl.BlockSpec((tf, d), lambda i, j: (j, 0)),
                  pl.BlockSpec((1, d), lambda i, j: (0, 0))],
        out_specs=pl.BlockSpec((tm, d), lambda i, j: (i, 0)),
        scratch_shapes=[pltpu.VMEM((tm, d), BF16), pltpu.VMEM((tm, d), F32)],
        compiler_params=_cparams(("arbitrary", "arbitrary")),
        name="ffn",
    )(x2d, g.reshape(1, d), wg, wu, wd, g_final.reshape(1, d))


def _reorder_in_proj(w_in, mu):
    d0 = 3 * GW + 2 * GW + 3 * GW
    sizes = (GW, W_LORA, GW, GW, A_LORA, G_LORA)
    offs = np.concatenate([[0], np.cumsum(sizes)])
    piece = lambda a, i: a[..., int(offs[i]):int(offs[i + 1])]
    zeros = lambda a, n: jnp.zeros(a.shape[:-1] + (n,), a.dtype)

    def reorder(a):
        return jnp.concatenate([piece(a, 0), piece(a, 2), piece(a, 3),
                                piece(a, 1), zeros(a, LORA_PAD - W_LORA),
                                piece(a, 4), zeros(a, LORA_PAD - A_LORA), piece(a, 5)], axis=-1)

    w = jnp.concatenate([w_in[:, :d0], reorder(w_in[:, d0:])], axis=-1)
    return w.astype(BF16), reorder(mu)


def kernel(x, norm_mix_g, w_in, pos_bias, sgu_ln_g, sgu_w, sgu_b, rwkv_mu, rwkv_w0, rwkv_w2, rwkv_a0, rwkv_a2,
           rwkv_g2, rwkv_k_k, rwkv_k_a, rwkv_r_k, rwkv_lnx_g, rwkv_lnx_b, branch_norm_g, w_out, norm_ffn_g,
           w_gate, w_up, w_down, norm_final_g):
    b, s, d = x.shape
    depth = w_in.shape[0]
    assert s % (DIL_BLOCK * DIL_PATTERNS[-1][1]) == 0 and s % MOBA_BLOCK == 0
    bias_a = _bias_tiles(pos_bias, _dil_bucket_tiles(), 0)
    bias_c = _bias_tiles(pos_bias, _moba_bucket_tiles(s // MOBA_BLOCK), N_HEADS)
    x2d = x.reshape(b * s, d)
    for l in range(depth):
        w_l, mu_l = _reorder_in_proj(w_in[l], rwkv_mu[l])
        proj3 = _inproj(x2d, norm_mix_g[l], w_l).reshape(b, s, D_PROJ)
        ya = _mixer_dilated(proj3, bias_a)
        yb = _mixer_sgu(proj3, sgu_ln_g[l], sgu_w[l], sgu_b[l])
        yc = _mixer_moba(proj3, bias_c)
        prep = _rwkv_prep(proj3, mu_l, rwkv_w0[l], rwkv_w2[l], rwkv_a0[l], rwkv_a2[l], rwkv_g2[l],
                          rwkv_k_k[l], rwkv_k_a[l], rwkv_r_k[l])
        yd = _rwkv_scan(prep, rwkv_lnx_g[l], rwkv_lnx_b[l])
        x2d = _outproj((ya, yb, yc, yd), branch_norm_g[l], w_out[l].astype(BF16), x2d)
        x2d = _ffn(x2d, norm_ffn_g[l], w_gate[l].astype(BF16), w_up[l].astype(BF16), w_down[l].astype(BF16),
                   norm_final_g, final_norm=(l == depth - 1))
    return x2d.reshape(b, s, d)
```

```python
import functools
import math

import jax
import jax.numpy as jnp
import numpy as np
from jax import lax
from jax.experimental import pallas as pl
from jax.experimental.pallas import tpu as pltpu

F32 = jnp.float32
BF16 = jnp.bfloat16

HEAD_DIM = 64
N_HEADS = 8
GW = N_HEADS * HEAD_DIM
LANES = 128
N_PAIRS = GW // LANES
DIL_PATTERNS = ((128, 1), (512, 4), (2048, 16))
DIL_BLOCK = 128
SGU_CHUNK = 128
SGU_LN_EPS = 1e-5
MOBA_BLOCK = 256
MOBA_TOPK = 3
W_LORA = 96
A_LORA = 96
G_LORA = 256
LORA_PAD = 128
RWKV_LN_EPS = 64e-5
RWKV_CHUNK = 64
RWKV_SUB = 16
NUM_BUCKETS = 32
MAX_DISTANCE = 2048
NORM_EPS = 1e-6
NEG_INF = -1e30
ATT_SCALE = HEAD_DIM ** -0.5

COL_A = 0
COL_B = 3 * GW
COL_C = COL_B + 2 * GW
COL_D = COL_C + 3 * GW
RWKV_W = 3 * GW + 2 * LORA_PAD + G_LORA
D_PROJ = COL_D + RWKV_W

VMEM_LIMIT = 56 * 1024 * 1024

HIGHEST = lax.Precision.HIGHEST


def _cparams(sem):
    return pltpu.CompilerParams(dimension_semantics=sem, vmem_limit_bytes=VMEM_LIMIT)


def _dot(a, b, precision=None):
    return lax.dot_general(a, b, (((1,), (0,)), ((), ())), precision=precision,
                           preferred_element_type=F32)


def _dot_nt(a, b, precision=None):
    return lax.dot_general(a, b, (((1,), (1,)), ((), ())), precision=precision,
                           preferred_element_type=F32)


def _dot_tn(a, b, precision=None):
    return lax.dot_general(a, b, (((0,), (0,)), ((), ())), precision=precision,
                           preferred_element_type=F32)


def _rms(x, g):
    return x * lax.rsqrt(jnp.mean(x * x, axis=-1, keepdims=True) + NORM_EPS) * g


def _t5_bucket_np(dist):
    dist = np.maximum(dist, 0)
    max_exact = NUM_BUCKETS // 2
    d = np.maximum(dist, 1).astype(np.float32)
    large = max_exact + (np.log(d / np.float32(max_exact)) / np.float32(math.log(MAX_DISTANCE / max_exact))
                         * np.float32(NUM_BUCKETS - max_exact)).astype(np.int32)
    large = np.minimum(large, NUM_BUCKETS - 1)
    return np.where(dist < max_exact, dist, large).astype(np.int32)


def _dil_bucket_tiles():
    qa = np.arange(DIL_BLOCK)[:, None]
    kj = np.arange(DIL_BLOCK)[None, :]
    tiles = []
    for _, dil in DIL_PATTERNS:
        tiles.append(_t5_bucket_np((qa - kj) * dil))
        tiles.append(_t5_bucket_np((qa + DIL_BLOCK - kj) * dil))
    return np.stack(tiles)


def _moba_bucket_tiles(nblk):
    ki = np.arange(MOBA_BLOCK)[:, None]
    qi = np.arange(MOBA_BLOCK)[None, :]
    return np.stack([_t5_bucket_np(db * MOBA_BLOCK + qi - ki) for db in range(nblk)])


def _bias_tile_kernel(tbl_ref, idx_ref, o_ref, *, head_offset):
    h = pl.program_id(0) + head_offset
    idx = idx_ref[0]
    acc = jnp.zeros(idx.shape, F32)
    for b in range(NUM_BUCKETS):
        acc = jnp.where(idx == b, tbl_ref[b, h], acc)
    o_ref[0, 0] = acc


def _bias_tiles(pos_bias, idx_np, head_offset):
    nt, r, c = idx_np.shape
    return pl.pallas_call(
        functools.partial(_bias_tile_kernel, head_offset=head_offset),
        out_shape=jax.ShapeDtypeStruct((N_HEADS, nt, r, c), F32),
        grid=(N_HEADS, nt),
        in_specs=[pl.BlockSpec(memory_space=pltpu.SMEM),
                  pl.BlockSpec((1, r, c), lambda h, t: (t, 0, 0))],
        out_specs=pl.BlockSpec((1, 1, r, c), lambda h, t: (h, t, 0, 0)),
        compiler_params=_cparams(("arbitrary", "arbitrary")),
        name="bias_tiles",
    )(pos_bias, jnp.asarray(idx_np))


def _inproj_kernel(x_ref, g_ref, w_ref, o_ref, h_scr):
    @pl.when(pl.program_id(1) == 0)
    def _():
        h_scr[...] = _rms(x_ref[...], g_ref[...]).astype(BF16)

    o_ref[...] = _dot(h_scr[...], w_ref[...])


def _inproj(x2d, g, w, tm=1024, tn=1024):
    t, d = x2d.shape
    n = w.shape[1]
    tm = min(tm, t)
    return pl.pallas_call(
        _inproj_kernel,
        out_shape=jax.ShapeDtypeStruct((t, n), F32),
        grid=(t // tm, n // tn),
        in_specs=[pl.BlockSpec((tm, d), lambda i, j: (i, 0)),
                  pl.BlockSpec((1, d), lambda i, j: (0, 0)),
                  pl.BlockSpec((d, tn), lambda i, j: (0, j))],
        out_specs=pl.BlockSpec((tm, tn), lambda i, j: (i, j)),
        scratch_shapes=[pltpu.VMEM((tm, d), BF16)],
        compiler_params=_cparams(("arbitrary", "arbitrary")),
        name="inproj",
    )(x2d, g.reshape(1, d), w)


def _dilated_kernel(q_ref, k_ref, v_ref, bias_ref, o_ref, m0, m1, l0, l1, acc, *, seq):
    c = DIL_BLOCK
    lane = lax.broadcasted_iota(jnp.int32, (c, LANES), 1)
    head0 = lane < HEAD_DIM
    row = lax.broadcasted_iota(jnp.int32, (c, c), 0)
    col = lax.broadcasted_iota(jnp.int32, (c, c), 1)
    cur_valid = col <= row
    prev_valid = col >= row

    m0[...] = jnp.full((seq, LANES), NEG_INF, F32)
    m1[...] = jnp.full((seq, LANES), NEG_INF, F32)
    l0[...] = jnp.zeros((seq, LANES), F32)
    l1[...] = jnp.zeros((seq, LANES), F32)
    acc[...] = jnp.zeros((seq, LANES), F32)

    stats = ((m0, l0), (m1, l1))
    head_lanes = (head0, ~head0)

    def tiles(pairs, bias_idx, valid):
        q = [q_ref[0, qsl, :] * ATT_SCALE for qsl, _ in pairs]
        kb = [k_ref[0, ksl, :].astype(BF16) for _, ksl in pairs]
        vb = [v_ref[0, ksl, :].astype(BF16) for _, ksl in pairs]
        idx = [(h, i) for i in range(len(pairs)) for h in range(2)]
        s = {(h, i): _dot_nt(jnp.where(head_lanes[h], q[i], 0.0).astype(BF16), kb[i]) for h, i in idx}
        s = {hi: jnp.where(valid, s[hi] + bias_ref[hi[0], bias_idx], NEG_INF) for hi in idx}
        m_old = {(h, i): stats[h][0][pairs[i][0], :] for h, i in idx}
        m_new = {hi: jnp.maximum(m_old[hi], jnp.max(s[hi], axis=1, keepdims=True)) for hi in idx}
        alpha = {hi: jnp.exp(m_old[hi] - m_new[hi]) for hi in idx}
        p = {hi: jnp.exp(s[hi] - m_new[hi]) for hi in idx}
        for h, i in idx:
            m_ref, l_ref = stats[h]
            qsl = pairs[i][0]
            l_ref[qsl, :] = alpha[h, i] * l_ref[qsl, :] + jnp.sum(p[h, i], axis=1, keepdims=True)
            m_ref[qsl, :] = m_new[h, i]
        o = {(h, i): _dot(p[h, i].astype(BF16), vb[i]) for h, i in idx}
        for i, (qsl, _) in enumerate(pairs):
            acc[qsl, :] = (acc[qsl, :] * jnp.where(head0, alpha[0, i], alpha[1, i])
                           + jnp.where(head0, o[0, i], o[1, i]))

    def group_size(n):
        return max(g for g in range(1, 9) if n % g == 0)

    for pi, (_, dil) in enumerate(DIL_PATTERNS):
        sub_len = seq // dil
        nb = sub_len // c

        def sl(r, n, dil=dil):
            start = r + n * (c * dil)
            if dil == 1:
                return pl.ds(pl.multiple_of(start, c), c)
            return pl.ds(start, c, stride=dil)

        n_cur = dil * nb
        g_cur = group_size(n_cur)

        def cur_body(it, carry, nb=nb, sl=sl, pi=pi, g=g_cur):
            pairs = []
            for j in range(g):
                i = it * g + j
                pairs.append((sl(i // nb, i % nb),) * 2)
            tiles(pairs, 2 * pi, cur_valid)
            return carry

        lax.fori_loop(0, n_cur // g_cur, cur_body, 0)
        if nb > 1:
            n_prev = dil * (nb - 1)
            g_prev = group_size(n_prev)

            def prev_body(it, carry, nb=nb, sl=sl, pi=pi, g=g_prev):
                pairs = []
                for j in range(g):
                    i = it * g + j
                    r, n = i // (nb - 1), i % (nb - 1) + 1
                    pairs.append((sl(r, n), sl(r, n - 1)))
                tiles(pairs, 2 * pi + 1, prev_valid)
                return carry

            lax.fori_loop(0, n_prev // g_prev, prev_body, 0)

    o_ref[0] = acc[...] / jnp.where(head0[:1], l0[...], l1[...])


def _mixer_dilated(proj3, bias_a):
    b, s, _ = proj3.shape
    blk = lambda off: pl.BlockSpec((1, s, LANES), lambda p, i, off=off: (i, 0, off + p))
    return pl.pallas_call(
        functools.partial(_dilated_kernel, seq=s),
        out_shape=jax.ShapeDtypeStruct((b, s, GW), F32),
        grid=(N_PAIRS, b),
        in_specs=[blk(COL_A // LANES), blk((COL_A + GW) // LANES), blk((COL_A + 2 * GW) // LANES),
                  pl.BlockSpec((2, 2 * len(DIL_PATTERNS), DIL_BLOCK, DIL_BLOCK),
                               lambda p, i: (p, 0, 0, 0))],
        out_specs=pl.BlockSpec((1, s, LANES), lambda p, i: (i, 0, p)),
        scratch_shapes=[pltpu.VMEM((s, LANES), F32)] * 5,
        compiler_params=_cparams(("arbitrary", "arbitrary")),
        name="mixer_dilated",
    )(proj3, proj3, proj3, bias_a)


def _gelu_tanh(x):
    return 0.5 * x * (1.0 + jnp.tanh(math.sqrt(2.0 / math.pi) * (x + 0.044715 * (x * x * x))))


def _sgu_kernel(u_ref, v_ref, lng_ref, w_ref, bias_ref, o_ref, *, rows):
    t = SGU_CHUNK
    u = _gelu_tanh(u_ref[0])
    v = _gelu_tanh(v_ref[0])
    mu = jnp.mean(v, axis=-1, keepdims=True)
    vc = v - mu
    var = jnp.mean(vc * vc, axis=-1, keepdims=True)
    vn = (vc * lax.rsqrt(var + SGU_LN_EPS) * lng_ref[...]).astype(BF16)
    r2 = lax.broadcasted_iota(jnp.int32, (2 * t, t), 0)
    c2 = lax.broadcasted_iota(jnp.int32, (2 * t, t), 1)
    causal = c2 <= jnp.where(r2 >= t, r2 - t, r2)
    first_group = lax.broadcasted_iota(jnp.int32, (t, LANES), 1) < HEAD_DIM
    for p in range(N_PAIRS):
        wp = jnp.where(causal, w_ref[p], 0.0).astype(BF16)
        for ci in range(rows // t):
            rs = slice(ci * t, (ci + 1) * t)
            cs = slice(p * LANES, (p + 1) * LANES)
            res = _dot(wp, vn[rs, cs])
            mixed = jnp.where(first_group, res[:t], res[t:]) + bias_ref[:, cs]
            o_ref[0, rs, cs] = u[rs, cs] * mixed


def _mixer_sgu(proj3, ln_g, w_s, b_s, rows=512):
    b, s, _ = proj3.shape
    rows = min(rows, s)
    t = SGU_CHUNK
    bias_full = jnp.repeat(b_s.T, HEAD_DIM, axis=1)
    w_pairs = w_s.reshape(N_PAIRS, 2 * t, t)
    return pl.pallas_call(
        functools.partial(_sgu_kernel, rows=rows),
        out_shape=jax.ShapeDtypeStruct((b, s, GW), F32),
        grid=(b, s // rows),
        in_specs=[pl.BlockSpec((1, rows, GW), lambda i, j: (i, j, COL_B // GW)),
                  pl.BlockSpec((1, rows, GW), lambda i, j: (i, j, COL_B // GW + 1)),
                  pl.BlockSpec((1, GW), lambda i, j: (0, 0)),
                  pl.BlockSpec((N_PAIRS, 2 * t, t), lambda i, j: (0, 0, 0)),
                  pl.BlockSpec((t, GW), lambda i, j: (0, 0))],
        out_specs=pl.BlockSpec((1, rows, GW), lambda i, j: (i, j, 0)),
        compiler_params=_cparams(("arbitrary", "arbitrary")),
        name="mixer_sgu",
    )(proj3, proj3, ln_g.reshape(1, GW), w_pairs, bias_full)


def _moba_kernel(q_ref, k_ref, v_ref, bias_ref, o_ref, kh_scr, vt_scr, ot_scr, *, seq):
    bs = MOBA_BLOCK
    nblk = seq // bs
    lane = lax.broadcasted_iota(jnp.int32, (1, LANES), 1)
    head_lanes = (lane < HEAD_DIM, lane >= HEAD_DIM)
    blk = lambda i: slice(i * bs, (i + 1) * bs)

    q_all = q_ref[0]
    kbar = jnp.concatenate([jnp.mean(k_ref[0, blk(j), :], axis=0, keepdims=True) for j in range(nblk)], axis=0)
    for j in range(nblk):
        kj = k_ref[0, blk(j), :]
        for h in range(2):
            kh_scr[h, j] = jnp.where(head_lanes[h], kj, 0.0).astype(BF16)
        vt_scr[j] = v_ref[0, blk(j), :].T.astype(BF16)

    jrow = lax.broadcasted_iota(jnp.int32, (nblk, seq), 0)
    own = lax.broadcasted_iota(jnp.int32, (nblk, seq), 1) // bs
    krow = lax.broadcasted_iota(jnp.int32, (bs, bs), 0)
    qcol = lax.broadcasted_iota(jnp.int32, (bs, bs), 1)
    causal = krow <= qcol

    sel = []
    for h in range(2):
        gate = _dot_nt(jnp.where(head_lanes[h], kbar, 0.0), q_all, precision=HIGHEST)
        gate = jnp.where(jrow < own, gate, NEG_INF)
        rank = jnp.zeros((nblk, seq), jnp.int32)
        for j2 in range(nblk):
            gj = gate[j2:j2 + 1, :]
            ahead = (gj > gate) | ((gj == gate) & (j2 < jrow))
            rank = rank + ahead.astype(jnp.int32)
        sel.append((rank < MOBA_TOPK) & (jrow < own))

    for grp in range(nblk // 2):
        probs = [(h, qb) for qb in (grp, nblk - 1 - grp) for h in range(2)]
        tiles = [(h, qb, j) for h, qb in probs for j in range(qb + 1)]
        qh = {qb: (q_ref[0, blk(qb), :] * ATT_SCALE).astype(BF16) for _, qb in probs}
        s = {(h, qb, j): _dot_nt(kh_scr[h, j], qh[qb]) for h, qb, j in tiles}
        for h, qb, j in tiles:
            valid = causal if j == qb else sel[h][j:j + 1, blk(qb)]
            s[h, qb, j] = jnp.where(valid, s[h, qb, j] + bias_ref[h, qb - j], NEG_INF)
        m = {}
        for h, qb in probs:
            m[h, qb] = functools.reduce(
                jnp.maximum, [jnp.max(s[h, qb, j], axis=0, keepdims=True) for j in range(qb + 1)])
        p = {t: jnp.exp(s[t] - m[t[0], t[1]]) for t in tiles}
        for h, qb in probs:
            vrows = slice(h * HEAD_DIM, (h + 1) * HEAD_DIM)
            l = sum(jnp.sum(p[h, qb, j], axis=0, keepdims=True) for j in range(qb + 1))
            acc = sum(_dot(vt_scr[j, vrows, :], p[h, qb, j].astype(BF16)) for j in range(qb + 1))
            ot_scr[qb, vrows, :] = acc / l

    for qb in range(nblk):
        o_ref[0, blk(qb), :] = ot_scr[qb].T


def _mixer_moba(proj3, bias_c):
    b, s, _ = proj3.shape
    nblk = s // MOBA_BLOCK
    blk = lambda off: pl.BlockSpec((1, s, LANES), lambda p, i, off=off: (i, 0, off + p))
    return pl.pallas_call(
        functools.partial(_moba_kernel, seq=s),
        out_shape=jax.ShapeDtypeStruct((b, s, GW), F32),
        grid=(N_PAIRS, b),
        in_specs=[blk(COL_C // LANES), blk((COL_C + GW) // LANES), blk((COL_C + 2 * GW) // LANES),
                  pl.BlockSpec((2, nblk, MOBA_BLOCK, MOBA_BLOCK), lambda p, i: (p, 0, 0, 0))],
        out_specs=pl.BlockSpec((1, s, LANES), lambda p, i: (i, 0, p)),
        scratch_shapes=[pltpu.VMEM((2, nblk, MOBA_BLOCK, LANES), BF16),
                        pltpu.VMEM((nblk, LANES, MOBA_BLOCK), BF16),
                        pltpu.VMEM((nblk, LANES, MOBA_BLOCK), F32)],
        compiler_params=_cparams(("arbitrary", "arbitrary")),
        name="mixer_moba",
    )(proj3, proj3, proj3, bias_c)


def _sigmoid(x):
    return 1.0 / (1.0 + jnp.exp(-x))


def _rwkv_prep_kernel(p_ref, prev_ref, mu_ref, w0_ref, w2_ref, a0_ref, a2_ref, g2_ref, kk_ref, ka_ref, rk_ref,
                      ones_ref, tri_ref, at_o, rt_o, bt_o, kt_o, be_o, ke_o, v_o, ee_o, g_o, bg_o, *, rows):
    c = RWKV_CHUNK
    p = p_ref[0]
    prev_row = jnp.where(pl.program_id(1) == 0, 0.0, prev_ref[0, 7:8, :])
    first_row = lax.broadcasted_iota(jnp.int32, (rows, 1), 0) == 0
    y_prev = jnp.where(first_row, prev_row, pltpu.roll(p, 1, axis=0))
    xs = p + (y_prev - p) * mu_ref[...]
    r = xs[:, 0:GW]
    k = xs[:, GW:2 * GW]
    v = xs[:, 2 * GW:3 * GW]
    wd = xs[:, 3 * GW:3 * GW + LORA_PAD]
    ad = xs[:, 3 * GW + LORA_PAD:3 * GW + 2 * LORA_PAD]
    gd = xs[:, 3 * GW + 2 * LORA_PAD:]
    nz = -(w0_ref[...] + _dot(jnp.tanh(wd), w2_ref[...], HIGHEST))
    softplus = jnp.maximum(nz, 0.0) + jnp.log(1.0 + jnp.exp(-jnp.abs(nz)))
    log_decay = -jnp.exp(-softplus - 0.5)
    a_sig = _sigmoid(a0_ref[...] + _dot(ad, a2_ref[...], HIGHEST))
    g = _dot(_sigmoid(gd), g2_ref[...], HIGHEST)
    kk = k * kk_ref[...]
    ss = _dot(kk * kk, ones_ref[...], HIGHEST)
    kk = kk / jnp.maximum(jnp.sqrt(ss), 1e-12)
    k_mod = k * (1.0 + (a_sig - 1.0) * ka_ref[...])
    kb = kk * a_sig
    hi = log_decay.astype(BF16)
    rem = log_decay - hi.astype(F32)
    mid = rem.astype(BF16)
    lo = (rem - mid.astype(F32)).astype(BF16)
    tri = tri_ref[...]
    cum = _dot(tri, hi) + _dot(tri, mid) + _dot(tri, lo)
    cum_end = jnp.concatenate(
        [jnp.broadcast_to(cum[(i + 1) * c - 1:(i + 1) * c, :], (c, GW)) for i in range(rows // c)], axis=0)
    e_cum = jnp.exp(cum)
    e_inv = jnp.exp(-cum)
    e_rem = jnp.exp(cum_end - cum)
    coef = _dot(r * k_mod * rk_ref[...], ones_ref[...], HIGHEST)
    outs = ((at_o, -kk * jnp.exp(cum - log_decay)), (rt_o, r * e_cum), (bt_o, kb * e_inv), (kt_o, k_mod * e_inv),
            (be_o, kb * e_rem), (ke_o, k_mod * e_rem), (v_o, v), (ee_o, jnp.exp(cum_end)), (g_o, g),
            (bg_o, coef * v * g))
    for ref, val in outs:
        for h in range(N_HEADS):
            ref[0, h] = val[:, h * HEAD_DIM:(h + 1) * HEAD_DIM].astype(ref.dtype)


def _rwkv_prep(proj3, mu, w0, w2, a0, a2, g2, k_k, k_a, r_k, rows=256):
    b, s, _ = proj3.shape
    rows = min(rows, s)
    pad = lambda w, n: jnp.concatenate([w, jnp.zeros((n - w.shape[0],) + w.shape[1:], w.dtype)], axis=0)
    head_of = np.arange(GW) // HEAD_DIM
    ones_bd = jnp.asarray((head_of[:, None] == head_of[None, :]).astype(np.float32))
    tok = np.arange(rows)
    tri_bd = jnp.asarray(((tok[:, None] // RWKV_CHUNK == tok[None, :] // RWKV_CHUNK)
                          & (tok[None, :] <= tok[:, None])).astype(np.float32), dtype=BF16)
    vec = lambda a: a.reshape(1, -1)
    full = lambda shape: pl.BlockSpec(shape, lambda i, j: (0,) * len(shape))
    col = COL_D // RWKV_W
    sub = rows // 8
    sd = lambda dt: jax.ShapeDtypeStruct((b, N_HEADS, s, HEAD_DIM), dt)
    return pl.pallas_call(
        functools.partial(_rwkv_prep_kernel, rows=rows),
        out_shape=[sd(BF16)] * 7 + [sd(F32)] * 3,
        grid=(b, s // rows),
        in_specs=[pl.BlockSpec((1, rows, RWKV_W), lambda i, j: (i, j, col)),
                  pl.BlockSpec((1, 8, RWKV_W), lambda i, j: (i, jnp.maximum(j * sub - 1, 0), col)),
                  full((1, RWKV_W)), full((1, GW)), full((LORA_PAD, GW)), full((1, GW)),
                  full((LORA_PAD, GW)), full((G_LORA, GW)), full((1, GW)), full((1, GW)), full((1, GW)),
                  full((GW, GW)), full((rows, rows))],
        out_specs=[pl.BlockSpec((1, N_HEADS, rows, HEAD_DIM), lambda i, j: (i, 0, j, 0))] * 10,
        compiler_params=_cparams(("arbitrary", "arbitrary")),
        name="rwkv_prep",
    )(proj3, proj3, vec(mu), vec(w0), pad(w2, LORA_PAD), vec(a0), pad(a2, LORA_PAD), g2, vec(k_k), vec(k_a),
      vec(r_k), ones_bd, tri_bd)


def _mm(a, b):
    return _dot(a.astype(BF16), b.astype(BF16))


def _mm_tn(a, b):
    return _dot_tn(a.astype(BF16), b.astype(BF16))


def _split2(x):
    hi = x.astype(BF16)
    return hi, (x - hi.astype(F32)).astype(BF16)


def _rwkv_chunk_terms(probs, masks):
    strict, incl, same_sub, eye = masks
    c = RWKV_CHUNK
    ident = eye.astype(F32)
    each = lambda f, *ls: [f(*xs) for xs in zip(*ls)]
    at, rt, bt, kt, b_end, k_end, v, e_end = [list(x) for x in zip(*probs)]
    ar = each(lambda a, r: jnp.concatenate([a, r], axis=0), at, rt)
    gb = each(_dot_nt, ar, bt)
    gk = each(_dot_nt, ar, kt)
    a_ab = [jnp.where(strict, g[:c], 0.0) for g in gb]
    a_rb = [jnp.where(incl, g[c:], 0.0).astype(BF16) for g in gb]
    a_kr = [jnp.concatenate([jnp.where(strict, g[:c], 0.0), jnp.where(incl, g[c:], 0.0)], axis=0).astype(BF16)
            for g in gk]
    ad = [jnp.where(same_sub, a, 0.0) for a in a_ab]
    an = [(a - d).astype(BF16) for a, d in zip(a_ab, ad)]
    adb = [d.astype(BF16) for d in ad]
    p2 = each(_dot, adb, adb)
    av = each(_dot, a_kr, v)
    p2b = [p.astype(BF16) for p in p2]
    p4 = each(_dot, p2b, p2b)
    x1 = each(lambda d, p: _mm(ident + d, ident + p), ad, p2)
    p4b = [p.astype(BF16) for p in p4]
    p8 = each(_dot, p4b, p4b)
    x2 = each(lambda p, q: _mm(ident + p, ident + q), p4, p8)
    td = [t.astype(BF16) for t in each(_mm, x1, x2)]
    m1 = each(_dot, td, an)
    m1b = [m.astype(BF16) for m in m1]
    m2 = each(_dot, m1b, m1b)
    x3 = each(lambda m, n: _mm(ident + m, ident + n), m1, m2)
    t_inv = [t.astype(BF16) for t in each(lambda x, t: _dot(x.astype(BF16), t), x3, td)]
    a_hat = [a.astype(BF16) for a in each(_dot, t_inv, at)]
    u0 = [u.astype(BF16) for u in each(lambda t, a: _dot(t, a[:c].astype(BF16)), t_inv, av)]
    r_hat = each(lambda r, a, h: r.astype(F32) + _dot(a, h), rt, a_rb, a_hat)
    o0 = each(lambda a, u, w: _dot(a, u) + w[c:], a_rb, u0, av)
    p_mat = each(lambda e, h, b: jnp.where(eye, e, 0.0) + _dot_tn(h, b), e_end, a_hat, b_end)
    z_mat = each(lambda u, b, w, k: _dot_tn(u, b) + _dot_tn(w, k), u0, b_end, v, k_end)
    return list(zip(p_mat, z_mat, r_hat, o0))


def _rwkv_scan_kernel(at_ref, rt_ref, bt_ref, kt_ref, be_ref, ke_ref, v_ref, ee_ref, g_ref, bg_ref, lng_ref,
                      lnb_ref, o_ref, state, *, rows):
    c = RWKV_CHUNK
    ri = lax.broadcasted_iota(jnp.int32, (c, c), 0)
    ci = lax.broadcasted_iota(jnp.int32, (c, c), 1)
    masks = (ci < ri, ci <= ri, (ri // RWKV_SUB) == (ci // RWKV_SUB), ri == ci)

    @pl.when(pl.program_id(1) == 0)
    def _():
        state[...] = jnp.zeros_like(state)

    chunks = [slice(i * c, (i + 1) * c) for i in range(rows // c)]
    seq_refs = (at_ref, rt_ref, bt_ref, kt_ref, be_ref, ke_ref, v_ref)
    probs = [tuple(ref[0, h, rs, :] for ref in seq_refs) + (ee_ref[0, h, rs.start:rs.start + 1, :],)
             for rs in chunks for h in range(N_HEADS)]
    terms = _rwkv_chunk_terms(probs, masks)
    s_cur = [state[h] for h in range(N_HEADS)]
    outs = []
    for i, rs in enumerate(chunks):
        row = []
        for h in range(N_HEADS):
            p_mat, z_mat, r_hat, o0 = terms[i * N_HEADS + h]
            s_hi, s_lo = _split2(s_cur[h])
            p_hi, p_lo = _split2(p_mat)
            o = _dot_nt(r_hat.astype(BF16), s_hi) + o0
            s_cur[h] = _dot(s_hi, p_hi) + _dot(s_hi, p_lo) + _dot(s_lo, p_hi) + z_mat
            mean = jnp.mean(o, axis=-1, keepdims=True)
            oc = o - mean
            var = jnp.mean(oc * oc, axis=-1, keepdims=True)
            y = oc * lax.rsqrt(var + RWKV_LN_EPS) * lng_ref[h:h + 1, :] + lnb_ref[h:h + 1, :]
            row.append(y * g_ref[0, h, rs, :] + bg_ref[0, h, rs, :])
        outs.append(jnp.concatenate(row, axis=-1))
    for h in range(N_HEADS):
        state[h] = s_cur[h]
    o_ref[0] = jnp.concatenate(outs, axis=0)


def _rwkv_scan(prep, lnx_g, lnx_b, rows=256):
    b, _, s, _ = prep[0].shape
    rows = min(rows, s)
    seq_spec = pl.BlockSpec((1, N_HEADS, rows, HEAD_DIM), lambda i, t: (i, 0, t, 0))
    par_spec = pl.BlockSpec((N_HEADS, HEAD_DIM), lambda i, t: (0, 0))
    par = lambda a: a.reshape(N_HEADS, HEAD_DIM)
    return pl.pallas_call(
        functools.partial(_rwkv_scan_kernel, rows=rows),
        out_shape=jax.ShapeDtypeStruct((b, s, GW), F32),
        grid=(b, s // rows),
        in_specs=[seq_spec] * 10 + [par_spec] * 2,
        out_specs=pl.BlockSpec((1, rows, GW), lambda i, t: (i, t, 0)),
        scratch_shapes=[pltpu.VMEM((N_HEADS, HEAD_DIM, HEAD_DIM), F32)],
        compiler_params=_cparams(("arbitrary", "arbitrary")),
        name="rwkv_scan",
    )(*prep, par(lnx_g), par(lnx_b))


def _outproj_kernel(ya_ref, yb_ref, yc_ref, yd_ref, g_ref, w_ref, x_ref, o_ref):
    acc = x_ref[...]
    for i, y_ref in enumerate((ya_ref, yb_ref, yc_ref, yd_ref)):
        yn = _rms(y_ref[...], g_ref[i:i + 1, :]).astype(BF16)
        acc = acc + _dot(yn, w_ref[i * GW:(i + 1) * GW, :])
    o_ref[...] = acc


def _outproj(ys, g, w, x2d, tm=512):
    t, d = x2d.shape
    tm = min(tm, t)
    y_spec = pl.BlockSpec((tm, GW), lambda i: (i, 0))
    return pl.pallas_call(
        _outproj_kernel,
        out_shape=jax.ShapeDtypeStruct((t, d), F32),
        grid=(t // tm,),
        in_specs=[y_spec] * 4 + [pl.BlockSpec((4, GW), lambda i: (0, 0)),
                                 pl.BlockSpec((4 * GW, d), lambda i: (0, 0)),
                                 pl.BlockSpec((tm, d), lambda i: (i, 0))],
        out_specs=pl.BlockSpec((tm, d), lambda i: (i, 0)),
        compiler_params=_cparams(("arbitrary",)),
        name="outproj",
    )(*[y.reshape(t, GW) for y in ys], g.reshape(4, GW), w, x2d)


def _ffn_kernel(x_ref, g_ref, wg_ref, wu_ref, wd_ref, gf_ref, o_ref, h_scr, acc_scr, *, final_norm):
    j = pl.program_id(1)

    @pl.when(j == 0)
    def _():
        h_scr[...] = _rms(x_ref[...], g_ref[...]).astype(BF16)
        acc_scr[...] = x_ref[...]

    h = h_scr[...]
    gate = _dot(h, wg_ref[...])
    up = _dot(h, wu_ref[...])
    act = (gate * _sigmoid(gate) * up).astype(BF16)
    acc_scr[...] += _dot(act, wd_ref[...])

    @pl.when(j == pl.num_programs(1) - 1)
    def _():
        y = acc_scr[...]
        o_ref[...] = _rms(y, gf_ref[...]) if final_norm else y


def _ffn(x2d, g, wg, wu, wd, g_final, final_norm, tm=512, tf=512):
    t, d = x2d.shape
    f = wg.shape[1]
    tm = min(tm, t)
    return pl.pallas_call(
        functools.partial(_ffn_kernel, final_norm=final_norm),
        out_shape=jax.ShapeDtypeStruct((t, d), F32),
        grid=(t // tm, f // tf),
        in_specs=[pl.BlockSpec((tm, d), lambda i, j: (i, 0)),
                  pl.BlockSpec((1, d), lambda i, j: (0, 0)),
                  pl.BlockSpec((d, tf), lambda i, j: (0, j)),
                  pl.BlockSpec((d, tf), lambda i, j: (0, j)),
                  pl.BlockSpec((tf, d), lambda i, j: (j, 0)),
                  pl.BlockSpec((1, d), lambda i, j: (0, 0))],
        out_specs=pl.BlockSpec((tm, d), lambda i, j: (i, 0)),
        scratch_shapes=[pltpu.VMEM((tm, d), BF16), pltpu.VMEM((tm, d), F32)],
        compiler_params=_cparams(("arbitrary", "arbitrary")),
        name="ffn",
    )(x2d, g.reshape(1, d), wg, wu, wd, g_final.reshape(1, d))


def _reorder_in_proj(w_in, mu):
    d0 = 3 * GW + 2 * GW + 3 * GW
    sizes = (GW, W_LORA, GW, GW, A_LORA, G_LORA)
    offs = np.concatenate([[0], np.cumsum(sizes)])
    piece = lambda a, i: a[..., int(offs[i]):int(offs[i + 1])]
    zeros = lambda a, n: jnp.zeros(a.shape[:-1] + (n,), a.dtype)

    def reorder(a):
        return jnp.concatenate([piece(a, 0), piece(a, 2), piece(a, 3),
                                piece(a, 1), zeros(a, LORA_PAD - W_LORA),
                                piece(a, 4), zeros(a, LORA_PAD - A_LORA), piece(a, 5)], axis=-1)

    w = jnp.concatenate([w_in[:, :d0], reorder(w_in[:, d0:])], axis=-1)
    return w.astype(BF16), reorder(mu)


def kernel(x, norm_mix_g, w_in, pos_bias, sgu_ln_g, sgu_w, sgu_b, rwkv_mu, rwkv_w0, rwkv_w2, rwkv_a0, rwkv_a2,
           rwkv_g2, rwkv_k_k, rwkv_k_a, rwkv_r_k, rwkv_lnx_g, rwkv_lnx_b, branch_norm_g, w_out, norm_ffn_g,
           w_gate, w_up, w_down, norm_final_g):
    b, s, d = x.shape
    depth = w_in.shape[0]
    assert s % (DIL_BLOCK * DIL_PATTERNS[-1][1]) == 0 and s % MOBA_BLOCK == 0
    bias_a = _bias_tiles(pos_bias, _dil_bucket_tiles(), 0)
    bias_c = _bias_tiles(pos_bias, _moba_bucket_tiles(s // MOBA_BLOCK), N_HEADS)
    x2d = x.reshape(b * s, d)
    for l in range(depth):
        w_l, mu_l = _reorder_in_proj(w_in[l], rwkv_mu[l])
        proj3 = _inproj(x2d, norm_mix_g[l], w_l).reshape(b, s, D_PROJ)
        ya = _mixer_dilated(proj3, bias_a)
        yb = _mixer_sgu(proj3, sgu_ln_g[l], sgu_w[l], sgu_b[l])
        yc = _mixer_moba(proj3, bias_c)
        prep = _rwkv_prep(proj3, mu_l, rwkv_w0[l], rwkv_w2[l], rwkv_a0[l], rwkv_a2[l], rwkv_g2[l],
                          rwkv_k_k[l], rwkv_k_a[l], rwkv_r_k[l])
        yd = _rwkv_scan(prep, rwkv_lnx_g[l], rwkv_lnx_b[l])
        x2d = _outproj((ya, yb, yc, yd), branch_norm_g[l], w_out[l].astype(BF16), x2d)
        x2d = _ffn(x2d, norm_ffn_g[l], w_gate[l].astype(BF16), w_up[l].astype(BF16), w_down[l].astype(BF16),
                   norm_final_g, final_norm=(l == depth - 1))
    return x2d.reshape(b, s, d)
```

```python
import functools
import math

import jax
import jax.numpy as jnp
import numpy as np
from jax import lax
from jax.experimental import pallas as pl
from jax.experimental.pallas import tpu as pltpu

F32 = jnp.float32
BF16 = jnp.bfloat16

HEAD_DIM = 64
N_HEADS = 8
GW = N_HEADS * HEAD_DIM
LANES = 128
N_PAIRS = GW // LANES
DIL_PATTERNS = ((128, 1), (512, 4), (2048, 16))
DIL_BLOCK = 128
SGU_CHUNK = 128
SGU_LN_EPS = 1e-5
MOBA_BLOCK = 256
MOBA_TOPK = 3
W_LORA = 96
A_LORA = 96
G_LORA = 256
LORA_PAD = 128
RWKV_LN_EPS = 64e-5
RWKV_CHUNK = 64
RWKV_SUB = 16
NUM_BUCKETS = 32
MAX_DISTANCE = 2048
NORM_EPS = 1e-6
NEG_INF = -1e30
ATT_SCALE = HEAD_DIM ** -0.5
LOG2E = math.log2(math.e)

COL_A = 0
COL_B = 3 * GW
COL_C = COL_B + 2 * GW
COL_D = COL_C + 3 * GW
RWKV_W = 3 * GW + 2 * LORA_PAD + G_LORA
D_PROJ = COL_D + RWKV_W

VMEM_LIMIT = 56 * 1024 * 1024

HIGHEST = lax.Precision.HIGHEST


def _cparams(sem):
    return pltpu.CompilerParams(dimension_semantics=sem, vmem_limit_bytes=VMEM_LIMIT)


def _dot(a, b, precision=None):
    return lax.dot_general(a, b, (((1,), (0,)), ((), ())), precision=precision,
                           preferred_element_type=F32)


def _dot_nt(a, b, precision=None):
    return lax.dot_general(a, b, (((1,), (1,)), ((), ())), precision=precision,
                           preferred_element_type=F32)


def _dot_tn(a, b, precision=None):
    return lax.dot_general(a, b, (((0,), (0,)), ((), ())), precision=precision,
                           preferred_element_type=F32)


def _rms(x, g):
    return x * lax.rsqrt(jnp.mean(x * x, axis=-1, keepdims=True) + NORM_EPS) * g


def _t5_bucket_np(dist):
    dist = np.maximum(dist, 0)
    max_exact = NUM_BUCKETS // 2
    d = np.maximum(dist, 1).astype(np.float32)
    large = max_exact + (np.log(d / np.float32(max_exact)) / np.float32(math.log(MAX_DISTANCE / max_exact))
                         * np.float32(NUM_BUCKETS - max_exact)).astype(np.int32)
    large = np.minimum(large, NUM_BUCKETS - 1)
    return np.where(dist < max_exact, dist, large).astype(np.int32)


def _dil_bucket_tiles():
    qa = np.arange(DIL_BLOCK)[:, None]
    kj = np.arange(DIL_BLOCK)[None, :]
    tiles = []
    for _, dil in DIL_PATTERNS:
        tiles.append(_t5_bucket_np((qa - kj) * dil))
        tiles.append(_t5_bucket_np((qa + DIL_BLOCK - kj) * dil))
    return np.stack(tiles)


def _moba_bucket_tiles(nblk):
    ki = np.arange(MOBA_BLOCK)[:, None]
    qi = np.arange(MOBA_BLOCK)[None, :]
    return np.stack([_t5_bucket_np(db * MOBA_BLOCK + qi - ki) for db in range(nblk)])


def _bias_tile_kernel(tbl_ref, idx_ref, o_ref, *, head_offset):
    h = pl.program_id(0) + head_offset
    idx = idx_ref[0]
    acc = jnp.zeros(idx.shape, F32)
    for b in range(NUM_BUCKETS):
        acc = jnp.where(idx == b, tbl_ref[b, h] * LOG2E, acc)
    o_ref[0, 0] = acc


def _bias_tiles(pos_bias, idx_np, head_offset):
    nt, r, c = idx_np.shape
    return pl.pallas_call(
        functools.partial(_bias_tile_kernel, head_offset=head_offset),
        out_shape=jax.ShapeDtypeStruct((N_HEADS, nt, r, c), F32),
        grid=(N_HEADS, nt),
        in_specs=[pl.BlockSpec(memory_space=pltpu.SMEM),
                  pl.BlockSpec((1, r, c), lambda h, t: (t, 0, 0))],
        out_specs=pl.BlockSpec((1, 1, r, c), lambda h, t: (h, t, 0, 0)),
        compiler_params=_cparams(("arbitrary", "arbitrary")),
        name="bias_tiles",
    )(pos_bias, jnp.asarray(idx_np))


def _inproj_kernel(x_ref, g_ref, w_ref, o_ref, h_scr):
    @pl.when(pl.program_id(1) == 0)
    def _():
        h_scr[...] = _rms(x_ref[...], g_ref[...]).astype(BF16)

    o_ref[...] = _dot(h_scr[...], w_ref[...])


def _inproj(x2d, g, w, tm=1024, tn=1024):
    t, d = x2d.shape
    n = w.shape[1]
    tm = min(tm, t)
    return pl.pallas_call(
        _inproj_kernel,
        out_shape=jax.ShapeDtypeStruct((t, n), F32),
        grid=(t // tm, n // tn),
        in_specs=[pl.BlockSpec((tm, d), lambda i, j: (i, 0)),
                  pl.BlockSpec((1, d), lambda i, j: (0, 0)),
                  pl.BlockSpec((d, tn), lambda i, j: (0, j))],
        out_specs=pl.BlockSpec((tm, tn), lambda i, j: (i, j)),
        scratch_shapes=[pltpu.VMEM((tm, d), BF16)],
        compiler_params=_cparams(("arbitrary", "arbitrary")),
        name="inproj",
    )(x2d, g.reshape(1, d), w)


def _dilated_kernel(q_ref, k_ref, v_ref, bias_ref, o_ref, m0, m1, l0, l1, acc, *, seq):
    c = DIL_BLOCK
    lane = lax.broadcasted_iota(jnp.int32, (c, LANES), 1)
    head0 = lane < HEAD_DIM
    row = lax.broadcasted_iota(jnp.int32, (c, c), 0)
    col = lax.broadcasted_iota(jnp.int32, (c, c), 1)
    cur_valid = col <= row
    prev_valid = col >= row

    stats = ((m0, l0), (m1, l1))
    ones = jnp.ones((c, LANES), BF16)
    head_lanes = (head0, ~head0)

    def tiles(pairs, bias_idx, valid, first):
        q = [q_ref[0, qsl, :] * (ATT_SCALE * LOG2E) for qsl, _ in pairs]
        kb = [k_ref[0, ksl, :].astype(BF16) for _, ksl in pairs]
        vb = [jnp.concatenate([v_ref[0, ksl, :].astype(BF16), ones], axis=1) for _, ksl in pairs]
        idx = [(h, i) for i in range(len(pairs)) for h in range(2)]
        s = {(h, i): _dot_nt(jnp.where(head_lanes[h], q[i], 0.0).astype(BF16), kb[i]) for h, i in idx}
        s = {hi: jnp.where(valid, s[hi] + bias_ref[hi[0], bias_idx], NEG_INF) for hi in idx}
        m_new = {hi: jnp.max(s[hi], axis=1, keepdims=True) for hi in idx}
        if not first:
            m_old = {(h, i): stats[h][0][pairs[i][0], :] for h, i in idx}
            m_new = {hi: jnp.maximum(m_old[hi], m_new[hi]) for hi in idx}
            alpha = {hi: jnp.exp2(m_old[hi] - m_new[hi]) for hi in idx}
        p = {hi: jnp.exp2(s[hi] - m_new[hi]).astype(BF16) for hi in idx}
        o = {(h, i): _dot(p[h, i], vb[i]) for h, i in idx}
        for h, i in idx:
            m_ref, l_ref = stats[h]
            qsl = pairs[i][0]
            l_new = o[h, i][:, LANES:]
            l_ref[qsl, :] = l_new if first else alpha[h, i] * l_ref[qsl, :] + l_new
            m_ref[qsl, :] = jnp.broadcast_to(m_new[h, i], (c, LANES))
        for i, (qsl, _) in enumerate(pairs):
            o_new = jnp.where(head0, o[0, i][:, :LANES], o[1, i][:, :LANES])
            acc[qsl, :] = o_new if first else acc[qsl, :] * jnp.where(head0, alpha[0, i], alpha[1, i]) + o_new

    def group_size(n):
        return max(g for g in range(1, 9) if n % g == 0)

    order = sorted(range(len(DIL_PATTERNS)), key=lambda i: -DIL_PATTERNS[i][1])
    for pi in order:
        dil = DIL_PATTERNS[pi][1]
        first_pass = pi == order[0]
        sub_len = seq // dil
        nb = sub_len // c

        def sl(r, n, dil=dil):
            start = r + n * (c * dil)
            if dil == 1:
                return pl.ds(pl.multiple_of(start, c), c)
            return pl.ds(start, c, stride=dil)

        n_cur = dil * nb
        g_cur = group_size(n_cur)

        def cur_body(it, carry, nb=nb, sl=sl, pi=pi, g=g_cur, first=first_pass):
            pairs = []
            for j in range(g):
                i = it * g + j
                pairs.append((sl(i // nb, i % nb),) * 2)
            tiles(pairs, 2 * pi, cur_valid, first)
            return carry

        lax.fori_loop(0, n_cur // g_cur, cur_body, 0)
        if nb > 1:
            n_prev = dil * (nb - 1)
            g_prev = group_size(n_prev)

            def prev_body(it, carry, nb=nb, sl=sl, pi=pi, g=g_prev):
                pairs = []
                for j in range(g):
                    i = it * g + j
                    r, n = i // (nb - 1), i % (nb - 1) + 1
                    pairs.append((sl(r, n), sl(r, n - 1)))
                tiles(pairs, 2 * pi + 1, prev_valid, False)
                return carry

            lax.fori_loop(0, n_prev // g_prev, prev_body, 0)

    o_ref[0] = acc[...] / jnp.where(head0[:1], l0[...], l1[...])


def _mixer_dilated(proj3, bias_a):
    b, s, _ = proj3.shape
    blk = lambda off: pl.BlockSpec((1, s, LANES), lambda p, i, off=off: (i, 0, off + p))
    return pl.pallas_call(
        functools.partial(_dilated_kernel, seq=s),
        out_shape=jax.ShapeDtypeStruct((b, s, GW), F32),
        grid=(N_PAIRS, b),
        in_specs=[blk(COL_A // LANES), blk((COL_A + GW) // LANES), blk((COL_A + 2 * GW) // LANES),
                  pl.BlockSpec((2, 2 * len(DIL_PATTERNS), DIL_BLOCK, DIL_BLOCK),
                               lambda p, i: (p, 0, 0, 0))],
        out_specs=pl.BlockSpec((1, s, LANES), lambda p, i: (i, 0, p)),
        scratch_shapes=[pltpu.VMEM((s, LANES), F32)] * 5,
        compiler_params=_cparams(("arbitrary", "arbitrary")),
        name="mixer_dilated",
    )(proj3, proj3, proj3, bias_a)


def _gelu_tanh(x):
    return 0.5 * x * (1.0 + jnp.tanh(math.sqrt(2.0 / math.pi) * (x + 0.044715 * (x * x * x))))


def _sgu_kernel(u_ref, v_ref, lng_ref, w_ref, bias_ref, o_ref, *, rows):
    t = SGU_CHUNK
    u = _gelu_tanh(u_ref[0])
    v = _gelu_tanh(v_ref[0])
    mu = jnp.mean(v, axis=-1, keepdims=True)
    vc = v - mu
    var = jnp.mean(vc * vc, axis=-1, keepdims=True)
    vn = (vc * lax.rsqrt(var + SGU_LN_EPS) * lng_ref[...]).astype(BF16)
    r2 = lax.broadcasted_iota(jnp.int32, (2 * t, t), 0)
    c2 = lax.broadcasted_iota(jnp.int32, (2 * t, t), 1)
    causal = c2 <= jnp.where(r2 >= t, r2 - t, r2)
    first_group = lax.broadcasted_iota(jnp.int32, (t, LANES), 1) < HEAD_DIM
    for p in range(N_PAIRS):
        wp = jnp.where(causal, w_ref[p], 0.0).astype(BF16)
        for ci in range(rows // t):
            rs = slice(ci * t, (ci + 1) * t)
            cs = slice(p * LANES, (p + 1) * LANES)
            res = _dot(wp, vn[rs, cs])
            mixed = jnp.where(first_group, res[:t], res[t:]) + bias_ref[:, cs]
            o_ref[0, rs, cs] = u[rs, cs] * mixed


def _mixer_sgu(proj3, ln_g, w_s, b_s, rows=512):
    b, s, _ = proj3.shape
    rows = min(rows, s)
    t = SGU_CHUNK
    bias_full = jnp.repeat(b_s.T, HEAD_DIM, axis=1)
    w_pairs = w_s.reshape(N_PAIRS, 2 * t, t)
    return pl.pallas_call(
        functools.partial(_sgu_kernel, rows=rows),
        out_shape=jax.ShapeDtypeStruct((b, s, GW), F32),
        grid=(b, s // rows),
        in_specs=[pl.BlockSpec((1, rows, GW), lambda i, j: (i, j, COL_B // GW)),
                  pl.BlockSpec((1, rows, GW), lambda i, j: (i, j, COL_B // GW + 1)),
                  pl.BlockSpec((1, GW), lambda i, j: (0, 0)),
                  pl.BlockSpec((N_PAIRS, 2 * t, t), lambda i, j: (0, 0, 0)),
                  pl.BlockSpec((t, GW), lambda i, j: (0, 0))],
        out_specs=pl.BlockSpec((1, rows, GW), lambda i, j: (i, j, 0)),
        compiler_params=_cparams(("arbitrary", "arbitrary")),
        name="mixer_sgu",
    )(proj3, proj3, ln_g.reshape(1, GW), w_pairs, bias_full)


def _moba_kernel(q_ref, k_ref, v_ref, bias_ref, o_ref, kh_scr, vt_scr, ot_scr, *, seq):
    ones_rows = 16
    bs = MOBA_BLOCK
    nblk = seq // bs
    lane = lax.broadcasted_iota(jnp.int32, (1, LANES), 1)
    head_lanes = (lane < HEAD_DIM, lane >= HEAD_DIM)
    blk = lambda i: slice(i * bs, (i + 1) * bs)

    q_all = q_ref[0]
    kbar = jnp.concatenate([jnp.mean(k_ref[0, blk(j), :], axis=0, keepdims=True) for j in range(nblk)], axis=0)
    for j in range(nblk):
        kj = k_ref[0, blk(j), :]
        for h in range(2):
            kh_scr[h, j] = jnp.where(head_lanes[h], kj, 0.0).astype(BF16)
        vt = v_ref[0, blk(j), :].T.astype(BF16)
        for h in range(2):
            vt_scr[j, h] = jnp.concatenate(
                [vt[h * HEAD_DIM:(h + 1) * HEAD_DIM], jnp.ones((ones_rows, bs), BF16)], axis=0)

    jrow = lax.broadcasted_iota(jnp.int32, (nblk, seq), 0)
    own = lax.broadcasted_iota(jnp.int32, (nblk, seq), 1) // bs
    krow = lax.broadcasted_iota(jnp.int32, (bs, bs), 0)
    qcol = lax.broadcasted_iota(jnp.int32, (bs, bs), 1)
    causal = krow <= qcol

    sel = []
    for h in range(2):
        gate = _dot_nt(jnp.where(head_lanes[h], kbar, 0.0), q_all, precision=HIGHEST)
        gate = jnp.where(jrow < own, gate, NEG_INF)
        rank = jnp.zeros((nblk, seq), jnp.int32)
        for j2 in range(nblk):
            gj = gate[j2:j2 + 1, :]
            ahead = (gj > gate) | ((gj == gate) & (j2 < jrow))
            rank = rank + ahead.astype(jnp.int32)
        sel.append((rank < MOBA_TOPK) & (jrow < own))

    for grp in range(nblk // 2):
        probs = [(h, qb) for qb in (grp, nblk - 1 - grp) for h in range(2)]
        tiles = [(h, qb, j) for h, qb in probs for j in range(qb + 1)]
        qh = {qb: (q_ref[0, blk(qb), :] * (ATT_SCALE * LOG2E)).astype(BF16) for _, qb in probs}
        s = {(h, qb, j): _dot_nt(kh_scr[h, j], qh[qb]) + bias_ref[h, qb - j] for h, qb, j in tiles}
        for h, qb in probs:
            s[h, qb, qb] = jnp.where(causal, s[h, qb, qb], NEG_INF)
        picked = {(h, qb, j): sel[h][j:j + 1, blk(qb)] for h, qb, j in tiles if j != qb}
        cmax = {t: jnp.max(s[t], axis=0, keepdims=True) for t in tiles}
        m = {}
        for h, qb in probs:
            m[h, qb] = functools.reduce(
                jnp.maximum, [cmax[h, qb, qb]] + [jnp.where(picked[h, qb, j], cmax[h, qb, j], NEG_INF)
                                                  for j in range(qb)])
        shift = {t: m[t[0], t[1]] if t[2] == t[1] else jnp.where(picked[t], m[t[0], t[1]], -NEG_INF)
                 for t in tiles}
        p = {t: jnp.exp2(s[t] - shift[t]).astype(BF16) for t in tiles}
        for h, qb in probs:
            acc = sum(_dot(vt_scr[j, h], p[h, qb, j]) for j in range(qb + 1))
            ot_scr[qb, h * HEAD_DIM:(h + 1) * HEAD_DIM, :] = acc[:HEAD_DIM] / acc[HEAD_DIM:HEAD_DIM + 1]

    for qb in range(nblk):
        o_ref[0, blk(qb), :] = ot_scr[qb].T


def _mixer_moba(proj3, bias_c):
    b, s, _ = proj3.shape
    nblk = s // MOBA_BLOCK
    blk = lambda off: pl.BlockSpec((1, s, LANES), lambda p, i, off=off: (i, 0, off + p))
    return pl.pallas_call(
        functools.partial(_moba_kernel, seq=s),
        out_shape=jax.ShapeDtypeStruct((b, s, GW), F32),
        grid=(N_PAIRS, b),
        in_specs=[blk(COL_C // LANES), blk((COL_C + GW) // LANES), blk((COL_C + 2 * GW) // LANES),
                  pl.BlockSpec((2, nblk, MOBA_BLOCK, MOBA_BLOCK), lambda p, i: (p, 0, 0, 0))],
        out_specs=pl.BlockSpec((1, s, LANES), lambda p, i: (i, 0, p)),
        scratch_shapes=[pltpu.VMEM((2, nblk, MOBA_BLOCK, LANES), BF16),
                        pltpu.VMEM((nblk, 2, HEAD_DIM + 16, MOBA_BLOCK), BF16),
                        pltpu.VMEM((nblk, LANES, MOBA_BLOCK), F32)],
        compiler_params=_cparams(("arbitrary", "arbitrary")),
        name="mixer_moba",
    )(proj3, proj3, proj3, bias_c)


def _sigmoid(x):
    return 1.0 / (1.0 + jnp.exp(-x))


def _split2(x):
    hi = x.astype(BF16)
    return hi, (x - hi.astype(F32)).astype(BF16)


def _head_sum(x, ones_bd):
    hi, lo = _split2(x)
    return _dot(hi, ones_bd) + _dot(lo, ones_bd)


def _rwkv_prep_kernel(p_ref, prev_ref, mu_ref, w0_ref, w2_ref, a0_ref, a2_ref, g2_ref, kk_ref, ka_ref, rk_ref,
                      ones_ref, tri_ref, at_o, rt_o, bt_o, kt_o, be_o, ke_o, v_o, ee_o, g_o, bg_o, *, rows):
    c = RWKV_CHUNK
    p = p_ref[0]
    prev_row = jnp.where(pl.program_id(1) == 0, 0.0, prev_ref[0, 7:8, :])
    first_row = lax.broadcasted_iota(jnp.int32, (rows, 1), 0) == 0
    y_prev = jnp.where(first_row, prev_row, pltpu.roll(p, 1, axis=0))
    xs = p + (y_prev - p) * mu_ref[...]
    r = xs[:, 0:GW]
    k = xs[:, GW:2 * GW]
    v = xs[:, 2 * GW:3 * GW]
    wd = xs[:, 3 * GW:3 * GW + LORA_PAD]
    ad = xs[:, 3 * GW + LORA_PAD:3 * GW + 2 * LORA_PAD]
    gd = xs[:, 3 * GW + 2 * LORA_PAD:]
    nz = -(w0_ref[...] + _dot(jnp.tanh(wd).astype(BF16), w2_ref[...]))
    softplus = jnp.maximum(nz, 0.0) + jnp.log(1.0 + jnp.exp(-jnp.abs(nz)))
    log_decay = -jnp.exp(-softplus - 0.5)
    a_sig = _sigmoid(a0_ref[...] + _dot(ad.astype(BF16), a2_ref[...]))
    g = _dot(_sigmoid(gd).astype(BF16), g2_ref[...])
    kk = k * kk_ref[...]
    ss = _head_sum(kk * kk, ones_ref[...])
    kk = kk / jnp.maximum(jnp.sqrt(ss), 1e-12)
    k_mod = k * (1.0 + (a_sig - 1.0) * ka_ref[...])
    kb = kk * a_sig
    hi = log_decay.astype(BF16)
    rem = log_decay - hi.astype(F32)
    mid = rem.astype(BF16)
    lo = (rem - mid.astype(F32)).astype(BF16)
    tri = tri_ref[...]
    cum = _dot(tri, hi) + _dot(tri, mid) + _dot(tri, lo)
    cum_end = jnp.concatenate(
        [jnp.broadcast_to(cum[(i + 1) * c - 1:(i + 1) * c, :], (c, GW)) for i in range(rows // c)], axis=0)
    e_cum = jnp.exp(cum)
    e_inv = jnp.exp(-cum)
    e_rem = jnp.exp(cum_end - cum)
    coef = _head_sum(r * k_mod * rk_ref[...], ones_ref[...])
    outs = ((at_o, -kk * jnp.exp(cum - log_decay)), (rt_o, r * e_cum), (bt_o, kb * e_inv), (kt_o, k_mod * e_inv),
            (be_o, kb * e_rem), (ke_o, k_mod * e_rem), (v_o, v), (ee_o, jnp.exp(cum_end)), (g_o, g),
            (bg_o, coef * v * g))
    for ref, val in outs:
        for h in range(N_HEADS):
            ref[0, h] = val[:, h * HEAD_DIM:(h + 1) * HEAD_DIM].astype(ref.dtype)


def _rwkv_prep(proj3, mu, w0, w2, a0, a2, g2, k_k, k_a, r_k, rows=256):
    b, s, _ = proj3.shape
    rows = min(rows, s)
    pad = lambda w, n: jnp.concatenate([w, jnp.zeros((n - w.shape[0],) + w.shape[1:], w.dtype)], axis=0)
    head_of = np.arange(GW) // HEAD_DIM
    ones_bd = jnp.asarray((head_of[:, None] == head_of[None, :]).astype(np.float32), dtype=BF16)
    tok = np.arange(rows)
    tri_bd = jnp.asarray(((tok[:, None] // RWKV_CHUNK == tok[None, :] // RWKV_CHUNK)
                          & (tok[None, :] <= tok[:, None])).astype(np.float32), dtype=BF16)
    vec = lambda a: a.reshape(1, -1)
    full = lambda shape: pl.BlockSpec(shape, lambda i, j: (0,) * len(shape))
    col = COL_D // RWKV_W
    sub = rows // 8
    sd = lambda dt: jax.ShapeDtypeStruct((b, N_HEADS, s, HEAD_DIM), dt)
    return pl.pallas_call(
        functools.partial(_rwkv_prep_kernel, rows=rows),
        out_shape=[sd(BF16)] * 7 + [sd(F32)] * 3,
        grid=(b, s // rows),
        in_specs=[pl.BlockSpec((1, rows, RWKV_W), lambda i, j: (i, j, col)),
                  pl.BlockSpec((1, 8, RWKV_W), lambda i, j: (i, jnp.maximum(j * sub - 1, 0), col)),
                  full((1, RWKV_W)), full((1, GW)), full((LORA_PAD, GW)), full((1, GW)),
                  full((LORA_PAD, GW)), full((G_LORA, GW)), full((1, GW)), full((1, GW)), full((1, GW)),
                  full((GW, GW)), full((rows, rows))],
        out_specs=[pl.BlockSpec((1, N_HEADS, rows, HEAD_DIM), lambda i, j: (i, 0, j, 0))] * 10,
        compiler_params=_cparams(("arbitrary", "arbitrary")),
        name="rwkv_prep",
    )(proj3, proj3, vec(mu), vec(w0), pad(w2, LORA_PAD).astype(BF16), vec(a0), pad(a2, LORA_PAD).astype(BF16),
      g2.astype(BF16), vec(k_k), vec(k_a), vec(r_k), ones_bd, tri_bd)


def _mm(a, b):
    return _dot(a.astype(BF16), b.astype(BF16))


def _rwkv_chunk_terms(probs, masks):
    strict, incl, same_sub, eye = masks
    c = RWKV_CHUNK
    ident = eye.astype(F32)
    each = lambda f, *ls: [f(*xs) for xs in zip(*ls)]
    at, rt, bt, kt, b_end, k_end, v, e_end = [list(x) for x in zip(*probs)]
    ar = each(lambda a, r: jnp.concatenate([a, r], axis=0), at, rt)
    gb = each(_dot_nt, ar, bt)
    gk = each(_dot_nt, ar, kt)
    a_ab = [jnp.where(strict, g[:c], 0.0) for g in gb]
    a_rb = [jnp.where(incl, g[c:], 0.0).astype(BF16) for g in gb]
    a_kr = [jnp.concatenate([jnp.where(strict, g[:c], 0.0), jnp.where(incl, g[c:], 0.0)], axis=0).astype(BF16)
            for g in gk]
    ad = [jnp.where(same_sub, a, 0.0) for a in a_ab]
    an = [(a - d).astype(BF16) for a, d in zip(a_ab, ad)]
    adb = [d.astype(BF16) for d in ad]
    p2 = each(_dot, adb, adb)
    av = each(_dot, a_kr, v)
    p2b = [p.astype(BF16) for p in p2]
    p4 = each(_dot, p2b, p2b)
    x1 = each(lambda d, p: _mm(ident + d, ident + p), ad, p2)
    p4b = [p.astype(BF16) for p in p4]
    p8 = each(_dot, p4b, p4b)
    x2 = each(lambda p, q: _mm(ident + p, ident + q), p4, p8)
    td = [t.astype(BF16) for t in each(_mm, x1, x2)]
    m1 = each(_dot, td, an)
    m1b = [m.astype(BF16) for m in m1]
    m2 = each(_dot, m1b, m1b)
    x3 = each(lambda m, n: _mm(ident + m, ident + n), m1, m2)
    t_inv = [t.astype(BF16) for t in each(lambda x, t: _dot(x.astype(BF16), t), x3, td)]
    a_hat = [a.astype(BF16) for a in each(_dot, t_inv, at)]
    u0 = [u.astype(BF16) for u in each(lambda t, a: _dot(t, a[:c].astype(BF16)), t_inv, av)]
    r_hat = each(lambda r, a, h: r.astype(F32) + _dot(a, h), rt, a_rb, a_hat)
    o0 = each(lambda a, u, w: _dot(a, u) + w[c:], a_rb, u0, av)
    p_mat = each(lambda e, h, b: jnp.where(eye, e, 0.0) + _dot_tn(h, b), e_end, a_hat, b_end)
    z_mat = each(lambda u, b, w, k: _dot_tn(u, b) + _dot_tn(w, k), u0, b_end, v, k_end)
    return list(zip(p_mat, z_mat, r_hat, o0))


def _rwkv_scan_kernel(at_ref, rt_ref, bt_ref, kt_ref, be_ref, ke_ref, v_ref, ee_ref, g_ref, bg_ref, lng_ref,
                      lnb_ref, o_ref, state, *, rows):
    c = RWKV_CHUNK
    ri = lax.broadcasted_iota(jnp.int32, (c, c), 0)
    ci = lax.broadcasted_iota(jnp.int32, (c, c), 1)
    masks = (ci < ri, ci <= ri, (ri // RWKV_SUB) == (ci // RWKV_SUB), ri == ci)

    @pl.when(pl.program_id(1) == 0)
    def _():
        state[...] = jnp.zeros_like(state)

    chunks = [slice(i * c, (i + 1) * c) for i in range(rows // c)]
    seq_refs = (at_ref, rt_ref, bt_ref, kt_ref, be_ref, ke_ref, v_ref)
    probs = [tuple(ref[0, h, rs, :] for ref in seq_refs) + (ee_ref[0, h, rs.start:rs.start + 1, :],)
             for rs in chunks for h in range(N_HEADS)]
    terms = _rwkv_chunk_terms(probs, masks)
    s_cur = [state[h] for h in range(N_HEADS)]
    outs = []
    for i, rs in enumerate(chunks):
        row = []
        for h in range(N_HEADS):
            p_mat, z_mat, r_hat, o0 = terms[i * N_HEADS + h]
            s_hi, s_lo = _split2(s_cur[h])
            p_hi, p_lo = _split2(p_mat)
            o = _dot_nt(r_hat.astype(BF16), s_hi) + o0
            s_cur[h] = _dot(s_hi, p_hi) + _dot(s_hi, p_lo) + _dot(s_lo, p_hi) + z_mat
            mean = jnp.mean(o, axis=-1, keepdims=True)
            oc = o - mean
            var = jnp.mean(oc * oc, axis=-1, keepdims=True)
            y = oc * lax.rsqrt(var + RWKV_LN_EPS) * lng_ref[h:h + 1, :] + lnb_ref[h:h + 1, :]
            row.append(y * g_ref[0, h, rs, :] + bg_ref[0, h, rs, :])
        outs.append(jnp.concatenate(row, axis=-1))
    for h in range(N_HEADS):
        state[h] = s_cur[h]
    o_ref[0] = jnp.concatenate(outs, axis=0)


def _rwkv_scan(prep, lnx_g, lnx_b, rows=256):
    b, _, s, _ = prep[0].shape
    rows = min(rows, s)
    seq_spec = pl.BlockSpec((1, N_HEADS, rows, HEAD_DIM), lambda i, t: (i, 0, t, 0))
    par_spec = pl.BlockSpec((N_HEADS, HEAD_DIM), lambda i, t: (0, 0))
    par = lambda a: a.reshape(N_HEADS, HEAD_DIM)
    return pl.pallas_call(
        functools.partial(_rwkv_scan_kernel, rows=rows),
        out_shape=jax.ShapeDtypeStruct((b, s, GW), F32),
        grid=(b, s // rows),
        in_specs=[seq_spec] * 10 + [par_spec] * 2,
        out_specs=pl.BlockSpec((1, rows, GW), lambda i, t: (i, t, 0)),
        scratch_shapes=[pltpu.VMEM((N_HEADS, HEAD_DIM, HEAD_DIM), F32)],
        compiler_params=_cparams(("arbitrary", "arbitrary")),
        name="rwkv_scan",
    )(*prep, par(lnx_g), par(lnx_b))


def _outproj_kernel(ya_ref, yb_ref, yc_ref, yd_ref, g_ref, w_ref, x_ref, o_ref):
    acc = x_ref[...]
    for i, y_ref in enumerate((ya_ref, yb_ref, yc_ref, yd_ref)):
        yn = _rms(y_ref[...], g_ref[i:i + 1, :]).astype(BF16)
        acc = acc + _dot(yn, w_ref[i * GW:(i + 1) * GW, :])
    o_ref[...] = acc


def _outproj(ys, g, w, x2d, tm=512):
    t, d = x2d.shape
    tm = min(tm, t)
    y_spec = pl.BlockSpec((tm, GW), lambda i: (i, 0))
    return pl.pallas_call(
        _outproj_kernel,
        out_shape=jax.ShapeDtypeStruct((t, d), F32),
        grid=(t // tm,),
        in_specs=[y_spec] * 4 + [pl.BlockSpec((4, GW), lambda i: (0, 0)),
                                 pl.BlockSpec((4 * GW, d), lambda i: (0, 0)),
                                 pl.BlockSpec((tm, d), lambda i: (i, 0))],
        out_specs=pl.BlockSpec((tm, d), lambda i: (i, 0)),
        compiler_params=_cparams(("arbitrary",)),
        name="outproj",
    )(*[y.reshape(t, GW) for y in ys], g.reshape(4, GW), w, x2d)


def _ffn_kernel(x_ref, g_ref, wg_ref, wu_ref, wd_ref, gf_ref, o_ref, h_scr, *, final_norm):
    j = pl.program_id(1)

    @pl.when(j == 0)
    def _():
        x = x_ref[...]
        h_scr[...] = _rms(x, g_ref[...]).astype(BF16)
        o_ref[...] = x

    h = h_scr[...]
    gate = _dot(h, wg_ref[...])
    up = _dot(h, wu_ref[...])
    act = (gate * _sigmoid(gate) * up).astype(BF16)
    o_ref[...] += _dot(act, wd_ref[...])

    if final_norm:
        @pl.when(j == pl.num_programs(1) - 1)
        def _():
            o_ref[...] = _rms(o_ref[...], gf_ref[...])


def _ffn(x2d, g, wg, wu, wd, g_final, final_norm, tm=1024, tf=512):
    t, d = x2d.shape
    f = wg.shape[1]
    tm = min(tm, t)
    return pl.pallas_call(
        functools.partial(_ffn_kernel, final_norm=final_norm),
        out_shape=jax.ShapeDtypeStruct((t, d), F32),
        grid=(t // tm, f // tf),
        in_specs=[pl.BlockSpec((tm, d), lambda i, j: (i, 0)),
                  pl.BlockSpec((1, d), lambda i, j: (0, 0)),
                  pl.BlockSpec((d, tf), lambda i, j: (0, j)),
                  pl.BlockSpec((d, tf), lambda i, j: (0, j)),
                  pl.BlockSpec((tf, d), lambda i, j: (j, 0)),
                  pl.BlockSpec((1, d), lambda i, j: (0, 0))],
        out_specs=pl.BlockSpec((tm, d), lambda i, j: (i, 0)),
        scratch_shapes=[pltpu.VMEM((tm, d), BF16)],
        compiler_params=_cparams(("arbitrary", "arbitrary")),
        name="ffn",
    )(x2d, g.reshape(1, d), wg, wu, wd, g_final.reshape(1, d))


def _reorder_in_proj(w_in, mu):
    d0 = 3 * GW + 2 * GW + 3 * GW
    sizes = (GW, W_LORA, GW, GW, A_LORA, G_LORA)
    offs = np.concatenate([[0], np.cumsum(sizes)])
    piece = lambda a, i: a[..., int(offs[i]):int(offs[i + 1])]
    zeros = lambda a, n: jnp.zeros(a.shape[:-1] + (n,), a.dtype)

    def reorder(a):
        return jnp.concatenate([piece(a, 0), piece(a, 2), piece(a, 3),
                                piece(a, 1), zeros(a, LORA_PAD - W_LORA),
                                piece(a, 4), zeros(a, LORA_PAD - A_LORA), piece(a, 5)], axis=-1)

    w = jnp.concatenate([w_in[:, :d0], reorder(w_in[:, d0:])], axis=-1)
    return w.astype(BF16), reorder(mu)


def kernel(x, norm_mix_g, w_in, pos_bias, sgu_ln_g, sgu_w, sgu_b, rwkv_mu, rwkv_w0, rwkv_w2, rwkv_a0, rwkv_a2,
           rwkv_g2, rwkv_k_k, rwkv_k_a, rwkv_r_k, rwkv_lnx_g, rwkv_lnx_b, branch_norm_g, w_out, norm_ffn_g,
           w_gate, w_up, w_down, norm_final_g):
    b, s, d = x.shape
    depth = w_in.shape[0]
    assert s % (DIL_BLOCK * DIL_PATTERNS[-1][1]) == 0 and s % (2 * MOBA_BLOCK) == 0
    bias_a = _bias_tiles(pos_bias, _dil_bucket_tiles(), 0)
    bias_c = _bias_tiles(pos_bias, _moba_bucket_tiles(s // MOBA_BLOCK), N_HEADS)
    x2d = x.reshape(b * s, d)
    for l in range(depth):
        w_l, mu_l = _reorder_in_proj(w_in[l], rwkv_mu[l])
        proj3 = _inproj(x2d, norm_mix_g[l], w_l).reshape(b, s, D_PROJ)
        ya = _mixer_dilated(proj3, bias_a)
        yb = _mixer_sgu(proj3, sgu_ln_g[l], sgu_w[l], sgu_b[l])
        yc = _mixer_moba(proj3, bias_c)
        prep = _rwkv_prep(proj3, mu_l, rwkv_w0[l], rwkv_w2[l], rwkv_a0[l], rwkv_a2[l], rwkv_g2[l],
                          rwkv_k_k[l], rwkv_k_a[l], rwkv_r_k[l])
        yd = _rwkv_scan(prep, rwkv_lnx_g[l], rwkv_lnx_b[l])
        x2d = _outproj((ya, yb, yc, yd), branch_norm_g[l], w_out[l].astype(BF16), x2d)
        x2d = _ffn(x2d, norm_ffn_g[l], w_gate[l].astype(BF16), w_up[l].astype(BF16), w_down[l].astype(BF16),
                   norm_final_g, final_norm=(l == depth - 1))
    return x2d.reshape(b, s, d)
```

```python
import functools
import math

import jax
import jax.numpy as jnp
import numpy as np
from jax import lax
from jax.experimental import pallas as pl
from jax.experimental.pallas import tpu as pltpu

F32 = jnp.float32
BF16 = jnp.bfloat16

HEAD_DIM = 64
N_HEADS = 8
GW = N_HEADS * HEAD_DIM
LANES = 128
N_PAIRS = GW // LANES
DIL_PATTERNS = ((128, 1), (512, 4), (2048, 16))
DIL_BLOCK = 128
SGU_CHUNK = 128
SGU_LN_EPS = 1e-5
MOBA_BLOCK = 256
MOBA_TOPK = 3
W_LORA = 96
A_LORA = 96
G_LORA = 256
LORA_PAD = 128
RWKV_LN_EPS = 64e-5
RWKV_CHUNK = 64
RWKV_SUB = 16
NUM_BUCKETS = 32
MAX_DISTANCE = 2048
NORM_EPS = 1e-6
NEG_INF = -1e30
ATT_SCALE = HEAD_DIM ** -0.5
LOG2E = math.log2(math.e)

COL_A = 0
COL_B = 3 * GW
COL_C = COL_B + 2 * GW
COL_D = COL_C + 3 * GW
RWKV_W = 3 * GW + 2 * LORA_PAD + G_LORA
D_PROJ = COL_D + RWKV_W

VMEM_LIMIT = 56 * 1024 * 1024

HIGHEST = lax.Precision.HIGHEST


def _cparams(sem):
    return pltpu.CompilerParams(dimension_semantics=sem, vmem_limit_bytes=VMEM_LIMIT)


def _dot(a, b, precision=None):
    return lax.dot_general(a, b, (((1,), (0,)), ((), ())), precision=precision,
                           preferred_element_type=F32)


def _dot_nt(a, b, precision=None):
    return lax.dot_general(a, b, (((1,), (1,)), ((), ())), precision=precision,
                           preferred_element_type=F32)


def _dot_tn(a, b, precision=None):
    return lax.dot_general(a, b, (((0,), (0,)), ((), ())), precision=precision,
                           preferred_element_type=F32)


def _rms(x, g):
    return x * lax.rsqrt(jnp.mean(x * x, axis=-1, keepdims=True) + NORM_EPS) * g


def _t5_bucket_np(dist):
    dist = np.maximum(dist, 0)
    max_exact = NUM_BUCKETS // 2
    d = np.maximum(dist, 1).astype(np.float32)
    large = max_exact + (np.log(d / np.float32(max_exact)) / np.float32(math.log(MAX_DISTANCE / max_exact))
                         * np.float32(NUM_BUCKETS - max_exact)).astype(np.int32)
    large = np.minimum(large, NUM_BUCKETS - 1)
    return np.where(dist < max_exact, dist, large).astype(np.int32)


def _dil_bucket_tiles():
    qa = np.arange(DIL_BLOCK)[:, None]
    kj = np.arange(DIL_BLOCK)[None, :]
    tiles = []
    for _, dil in DIL_PATTERNS:
        tiles.append(_t5_bucket_np((qa - kj) * dil))
        tiles.append(_t5_bucket_np((qa + DIL_BLOCK - kj) * dil))
    return np.stack(tiles)


def _moba_bucket_tiles(nblk):
    ki = np.arange(MOBA_BLOCK)[:, None]
    qi = np.arange(MOBA_BLOCK)[None, :]
    return np.stack([_t5_bucket_np(db * MOBA_BLOCK + qi - ki) for db in range(nblk)])


def _bias_tile_kernel(tbl_ref, idx_ref, o_ref, *, head_offset):
    h = pl.program_id(0) + head_offset
    idx = idx_ref[0]
    acc = jnp.zeros(idx.shape, F32)
    for b in range(NUM_BUCKETS):
        acc = jnp.where(idx == b, tbl_ref[b, h] * LOG2E, acc)
    o_ref[0, 0] = acc


def _bias_tiles(pos_bias, idx_np, head_offset):
    nt, r, c = idx_np.shape
    return pl.pallas_call(
        functools.partial(_bias_tile_kernel, head_offset=head_offset),
        out_shape=jax.ShapeDtypeStruct((N_HEADS, nt, r, c), F32),
        grid=(N_HEADS, nt),
        in_specs=[pl.BlockSpec(memory_space=pltpu.SMEM),
                  pl.BlockSpec((1, r, c), lambda h, t: (t, 0, 0))],
        out_specs=pl.BlockSpec((1, 1, r, c), lambda h, t: (h, t, 0, 0)),
        compiler_params=_cparams(("arbitrary", "arbitrary")),
        name="bias_tiles",
    )(pos_bias, jnp.asarray(idx_np))


def _inproj_kernel(x_ref, g_ref, w_ref, o_ref, h_scr):
    @pl.when(pl.program_id(1) == 0)
    def _():
        h_scr[...] = _rms(x_ref[...], g_ref[...]).astype(BF16)

    o_ref[...] = _dot(h_scr[...], w_ref[...])


def _inproj(x2d, g, w, tm=1024, tn=1024):
    t, d = x2d.shape
    n = w.shape[1]
    tm = min(tm, t)
    return pl.pallas_call(
        _inproj_kernel,
        out_shape=jax.ShapeDtypeStruct((t, n), F32),
        grid=(t // tm, n // tn),
        in_specs=[pl.BlockSpec((tm, d), lambda i, j: (i, 0)),
                  pl.BlockSpec((1, d), lambda i, j: (0, 0)),
                  pl.BlockSpec((d, tn), lambda i, j: (0, j))],
        out_specs=pl.BlockSpec((tm, tn), lambda i, j: (i, j)),
        scratch_shapes=[pltpu.VMEM((tm, d), BF16)],
        compiler_params=_cparams(("arbitrary", "arbitrary")),
        name="inproj",
    )(x2d, g.reshape(1, d), w)


def _dilated_kernel(q_ref, k_ref, v_ref, bias_ref, o_ref, m0, m1, l0, l1, acc, *, seq):
    c = DIL_BLOCK
    lane = lax.broadcasted_iota(jnp.int32, (c, LANES), 1)
    head0 = lane < HEAD_DIM
    row = lax.broadcasted_iota(jnp.int32, (c, c), 0)
    col = lax.broadcasted_iota(jnp.int32, (c, c), 1)
    cur_valid = col <= row
    prev_valid = col >= row

    stats = ((m0, l0), (m1, l1))
    head_lanes = (head0, ~head0)

    def tiles(pairs, pi, wide, first):
        rep = (lambda x: jnp.concatenate([x, x], axis=1)) if wide else (lambda x: x)
        valid = jnp.concatenate([prev_valid, cur_valid], axis=1) if wide else cur_valid
        ones = jnp.ones(((2 if wide else 1) * c, LANES), BF16)
        bias = [jnp.concatenate([bias_ref[h, 2 * pi + 1], bias_ref[h, 2 * pi]], axis=1) if wide
                else bias_ref[h, 2 * pi] for h in range(2)]
        q = [q_ref[0, qsl, :] * (ATT_SCALE * LOG2E) for qsl, _ in pairs]
        kb = [k_ref[0, ksl, :].astype(BF16) for _, ksl in pairs]
        vb = [jnp.concatenate([v_ref[0, ksl, :].astype(BF16), ones], axis=1) for _, ksl in pairs]
        idx = [(h, i) for i in range(len(pairs)) for h in range(2)]
        s = {(h, i): _dot_nt(jnp.where(head_lanes[h], q[i], 0.0).astype(BF16), kb[i]) for h, i in idx}
        s = {hi: jnp.where(valid, s[hi] + bias[hi[0]], NEG_INF) for hi in idx}
        m_new = {hi: jnp.broadcast_to(jnp.max(s[hi], axis=1, keepdims=True), (c, LANES)) for hi in idx}
        if not first:
            m_old = {(h, i): stats[h][0][pairs[i][0], :] for h, i in idx}
            m_new = {hi: jnp.maximum(m_old[hi], m_new[hi]) for hi in idx}
            alpha = {hi: jnp.exp2(m_old[hi] - m_new[hi]) for hi in idx}
        p = {hi: jnp.exp2(s[hi] - rep(m_new[hi])).astype(BF16) for hi in idx}
        o = {(h, i): _dot(p[h, i], vb[i]) for h, i in idx}
        for h, i in idx:
            m_ref, l_ref = stats[h]
            qsl = pairs[i][0]
            l_new = o[h, i][:, LANES:]
            l_ref[qsl, :] = l_new if first else alpha[h, i] * l_ref[qsl, :] + l_new
            m_ref[qsl, :] = m_new[h, i]
        for i, (qsl, _) in enumerate(pairs):
            o_new = jnp.where(head0, o[0, i][:, :LANES], o[1, i][:, :LANES])
            acc[qsl, :] = o_new if first else acc[qsl, :] * jnp.where(head0, alpha[0, i], alpha[1, i]) + o_new

    def group_size(n, cap):
        return max(g for g in range(1, cap + 1) if n % g == 0)

    order = sorted(range(len(DIL_PATTERNS)), key=lambda i: -DIL_PATTERNS[i][1])
    for pi in order:
        dil = DIL_PATTERNS[pi][1]
        first_pass = pi == order[0]
        nb = seq // dil // c

        def sl(r, n, blocks, dil=dil):
            start = r + n * (c * dil)
            if dil == 1:
                return pl.ds(pl.multiple_of(start, c), blocks * c)
            return pl.ds(start, blocks * c, stride=dil)

        g_head = group_size(dil, 8)

        def head_body(it, carry, sl=sl, pi=pi, g=g_head, first=first_pass):
            tiles([(sl(it * g + j, 0, 1),) * 2 for j in range(g)], pi, False, first)
            return carry

        lax.fori_loop(0, dil // g_head, head_body, 0)
        if nb > 1:
            n_body = dil * (nb - 1)
            g_body = group_size(n_body, 4)

            def body(it, carry, nb=nb, sl=sl, pi=pi, g=g_body, first=first_pass):
                pairs = []
                for j in range(g):
                    i = it * g + j
                    r, n = i // (nb - 1), i % (nb - 1) + 1
                    pairs.append((sl(r, n, 1), sl(r, n - 1, 2)))
                tiles(pairs, pi, True, first)
                return carry

            lax.fori_loop(0, n_body // g_body, body, 0)

    o_ref[0] = acc[...] / jnp.where(head0[:1], l0[...], l1[...])


def _mixer_dilated(proj3, bias_a):
    b, s, _ = proj3.shape
    blk = lambda off: pl.BlockSpec((1, s, LANES), lambda p, i, off=off: (i, 0, off + p))
    return pl.pallas_call(
        functools.partial(_dilated_kernel, seq=s),
        out_shape=jax.ShapeDtypeStruct((b, s, GW), F32),
        grid=(N_PAIRS, b),
        in_specs=[blk(COL_A // LANES), blk((COL_A + GW) // LANES), blk((COL_A + 2 * GW) // LANES),
                  pl.BlockSpec((2, 2 * len(DIL_PATTERNS), DIL_BLOCK, DIL_BLOCK),
                               lambda p, i: (p, 0, 0, 0))],
        out_specs=pl.BlockSpec((1, s, LANES), lambda p, i: (i, 0, p)),
        scratch_shapes=[pltpu.VMEM((s, LANES), F32)] * 5,
        compiler_params=_cparams(("arbitrary", "arbitrary")),
        name="mixer_dilated",
    )(proj3, proj3, proj3, bias_a)


def _gelu_tanh(x):
    return 0.5 * x * (1.0 + jnp.tanh(math.sqrt(2.0 / math.pi) * (x + 0.044715 * (x * x * x))))


def _sgu_kernel(u_ref, v_ref, lng_ref, w_ref, bias_ref, o_ref, *, rows):
    t = SGU_CHUNK
    u = _gelu_tanh(u_ref[0])
    v = _gelu_tanh(v_ref[0])
    mu = jnp.mean(v, axis=-1, keepdims=True)
    vc = v - mu
    var = jnp.mean(vc * vc, axis=-1, keepdims=True)
    vn = (vc * lax.rsqrt(var + SGU_LN_EPS) * lng_ref[...]).astype(BF16)
    r2 = lax.broadcasted_iota(jnp.int32, (2 * t, t), 0)
    c2 = lax.broadcasted_iota(jnp.int32, (2 * t, t), 1)
    causal = c2 <= jnp.where(r2 >= t, r2 - t, r2)
    first_group = lax.broadcasted_iota(jnp.int32, (t, LANES), 1) < HEAD_DIM
    for p in range(N_PAIRS):
        wp = jnp.where(causal, w_ref[p], 0.0).astype(BF16)
        for ci in range(rows // t):
            rs = slice(ci * t, (ci + 1) * t)
            cs = slice(p * LANES, (p + 1) * LANES)
            res = _dot(wp, vn[rs, cs])
            mixed = jnp.where(first_group, res[:t], res[t:]) + bias_ref[:, cs]
            o_ref[0, rs, cs] = u[rs, cs] * mixed


def _mixer_sgu(proj3, ln_g, w_s, b_s, rows=512):
    b, s, _ = proj3.shape
    rows = min(rows, s)
    t = SGU_CHUNK
    bias_full = jnp.repeat(b_s.T, HEAD_DIM, axis=1)
    w_pairs = w_s.reshape(N_PAIRS, 2 * t, t)
    return pl.pallas_call(
        functools.partial(_sgu_kernel, rows=rows),
        out_shape=jax.ShapeDtypeStruct((b, s, GW), F32),
        grid=(b, s // rows),
        in_specs=[pl.BlockSpec((1, rows, GW), lambda i, j: (i, j, COL_B // GW)),
                  pl.BlockSpec((1, rows, GW), lambda i, j: (i, j, COL_B // GW + 1)),
                  pl.BlockSpec((1, GW), lambda i, j: (0, 0)),
                  pl.BlockSpec((N_PAIRS, 2 * t, t), lambda i, j: (0, 0, 0)),
                  pl.BlockSpec((t, GW), lambda i, j: (0, 0))],
        out_specs=pl.BlockSpec((1, rows, GW), lambda i, j: (i, j, 0)),
        compiler_params=_cparams(("arbitrary", "arbitrary")),
        name="mixer_sgu",
    )(proj3, proj3, ln_g.reshape(1, GW), w_pairs, bias_full)


def _moba_kernel(q_ref, k_ref, v_ref, bias_ref, o_ref, kh_scr, vt_scr, ot_scr, *, seq):
    ones_rows = 16
    bs = MOBA_BLOCK
    nblk = seq // bs
    lane = lax.broadcasted_iota(jnp.int32, (1, LANES), 1)
    head_lanes = (lane < HEAD_DIM, lane >= HEAD_DIM)
    blk = lambda i: slice(i * bs, (i + 1) * bs)

    q_all = q_ref[0]
    kbar = jnp.concatenate([jnp.mean(k_ref[0, blk(j), :], axis=0, keepdims=True) for j in range(nblk)], axis=0)
    for j in range(nblk):
        kj = k_ref[0, blk(j), :]
        for h in range(2):
            kh_scr[h, j] = jnp.where(head_lanes[h], kj, 0.0).astype(BF16)
        vt = v_ref[0, blk(j), :].T.astype(BF16)
        for h in range(2):
            vt_scr[j, h] = jnp.concatenate(
                [vt[h * HEAD_DIM:(h + 1) * HEAD_DIM], jnp.ones((ones_rows, bs), BF16)], axis=0)

    jrow = lax.broadcasted_iota(jnp.int32, (nblk, seq), 0)
    own = lax.broadcasted_iota(jnp.int32, (nblk, seq), 1) // bs
    krow = lax.broadcasted_iota(jnp.int32, (bs, bs), 0)
    qcol = lax.broadcasted_iota(jnp.int32, (bs, bs), 1)
    causal = krow <= qcol

    sel = []
    for h in range(2):
        gate = _dot_nt(jnp.where(head_lanes[h], kbar, 0.0), q_all, precision=HIGHEST)
        gate = jnp.where(jrow < own, gate, NEG_INF)
        rank = jnp.zeros((nblk, seq), jnp.int32)
        for j2 in range(nblk):
            gj = gate[j2:j2 + 1, :]
            ahead = (gj > gate) | ((gj == gate) & (j2 < jrow))
            rank = rank + ahead.astype(jnp.int32)
        sel.append((rank < MOBA_TOPK) & (jrow < own))

    for grp in range(nblk // 2):
        probs = [(h, qb) for qb in (grp, nblk - 1 - grp) for h in range(2)]
        tiles = [(h, qb, j) for h, qb in probs for j in range(qb + 1)]
        qh = {qb: (q_ref[0, blk(qb), :] * (ATT_SCALE * LOG2E)).astype(BF16) for _, qb in probs}
        s = {(h, qb, j): _dot_nt(kh_scr[h, j], qh[qb]) + bias_ref[h, qb - j] for h, qb, j in tiles}
        for h, qb in probs:
            s[h, qb, qb] = jnp.where(causal, s[h, qb, qb], NEG_INF)
        picked = {(h, qb, j): sel[h][j:j + 1, blk(qb)] for h, qb, j in tiles if j != qb}
        cmax = {t: jnp.max(s[t], axis=0, keepdims=True) for t in tiles}
        m = {}
        for h, qb in probs:
            m[h, qb] = functools.reduce(
                jnp.maximum, [cmax[h, qb, qb]] + [jnp.where(picked[h, qb, j], cmax[h, qb, j], NEG_INF)
                                                  for j in range(qb)])
        shift = {t: m[t[0], t[1]] if t[2] == t[1] else jnp.where(picked[t], m[t[0], t[1]], -NEG_INF)
                 for t in tiles}
        p = {t: jnp.exp2(s[t] - shift[t]).astype(BF16) for t in tiles}
        for h, qb in probs:
            acc = sum(_dot(vt_scr[j, h], p[h, qb, j]) for j in range(qb + 1))
            ot_scr[qb, h * HEAD_DIM:(h + 1) * HEAD_DIM, :] = acc[:HEAD_DIM] / acc[HEAD_DIM:HEAD_DIM + 1]

    for qb in range(nblk):
        o_ref[0, blk(qb), :] = ot_scr[qb].T


def _mixer_moba(proj3, bias_c):
    b, s, _ = proj3.shape
    nblk = s // MOBA_BLOCK
    blk = lambda off: pl.BlockSpec((1, s, LANES), lambda p, i, off=off: (i, 0, off + p))
    return pl.pallas_call(
        functools.partial(_moba_kernel, seq=s),
        out_shape=jax.ShapeDtypeStruct((b, s, GW), F32),
        grid=(N_PAIRS, b),
        in_specs=[blk(COL_C // LANES), blk((COL_C + GW) // LANES), blk((COL_C + 2 * GW) // LANES),
                  pl.BlockSpec((2, nblk, MOBA_BLOCK, MOBA_BLOCK), lambda p, i: (p, 0, 0, 0))],
        out_specs=pl.BlockSpec((1, s, LANES), lambda p, i: (i, 0, p)),
        scratch_shapes=[pltpu.VMEM((2, nblk, MOBA_BLOCK, LANES), BF16),
                        pltpu.VMEM((nblk, 2, HEAD_DIM + 16, MOBA_BLOCK), BF16),
                        pltpu.VMEM((nblk, LANES, MOBA_BLOCK), F32)],
        compiler_params=_cparams(("arbitrary", "arbitrary")),
        name="mixer_moba",
    )(proj3, proj3, proj3, bias_c)


def _sigmoid(x):
    return 1.0 / (1.0 + jnp.exp(-x))


def _split2(x):
    hi = x.astype(BF16)
    return hi, (x - hi.astype(F32)).astype(BF16)


def _head_sum(x, ones_bd):
    hi, lo = _split2(x)
    return _dot(hi, ones_bd) + _dot(lo, ones_bd)


def _rwkv_prep_kernel(p_ref, prev_ref, mu_ref, w0_ref, w2_ref, a0_ref, a2_ref, g2_ref, kk_ref, ka_ref, rk_ref,
                      ones_ref, tri_ref, at_o, rt_o, bt_o, kt_o, be_o, ke_o, v_o, ee_o, g_o, bg_o, *, rows):
    c = RWKV_CHUNK
    p = p_ref[0]
    prev_row = jnp.where(pl.program_id(1) == 0, 0.0, prev_ref[0, 7:8, :])
    first_row = lax.broadcasted_iota(jnp.int32, (rows, 1), 0) == 0
    y_prev = jnp.where(first_row, prev_row, pltpu.roll(p, 1, axis=0))
    xs = p + (y_prev - p) * mu_ref[...]
    r = xs[:, 0:GW]
    k = xs[:, GW:2 * GW]
    v = xs[:, 2 * GW:3 * GW]
    wd = xs[:, 3 * GW:3 * GW + LORA_PAD]
    ad = xs[:, 3 * GW + LORA_PAD:3 * GW + 2 * LORA_PAD]
    gd = xs[:, 3 * GW + 2 * LORA_PAD:]
    nz = -(w0_ref[...] + _dot(jnp.tanh(wd).astype(BF16), w2_ref[...]))
    softplus = jnp.maximum(nz, 0.0) + jnp.log(1.0 + jnp.exp(-jnp.abs(nz)))
    log_decay = -jnp.exp(-softplus - 0.5)
    a_sig = _sigmoid(a0_ref[...] + _dot(ad.astype(BF16), a2_ref[...]))
    g = _dot(_sigmoid(gd).astype(BF16), g2_ref[...])
    kk = k * kk_ref[...]
    ss = _head_sum(kk * kk, ones_ref[...])
    kk = kk / jnp.maximum(jnp.sqrt(ss), 1e-12)
    k_mod = k * (1.0 + (a_sig - 1.0) * ka_ref[...])
    kb = kk * a_sig
    hi = log_decay.astype(BF16)
    rem = log_decay - hi.astype(F32)
    mid = rem.astype(BF16)
    lo = (rem - mid.astype(F32)).astype(BF16)
    tri = tri_ref[...]
    cum = _dot(tri, hi) + _dot(tri, mid) + _dot(tri, lo)
    cum_end = jnp.concatenate(
        [jnp.broadcast_to(cum[(i + 1) * c - 1:(i + 1) * c, :], (c, GW)) for i in range(rows // c)], axis=0)
    e_cum = jnp.exp(cum)
    e_inv = jnp.exp(-cum)
    e_rem = jnp.exp(cum_end - cum)
    coef = _head_sum(r * k_mod * rk_ref[...], ones_ref[...])
    outs = ((at_o, -kk * jnp.exp(cum - log_decay)), (rt_o, r * e_cum), (bt_o, kb * e_inv), (kt_o, k_mod * e_inv),
            (be_o, kb * e_rem), (ke_o, k_mod * e_rem), (v_o, v), (ee_o, jnp.exp(cum_end)), (g_o, g),
            (bg_o, coef * v * g))
    for ref, val in outs:
        for pr in range(N_PAIRS):
            ref[0, pr] = val[:, pr * LANES:(pr + 1) * LANES].astype(ref.dtype)


def _rwkv_prep(proj3, mu, w0, w2, a0, a2, g2, k_k, k_a, r_k, rows=256):
    b, s, _ = proj3.shape
    rows = min(rows, s)
    pad = lambda w, n: jnp.concatenate([w, jnp.zeros((n - w.shape[0],) + w.shape[1:], w.dtype)], axis=0)
    head_of = np.arange(GW) // HEAD_DIM
    ones_bd = jnp.asarray((head_of[:, None] == head_of[None, :]).astype(np.float32), dtype=BF16)
    tok = np.arange(rows)
    tri_bd = jnp.asarray(((tok[:, None] // RWKV_CHUNK == tok[None, :] // RWKV_CHUNK)
                          & (tok[None, :] <= tok[:, None])).astype(np.float32), dtype=BF16)
    vec = lambda a: a.reshape(1, -1)
    full = lambda shape: pl.BlockSpec(shape, lambda i, j: (0,) * len(shape))
    col = COL_D // RWKV_W
    sub = rows // 8
    sd = lambda dt: jax.ShapeDtypeStruct((b, N_PAIRS, s, LANES), dt)
    return pl.pallas_call(
        functools.partial(_rwkv_prep_kernel, rows=rows),
        out_shape=[sd(BF16)] * 7 + [sd(F32)] * 3,
        grid=(b, s // rows),
        in_specs=[pl.BlockSpec((1, rows, RWKV_W), lambda i, j: (i, j, col)),
                  pl.BlockSpec((1, 8, RWKV_W), lambda i, j: (i, jnp.maximum(j * sub - 1, 0), col)),
                  full((1, RWKV_W)), full((1, GW)), full((LORA_PAD, GW)), full((1, GW)),
                  full((LORA_PAD, GW)), full((G_LORA, GW)), full((1, GW)), full((1, GW)), full((1, GW)),
                  full((GW, GW)), full((rows, rows))],
        out_specs=[pl.BlockSpec((1, N_PAIRS, rows, LANES), lambda i, j: (i, 0, j, 0))] * 10,
        compiler_params=_cparams(("arbitrary", "arbitrary")),
        name="rwkv_prep",
    )(proj3, proj3, vec(mu), vec(w0), pad(w2, LORA_PAD).astype(BF16), vec(a0), pad(a2, LORA_PAD).astype(BF16),
      g2.astype(BF16), vec(k_k), vec(k_a), vec(r_k), ones_bd, tri_bd)


def _block_diag(y, first):
    zero = jnp.zeros_like(y)
    return jnp.concatenate([jnp.where(first, y, zero), jnp.where(first, zero, y)], axis=0)


def _pair_nn(x, y, first):
    return _dot(x, _block_diag(y, first))


def _pair_nt(x, y, first):
    return _dot_nt(x, _block_diag(y, first))


def _pair_tn(x, y, first):
    full = _dot_tn(x, y)
    return jnp.where(first, full[:HEAD_DIM], full[HEAD_DIM:])


def _rwkv_chunk_terms(probs, masks):
    first, strict, incl, same_sub, eye = masks
    c = RWKV_CHUNK
    bf = lambda xs: [x.astype(BF16) for x in xs]
    nn = lambda xs, ys: [_pair_nn(x, y, first) for x, y in zip(xs, ys)]
    ident = eye.astype(F32)
    at, rt, bt, kt, b_end, k_end, v, e_end = [list(x) for x in zip(*probs)]
    ar = [jnp.concatenate([a, r], axis=0) for a, r in zip(at, rt)]
    gb = [_pair_nt(x, y, first) for x, y in zip(ar, bt)]
    gk = [_pair_nt(x, y, first) for x, y in zip(ar, kt)]
    a_ab = [jnp.where(strict, g[:c], 0.0) for g in gb]
    a_rb = bf([jnp.where(incl, g[c:], 0.0) for g in gb])
    a_kr = bf([jnp.concatenate([jnp.where(strict, g[:c], 0.0), jnp.where(incl, g[c:], 0.0)], axis=0) for g in gk])
    ad = [jnp.where(same_sub, a, 0.0) for a in a_ab]
    an = bf([a - d for a, d in zip(a_ab, ad)])
    adb = bf(ad)
    p2 = nn(adb, adb)
    av = nn(a_kr, v)
    p2b = bf(p2)
    p4 = nn(p2b, p2b)
    x1 = nn(bf([ident + d for d in ad]), bf([ident + p for p in p2]))
    p4b = bf(p4)
    p8 = nn(p4b, p4b)
    x2 = nn(bf([ident + p for p in p4]), bf([ident + p for p in p8]))
    td = bf(nn(bf(x1), bf(x2)))
    m1 = nn(td, an)
    m1b = bf(m1)
    m2 = nn(m1b, m1b)
    x3 = nn(bf([ident + m for m in m1]), bf([ident + m for m in m2]))
    t_inv = bf(nn(bf(x3), td))
    a_hat = bf(nn(t_inv, at))
    u0 = bf(nn(t_inv, bf([w[:c] for w in av])))
    r_hat = [r.astype(F32) + x for r, x in zip(rt, nn(a_rb, a_hat))]
    o0 = [x + w[c:] for x, w in zip(nn(a_rb, u0), av)]
    p_mat = [jnp.where(eye, e, 0.0) + _pair_tn(h, b, first) for e, h, b in zip(e_end, a_hat, b_end)]
    z_mat = [_pair_tn(jnp.concatenate([u, w], axis=0), jnp.concatenate([b, k], axis=0), first)
             for u, w, b, k in zip(u0, v, b_end, k_end)]
    return list(zip(p_mat, z_mat, r_hat, o0))


def _rwkv_scan_kernel(at_ref, rt_ref, bt_ref, kt_ref, be_ref, ke_ref, v_ref, ee_ref, g_ref, bg_ref, lng_ref,
                      lnb_ref, o_ref, state, *, rows):
    c = RWKV_CHUNK
    ri = lax.broadcasted_iota(jnp.int32, (c, LANES), 0)
    lane = lax.broadcasted_iota(jnp.int32, (c, LANES), 1)
    first = lane < HEAD_DIM
    ci = jnp.where(first, lane, lane - HEAD_DIM)
    masks = (first, ci < ri, ci <= ri, (ri // RWKV_SUB) == (ci // RWKV_SUB), ri == ci)

    @pl.when(pl.program_id(1) == 0)
    def _():
        state[...] = jnp.zeros_like(state)

    def head_mean(x):
        lo = jnp.sum(jnp.where(first, x, 0.0), axis=-1, keepdims=True)
        hi = jnp.sum(jnp.where(first, 0.0, x), axis=-1, keepdims=True)
        return jnp.where(first, lo, hi) * (1.0 / HEAD_DIM)

    chunks = [slice(i * c, (i + 1) * c) for i in range(rows // c)]
    seq_refs = (at_ref, rt_ref, bt_ref, kt_ref, be_ref, ke_ref, v_ref)
    probs = [tuple(ref[0, pr, rs, :] for ref in seq_refs) + (ee_ref[0, pr, rs.start:rs.start + 1, :],)
             for rs in chunks for pr in range(N_PAIRS)]
    terms = _rwkv_chunk_terms(probs, masks)
    s_cur = [state[pr] for pr in range(N_PAIRS)]
    for i, rs in enumerate(chunks):
        for pr in range(N_PAIRS):
            p_mat, z_mat, r_hat, o0 = terms[i * N_PAIRS + pr]
            s_hi, s_lo = _split2(s_cur[pr])
            p_hi, p_lo = _split2(p_mat)
            o = _pair_nt(r_hat.astype(BF16), s_hi, first) + o0
            s_cur[pr] = (_pair_nn(s_hi, p_hi, first) + _pair_nn(s_hi, p_lo, first) + _pair_nn(s_lo, p_hi, first)
                         + z_mat)
            oc = o - head_mean(o)
            y = oc * lax.rsqrt(head_mean(oc * oc) + RWKV_LN_EPS) * lng_ref[pr:pr + 1, :] + lnb_ref[pr:pr + 1, :]
            o_ref[0, rs, pr * LANES:(pr + 1) * LANES] = y * g_ref[0, pr, rs, :] + bg_ref[0, pr, rs, :]
    for pr in range(N_PAIRS):
        state[pr] = s_cur[pr]


def _rwkv_scan(prep, lnx_g, lnx_b, rows=256):
    b, _, s, _ = prep[0].shape
    rows = min(rows, s)
    seq_spec = pl.BlockSpec((1, N_PAIRS, rows, LANES), lambda i, t: (i, 0, t, 0))
    par_spec = pl.BlockSpec((N_PAIRS, LANES), lambda i, t: (0, 0))
    par = lambda a: a.reshape(N_PAIRS, LANES)
    return pl.pallas_call(
        functools.partial(_rwkv_scan_kernel, rows=rows),
        out_shape=jax.ShapeDtypeStruct((b, s, GW), F32),
        grid=(b, s // rows),
        in_specs=[seq_spec] * 10 + [par_spec] * 2,
        out_specs=pl.BlockSpec((1, rows, GW), lambda i, t: (i, t, 0)),
        scratch_shapes=[pltpu.VMEM((N_PAIRS, HEAD_DIM, LANES), F32)],
        compiler_params=_cparams(("arbitrary", "arbitrary")),
        name="rwkv_scan",
    )(*prep, par(lnx_g), par(lnx_b))


def _outproj_kernel(ya_ref, yb_ref, yc_ref, yd_ref, g_ref, w_ref, x_ref, o_ref):
    acc = x_ref[...]
    for i, y_ref in enumerate((ya_ref, yb_ref, yc_ref, yd_ref)):
        yn = _rms(y_ref[...], g_ref[i:i + 1, :]).astype(BF16)
        acc = acc + _dot(yn, w_ref[i * GW:(i + 1) * GW, :])
    o_ref[...] = acc


def _outproj(ys, g, w, x2d, tm=512):
    t, d = x2d.shape
    tm = min(tm, t)
    y_spec = pl.BlockSpec((tm, GW), lambda i: (i, 0))
    return pl.pallas_call(
        _outproj_kernel,
        out_shape=jax.ShapeDtypeStruct((t, d), F32),
        grid=(t // tm,),
        in_specs=[y_spec] * 4 + [pl.BlockSpec((4, GW), lambda i: (0, 0)),
                                 pl.BlockSpec((4 * GW, d), lambda i: (0, 0)),
                                 pl.BlockSpec((tm, d), lambda i: (i, 0))],
        out_specs=pl.BlockSpec((tm, d), lambda i: (i, 0)),
        compiler_params=_cparams(("arbitrary",)),
        name="outproj",
    )(*[y.reshape(t, GW) for y in ys], g.reshape(4, GW), w, x2d)


def _ffn_kernel(x_ref, g_ref, wg_ref, wu_ref, wd_ref, gf_ref, o_ref, h_scr, *, final_norm):
    j = pl.program_id(1)

    @pl.when(j == 0)
    def _():
        x = x_ref[...]
        h_scr[...] = _rms(x, g_ref[...]).astype(BF16)
        o_ref[...] = x

    h = h_scr[...]
    gate = _dot(h, wg_ref[...])
    up = _dot(h, wu_ref[...])
    act = (gate * _sigmoid(gate) * up).astype(BF16)
    o_ref[...] += _dot(act, wd_ref[...])

    if final_norm:
        @pl.when(j == pl.num_programs(1) - 1)
        def _():
            o_ref[...] = _rms(o_ref[...], gf_ref[...])


def _ffn(x2d, g, wg, wu, wd, g_final, final_norm, tm=1024, tf=512):
    t, d = x2d.shape
    f = wg.shape[1]
    tm = min(tm, t)
    return pl.pallas_call(
        functools.partial(_ffn_kernel, final_norm=final_norm),
        out_shape=jax.ShapeDtypeStruct((t, d), F32),
        grid=(t // tm, f // tf),
        in_specs=[pl.BlockSpec((tm, d), lambda i, j: (i, 0)),
                  pl.BlockSpec((1, d), lambda i, j: (0, 0)),
                  pl.BlockSpec((d, tf), lambda i, j: (0, j)),
                  pl.BlockSpec((d, tf), lambda i, j: (0, j)),
                  pl.BlockSpec((tf, d), lambda i, j: (j, 0)),
                  pl.BlockSpec((1, d), lambda i, j: (0, 0))],
        out_specs=pl.BlockSpec((tm, d), lambda i, j: (i, 0)),
        scratch_shapes=[pltpu.VMEM((tm, d), BF16)],
        compiler_params=_cparams(("arbitrary", "arbitrary")),
        name="ffn",
    )(x2d, g.reshape(1, d), wg, wu, wd, g_final.reshape(1, d))


def _cast_kernel(x_ref, o_ref):
    o_ref[...] = x_ref[...].astype(o_ref.dtype)


def _to_bf16(w, rows=256):
    shape = w.shape
    w2 = w.reshape(-1, shape[-1])
    n, c = w2.shape
    out = pl.pallas_call(
        _cast_kernel,
        out_shape=jax.ShapeDtypeStruct((n, c), BF16),
        grid=(n // rows,),
        in_specs=[pl.BlockSpec((rows, c), lambda i: (i, 0))],
        out_specs=pl.BlockSpec((rows, c), lambda i: (i, 0)),
        compiler_params=_cparams(("arbitrary",)),
        name="cast_bf16",
    )(w2)
    return out.reshape(shape)


def _reorder_in_proj(w_in, mu):
    d0 = 3 * GW + 2 * GW + 3 * GW
    sizes = (GW, W_LORA, GW, GW, A_LORA, G_LORA)
    offs = np.concatenate([[0], np.cumsum(sizes)])
    piece = lambda a, i: a[..., int(offs[i]):int(offs[i + 1])]
    zeros = lambda a, n: jnp.zeros(a.shape[:-1] + (n,), a.dtype)

    def reorder(a):
        return jnp.concatenate([piece(a, 0), piece(a, 2), piece(a, 3),
                                piece(a, 1), zeros(a, LORA_PAD - W_LORA),
                                piece(a, 4), zeros(a, LORA_PAD - A_LORA), piece(a, 5)], axis=-1)

    w = jnp.concatenate([w_in[:, :d0], reorder(w_in[:, d0:])], axis=-1)
    return w.astype(BF16), reorder(mu)


def kernel(x, norm_mix_g, w_in, pos_bias, sgu_ln_g, sgu_w, sgu_b, rwkv_mu, rwkv_w0, rwkv_w2, rwkv_a0, rwkv_a2,
           rwkv_g2, rwkv_k_k, rwkv_k_a, rwkv_r_k, rwkv_lnx_g, rwkv_lnx_b, branch_norm_g, w_out, norm_ffn_g,
           w_gate, w_up, w_down, norm_final_g):
    b, s, d = x.shape
    depth = w_in.shape[0]
    assert s % (DIL_BLOCK * DIL_PATTERNS[-1][1]) == 0 and s % (2 * MOBA_BLOCK) == 0
    bias_a = _bias_tiles(pos_bias, _dil_bucket_tiles(), 0)
    bias_c = _bias_tiles(pos_bias, _moba_bucket_tiles(s // MOBA_BLOCK), N_HEADS)
    x2d = x.reshape(b * s, d)
    w_out_b, w_gate_b, w_up_b, w_down_b = (_to_bf16(w) for w in (w_out, w_gate, w_up, w_down))
    for l in range(depth):
        w_l, mu_l = _reorder_in_proj(w_in[l], rwkv_mu[l])
        proj3 = _inproj(x2d, norm_mix_g[l], w_l).reshape(b, s, D_PROJ)
        ya = _mixer_dilated(proj3, bias_a)
        yb = _mixer_sgu(proj3, sgu_ln_g[l], sgu_w[l], sgu_b[l])
        yc = _mixer_moba(proj3, bias_c)
        prep = _rwkv_prep(proj3, mu_l, rwkv_w0[l], rwkv_w2[l], rwkv_a0[l], rwkv_a2[l], rwkv_g2[l],
                          rwkv_k_k[l], rwkv_k_a[l], rwkv_r_k[l])
        yd = _rwkv_scan(prep, rwkv_lnx_g[l], rwkv_lnx_b[l])
        x2d = _outproj((ya, yb, yc, yd), branch_norm_g[l], w_out_b[l], x2d)
        x2d = _ffn(x2d, norm_ffn_g[l], w_gate_b[l], w_up_b[l], w_down_b[l], norm_final_g,
                   final_norm=(l == depth - 1))
    return x2d.reshape(b, s, d)
```

```python
import functools
import math

import jax
import jax.numpy as jnp
import numpy as np
from jax import lax
from jax.experimental import pallas as pl
from jax.experimental.pallas import tpu as pltpu

F32 = jnp.float32
BF16 = jnp.bfloat16

HEAD_DIM = 64
N_HEADS = 8
GW = N_HEADS * HEAD_DIM
LANES = 128
N_PAIRS = GW // LANES
DIL_PATTERNS = ((128, 1), (512, 4), (2048, 16))
DIL_BLOCK = 128
SGU_CHUNK = 128
SGU_LN_EPS = 1e-5
MOBA_BLOCK = 256
MOBA_TOPK = 3
W_LORA = 96
A_LORA = 96
G_LORA = 256
LORA_PAD = 128
RWKV_LN_EPS = 64e-5
RWKV_CHUNK = 64
RWKV_SUB = 16
NUM_BUCKETS = 32
MAX_DISTANCE = 2048
NORM_EPS = 1e-6
NEG_INF = -1e30
ATT_SCALE = HEAD_DIM ** -0.5
LOG2E = math.log2(math.e)

COL_A = 0
COL_B = 3 * GW
COL_C = COL_B + 2 * GW
COL_D = COL_C + 3 * GW
RWKV_W = 3 * GW + 2 * LORA_PAD + G_LORA
D_PROJ = COL_D + RWKV_W

VMEM_LIMIT = 56 * 1024 * 1024

HIGHEST = lax.Precision.HIGHEST


def _cparams(sem):
    return pltpu.CompilerParams(dimension_semantics=sem, vmem_limit_bytes=VMEM_LIMIT)


def _dot(a, b, precision=None):
    return lax.dot_general(a, b, (((1,), (0,)), ((), ())), precision=precision,
                           preferred_element_type=F32)


def _dot_nt(a, b, precision=None):
    return lax.dot_general(a, b, (((1,), (1,)), ((), ())), precision=precision,
                           preferred_element_type=F32)


def _dot_tn(a, b, precision=None):
    return lax.dot_general(a, b, (((0,), (0,)), ((), ())), precision=precision,
                           preferred_element_type=F32)


def _rms(x, g):
    return x * lax.rsqrt(jnp.mean(x * x, axis=-1, keepdims=True) + NORM_EPS) * g


def _t5_bucket_np(dist):
    dist = np.maximum(dist, 0)
    max_exact = NUM_BUCKETS // 2
    d = np.maximum(dist, 1).astype(np.float32)
    large = max_exact + (np.log(d / np.float32(max_exact)) / np.float32(math.log(MAX_DISTANCE / max_exact))
                         * np.float32(NUM_BUCKETS - max_exact)).astype(np.int32)
    large = np.minimum(large, NUM_BUCKETS - 1)
    return np.where(dist < max_exact, dist, large).astype(np.int32)


def _dil_bucket_tiles():
    qa = np.arange(DIL_BLOCK)[:, None]
    kj = np.arange(DIL_BLOCK)[None, :]
    tiles = []
    for _, dil in DIL_PATTERNS:
        tiles.append(_t5_bucket_np((qa - kj) * dil))
        tiles.append(_t5_bucket_np((qa + DIL_BLOCK - kj) * dil))
    return np.stack(tiles)


def _moba_bucket_tiles(nblk):
    ki = np.arange(MOBA_BLOCK)[:, None]
    qi = np.arange(MOBA_BLOCK)[None, :]
    return np.stack([_t5_bucket_np(db * MOBA_BLOCK + qi - ki) for db in range(nblk)])


def _bias_tile_kernel(tbl_ref, idx_ref, o_ref, *, head_offset, buckets):
    h = pl.program_id(0) + head_offset
    for t, present in enumerate(buckets):
        idx = idx_ref[t]
        acc = jnp.zeros(idx.shape, F32)
        for b in present:
            acc = jnp.where(idx == b, tbl_ref[b, h] * LOG2E, acc)
        o_ref[0, t] = acc


def _bias_tiles(pos_bias, idx_np, head_offset):
    nt, r, c = idx_np.shape
    buckets = tuple(tuple(int(b) for b in np.unique(idx_np[t])) for t in range(nt))
    return pl.pallas_call(
        functools.partial(_bias_tile_kernel, head_offset=head_offset, buckets=buckets),
        out_shape=jax.ShapeDtypeStruct((N_HEADS, nt, r, c), F32),
        grid=(N_HEADS,),
        in_specs=[pl.BlockSpec(memory_space=pltpu.SMEM),
                  pl.BlockSpec((nt, r, c), lambda h: (0, 0, 0))],
        out_specs=pl.BlockSpec((1, nt, r, c), lambda h: (h, 0, 0, 0)),
        compiler_params=_cparams(("arbitrary",)),
        name="bias_tiles",
    )(pos_bias, jnp.asarray(idx_np))


def _inproj_kernel(x_ref, g_ref, w_ref, o_ref, h_scr):
    @pl.when(pl.program_id(1) == 0)
    def _():
        h_scr[...] = _rms(x_ref[...], g_ref[...]).astype(BF16)

    o_ref[...] = _dot(h_scr[...], w_ref[...])


def _inproj(x2d, g, w_all, layer, tm=1024, tn=1024):
    t, d = x2d.shape
    n = w_all.shape[2]
    tm = min(tm, t)
    return pl.pallas_call(
        _inproj_kernel,
        out_shape=jax.ShapeDtypeStruct((t, n), F32),
        grid=(t // tm, n // tn),
        in_specs=[pl.BlockSpec((tm, d), lambda i, j: (i, 0)),
                  pl.BlockSpec((1, d), lambda i, j: (0, 0)),
                  pl.BlockSpec((None, d, tn), lambda i, j: (layer, 0, j))],
        out_specs=pl.BlockSpec((tm, tn), lambda i, j: (i, j)),
        scratch_shapes=[pltpu.VMEM((tm, d), BF16)],
        compiler_params=_cparams(("arbitrary", "arbitrary")),
        name="inproj",
    )(x2d, g.reshape(1, d), w_all)


def _dilated_kernel(q_ref, k_ref, v_ref, bias_ref, o_ref, m0, m1, l0, l1, acc, *, seq):
    c = DIL_BLOCK
    lane = lax.broadcasted_iota(jnp.int32, (c, LANES), 1)
    head0 = lane < HEAD_DIM
    row = lax.broadcasted_iota(jnp.int32, (c, c), 0)
    col = lax.broadcasted_iota(jnp.int32, (c, c), 1)
    cur_valid = col <= row
    prev_valid = col >= row

    stats = ((m0, l0), (m1, l1))
    head_lanes = (head0, ~head0)

    def tiles(pairs, pi, wide, first):
        rep = (lambda x: jnp.concatenate([x, x], axis=1)) if wide else (lambda x: x)
        valid = jnp.concatenate([prev_valid, cur_valid], axis=1) if wide else cur_valid
        ones = jnp.ones(((2 if wide else 1) * c, LANES), BF16)
        bias = [jnp.concatenate([bias_ref[h, 2 * pi + 1], bias_ref[h, 2 * pi]], axis=1) if wide
                else bias_ref[h, 2 * pi] for h in range(2)]
        q = [q_ref[0, qsl, :] * (ATT_SCALE * LOG2E) for qsl, _ in pairs]
        kb = [k_ref[0, ksl, :].astype(BF16) for _, ksl in pairs]
        vb = [jnp.concatenate([v_ref[0, ksl, :].astype(BF16), ones], axis=1) for _, ksl in pairs]
        idx = [(h, i) for i in range(len(pairs)) for h in range(2)]
        s = {(h, i): _dot_nt(jnp.where(head_lanes[h], q[i], 0.0).astype(BF16), kb[i]) for h, i in idx}
        s = {hi: jnp.where(valid, s[hi] + bias[hi[0]], NEG_INF) for hi in idx}
        m_new = {hi: jnp.broadcast_to(jnp.max(s[hi], axis=1, keepdims=True), (c, LANES)) for hi in idx}
        if not first:
            m_old = {(h, i): stats[h][0][pairs[i][0], :] for h, i in idx}
            m_new = {hi: jnp.maximum(m_old[hi], m_new[hi]) for hi in idx}
            alpha = {hi: jnp.exp2(m_old[hi] - m_new[hi]) for hi in idx}
        p = {hi: jnp.exp2(s[hi] - rep(m_new[hi])).astype(BF16) for hi in idx}
        o = {(h, i): _dot(p[h, i], vb[i]) for h, i in idx}
        for h, i in idx:
            m_ref, l_ref = stats[h]
            qsl = pairs[i][0]
            l_new = o[h, i][:, LANES:]
            l_ref[qsl, :] = l_new if first else alpha[h, i] * l_ref[qsl, :] + l_new
            m_ref[qsl, :] = m_new[h, i]
        for i, (qsl, _) in enumerate(pairs):
            o_new = jnp.where(head0, o[0, i][:, :LANES], o[1, i][:, :LANES])
            acc[qsl, :] = o_new if first else acc[qsl, :] * jnp.where(head0, alpha[0, i], alpha[1, i]) + o_new

    def group_size(n, cap):
        return max(g for g in range(1, cap + 1) if n % g == 0)

    order = sorted(range(len(DIL_PATTERNS)), key=lambda i: -DIL_PATTERNS[i][1])
    for pi in order:
        dil = DIL_PATTERNS[pi][1]
        first_pass = pi == order[0]
        nb = seq // dil // c

        def sl(r, n, blocks, dil=dil):
            start = r + n * (c * dil)
            if dil == 1:
                return pl.ds(pl.multiple_of(start, c), blocks * c)
            return pl.ds(start, blocks * c, stride=dil)

        g_head = group_size(dil, 8)

        def head_body(it, carry, sl=sl, pi=pi, g=g_head, first=first_pass):
            tiles([(sl(it * g + j, 0, 1),) * 2 for j in range(g)], pi, False, first)
            return carry

        lax.fori_loop(0, dil // g_head, head_body, 0)
        if nb > 1:
            n_body = dil * (nb - 1)
            g_body = group_size(n_body, 4)

            def body(it, carry, nb=nb, sl=sl, pi=pi, g=g_body, first=first_pass):
                pairs = []
                for j in range(g):
                    i = it * g + j
                    r, n = i // (nb - 1), i % (nb - 1) + 1
                    pairs.append((sl(r, n, 1), sl(r, n - 1, 2)))
                tiles(pairs, pi, True, first)
                return carry

            lax.fori_loop(0, n_body // g_body, body, 0)

    o_ref[0] = acc[...] / jnp.where(head0[:1], l0[...], l1[...])


def _mixer_dilated(proj3, bias_a):
    b, s, _ = proj3.shape
    blk = lambda off: pl.BlockSpec((1, s, LANES), lambda p, i, off=off: (i, 0, off + p))
    return pl.pallas_call(
        functools.partial(_dilated_kernel, seq=s),
        out_shape=jax.ShapeDtypeStruct((b, s, GW), F32),
        grid=(N_PAIRS, b),
        in_specs=[blk(COL_A // LANES), blk((COL_A + GW) // LANES), blk((COL_A + 2 * GW) // LANES),
                  pl.BlockSpec((2, 2 * len(DIL_PATTERNS), DIL_BLOCK, DIL_BLOCK),
                               lambda p, i: (p, 0, 0, 0))],
        out_specs=pl.BlockSpec((1, s, LANES), lambda p, i: (i, 0, p)),
        scratch_shapes=[pltpu.VMEM((s, LANES), F32)] * 5,
        compiler_params=_cparams(("arbitrary", "arbitrary")),
        name="mixer_dilated",
    )(proj3, proj3, proj3, bias_a)


def _gelu_tanh(x):
    return 0.5 * x * (1.0 + jnp.tanh(math.sqrt(2.0 / math.pi) * (x + 0.044715 * (x * x * x))))


def _sgu_kernel(u_ref, v_ref, lng_ref, w_ref, bias_ref, o_ref, *, rows):
    t = SGU_CHUNK
    u = _gelu_tanh(u_ref[0])
    v = _gelu_tanh(v_ref[0])
    mu = jnp.mean(v, axis=-1, keepdims=True)
    vc = v - mu
    var = jnp.mean(vc * vc, axis=-1, keepdims=True)
    vn = (vc * lax.rsqrt(var + SGU_LN_EPS) * lng_ref[...]).astype(BF16)
    r2 = lax.broadcasted_iota(jnp.int32, (2 * t, t), 0)
    c2 = lax.broadcasted_iota(jnp.int32, (2 * t, t), 1)
    causal = c2 <= jnp.where(r2 >= t, r2 - t, r2)
    first_group = lax.broadcasted_iota(jnp.int32, (t, LANES), 1) < HEAD_DIM
    for p in range(N_PAIRS):
        wp = jnp.where(causal, w_ref[p], 0.0).astype(BF16)
        for ci in range(rows // t):
            rs = slice(ci * t, (ci + 1) * t)
            cs = slice(p * LANES, (p + 1) * LANES)
            res = _dot(wp, vn[rs, cs])
            mixed = jnp.where(first_group, res[:t], res[t:]) + bias_ref[:, cs]
            o_ref[0, rs, cs] = u[rs, cs] * mixed


def _mixer_sgu(proj3, ln_g, w_s, b_s, rows=512):
    b, s, _ = proj3.shape
    rows = min(rows, s)
    t = SGU_CHUNK
    bias_full = jnp.repeat(b_s.T, HEAD_DIM, axis=1)
    w_pairs = w_s.reshape(N_PAIRS, 2 * t, t)
    return pl.pallas_call(
        functools.partial(_sgu_kernel, rows=rows),
        out_shape=jax.ShapeDtypeStruct((b, s, GW), F32),
        grid=(b, s // rows),
        in_specs=[pl.BlockSpec((1, rows, GW), lambda i, j: (i, j, COL_B // GW)),
                  pl.BlockSpec((1, rows, GW), lambda i, j: (i, j, COL_B // GW + 1)),
                  pl.BlockSpec((1, GW), lambda i, j: (0, 0)),
                  pl.BlockSpec((N_PAIRS, 2 * t, t), lambda i, j: (0, 0, 0)),
                  pl.BlockSpec((t, GW), lambda i, j: (0, 0))],
        out_specs=pl.BlockSpec((1, rows, GW), lambda i, j: (i, j, 0)),
        compiler_params=_cparams(("arbitrary", "arbitrary")),
        name="mixer_sgu",
    )(proj3, proj3, ln_g.reshape(1, GW), w_pairs, bias_full)


def _moba_kernel(q_ref, k_ref, v_ref, bias_ref, o_ref, kh_scr, vt_scr, ot_scr, *, seq):
    ones_rows = 16
    bs = MOBA_BLOCK
    nblk = seq // bs
    lane = lax.broadcasted_iota(jnp.int32, (1, LANES), 1)
    head_lanes = (lane < HEAD_DIM, lane >= HEAD_DIM)
    blk = lambda i: slice(i * bs, (i + 1) * bs)

    q_all = q_ref[0]
    kbar = jnp.concatenate([jnp.mean(k_ref[0, blk(j), :], axis=0, keepdims=True) for j in range(nblk)], axis=0)
    for j in range(nblk):
        kj = k_ref[0, blk(j), :]
        for h in range(2):
            kh_scr[h, j] = jnp.where(head_lanes[h], kj, 0.0).astype(BF16)
        vt = v_ref[0, blk(j), :].T.astype(BF16)
        for h in range(2):
            vt_scr[j, h] = jnp.concatenate(
                [vt[h * HEAD_DIM:(h + 1) * HEAD_DIM], jnp.ones((ones_rows, bs), BF16)], axis=0)

    jrow = lax.broadcasted_iota(jnp.int32, (nblk, seq), 0)
    own = lax.broadcasted_iota(jnp.int32, (nblk, seq), 1) // bs
    krow = lax.broadcasted_iota(jnp.int32, (bs, bs), 0)
    qcol = lax.broadcasted_iota(jnp.int32, (bs, bs), 1)
    causal = krow <= qcol

    sel = []
    for h in range(2):
        gate = _dot_nt(jnp.where(head_lanes[h], kbar, 0.0), q_all, precision=HIGHEST)
        gate = jnp.where(jrow < own, gate, NEG_INF)
        rank = jnp.zeros((nblk, seq), jnp.int32)
        for j2 in range(nblk):
            gj = gate[j2:j2 + 1, :]
            ahead = (gj > gate) | ((gj == gate) & (j2 < jrow))
            rank = rank + ahead.astype(jnp.int32)
        sel.append((rank < MOBA_TOPK) & (jrow < own))

    for grp in range(nblk // 2):
        probs = [(h, qb) for qb in (grp, nblk - 1 - grp) for h in range(2)]
        tiles = [(h, qb, j) for h, qb in probs for j in range(qb + 1)]
        qh = {qb: (q_ref[0, blk(qb), :] * (ATT_SCALE * LOG2E)).astype(BF16) for _, qb in probs}
        s = {(h, qb, j): _dot_nt(kh_scr[h, j], qh[qb]) + bias_ref[h, qb - j] for h, qb, j in tiles}
        for h, qb in probs:
            s[h, qb, qb] = jnp.where(causal, s[h, qb, qb], NEG_INF)
        picked = {(h, qb, j): sel[h][j:j + 1, blk(qb)] for h, qb, j in tiles if j != qb}
        cmax = {t: jnp.max(s[t], axis=0, keepdims=True) for t in tiles}
        m = {}
        for h, qb in probs:
            m[h, qb] = functools.reduce(
                jnp.maximum, [cmax[h, qb, qb]] + [jnp.where(picked[h, qb, j], cmax[h, qb, j], NEG_INF)
                                                  for j in range(qb)])
        shift = {t: m[t[0], t[1]] if t[2] == t[1] else jnp.where(picked[t], m[t[0], t[1]], -NEG_INF)
                 for t in tiles}
        p = {t: jnp.exp2(s[t] - shift[t]).astype(BF16) for t in tiles}
        for h, qb in probs:
            acc = sum(_dot(vt_scr[j, h], p[h, qb, j]) for j in range(qb + 1))
            ot_scr[qb, h * HEAD_DIM:(h + 1) * HEAD_DIM, :] = acc[:HEAD_DIM] / acc[HEAD_DIM:HEAD_DIM + 1]

    for qb in range(nblk):
        o_ref[0, blk(qb), :] = ot_scr[qb].T


def _mixer_moba(proj3, bias_c):
    b, s, _ = proj3.shape
    nblk = s // MOBA_BLOCK
    blk = lambda off: pl.BlockSpec((1, s, LANES), lambda p, i, off=off: (i, 0, off + p))
    return pl.pallas_call(
        functools.partial(_moba_kernel, seq=s),
        out_shape=jax.ShapeDtypeStruct((b, s, GW), F32),
        grid=(N_PAIRS, b),
        in_specs=[blk(COL_C // LANES), blk((COL_C + GW) // LANES), blk((COL_C + 2 * GW) // LANES),
                  pl.BlockSpec((2, nblk, MOBA_BLOCK, MOBA_BLOCK), lambda p, i: (p, 0, 0, 0))],
        out_specs=pl.BlockSpec((1, s, LANES), lambda p, i: (i, 0, p)),
        scratch_shapes=[pltpu.VMEM((2, nblk, MOBA_BLOCK, LANES), BF16),
                        pltpu.VMEM((nblk, 2, HEAD_DIM + 16, MOBA_BLOCK), BF16),
                        pltpu.VMEM((nblk, LANES, MOBA_BLOCK), F32)],
        compiler_params=_cparams(("arbitrary", "arbitrary")),
        name="mixer_moba",
    )(proj3, proj3, proj3, bias_c)


def _sigmoid(x):
    return 1.0 / (1.0 + jnp.exp(-x))


def _split2(x):
    hi = x.astype(BF16)
    return hi, (x - hi.astype(F32)).astype(BF16)


def _head_sum(x, ones_bd):
    hi, lo = _split2(x)
    return _dot(hi, ones_bd) + _dot(lo, ones_bd)


def _rwkv_prep_kernel(p_ref, prev_ref, mu_ref, w0_ref, w2_ref, a0_ref, a2_ref, g2_ref, kk_ref, ka_ref, rk_ref,
                      ones_ref, tri_ref, at_o, rt_o, bt_o, kt_o, be_o, ke_o, v_o, ee_o, g_o, bg_o, *, rows):
    c = RWKV_CHUNK
    p = p_ref[0]
    prev_row = jnp.where(pl.program_id(1) == 0, 0.0, prev_ref[0, 7:8, :])
    first_row = lax.broadcasted_iota(jnp.int32, (rows, 1), 0) == 0
    y_prev = jnp.where(first_row, prev_row, pltpu.roll(p, 1, axis=0))
    xs = p + (y_prev - p) * mu_ref[...]
    r = xs[:, 0:GW]
    k = xs[:, GW:2 * GW]
    v = xs[:, 2 * GW:3 * GW]
    wd = xs[:, 3 * GW:3 * GW + LORA_PAD]
    ad = xs[:, 3 * GW + LORA_PAD:3 * GW + 2 * LORA_PAD]
    gd = xs[:, 3 * GW + 2 * LORA_PAD:]
    nz = -(w0_ref[...] + _dot(jnp.tanh(wd).astype(BF16), w2_ref[...]))
    softplus = jnp.maximum(nz, 0.0) + jnp.log(1.0 + jnp.exp(-jnp.abs(nz)))
    log_decay = -jnp.exp(-softplus - 0.5)
    a_sig = _sigmoid(a0_ref[...] + _dot(ad.astype(BF16), a2_ref[...]))
    g = _dot(_sigmoid(gd).astype(BF16), g2_ref[...])
    kk = k * kk_ref[...]
    ss = _head_sum(kk * kk, ones_ref[...])
    kk = kk / jnp.maximum(jnp.sqrt(ss), 1e-12)
    k_mod = k * (1.0 + (a_sig - 1.0) * ka_ref[...])
    kb = kk * a_sig
    hi = log_decay.astype(BF16)
    rem = log_decay - hi.astype(F32)
    mid = rem.astype(BF16)
    lo = (rem - mid.astype(F32)).astype(BF16)
    tri = tri_ref[...]
    cum = _dot(tri, hi) + _dot(tri, mid) + _dot(tri, lo)
    cum_end = jnp.concatenate(
        [jnp.broadcast_to(cum[(i + 1) * c - 1:(i + 1) * c, :], (c, GW)) for i in range(rows // c)], axis=0)
    e_cum = jnp.exp(cum)
    e_inv = jnp.exp(-cum)
    e_rem = jnp.exp(cum_end - cum)
    coef = _head_sum(r * k_mod * rk_ref[...], ones_ref[...])
    outs = ((at_o, -kk * jnp.exp(cum - log_decay)), (rt_o, r * e_cum), (bt_o, kb * e_inv), (kt_o, k_mod * e_inv),
            (be_o, kb * e_rem), (ke_o, k_mod * e_rem), (v_o, v), (ee_o, jnp.exp(cum_end)), (g_o, g),
            (bg_o, coef * v * g))
    for ref, val in outs:
        for pr in range(N_PAIRS):
            ref[0, pr] = val[:, pr * LANES:(pr + 1) * LANES].astype(ref.dtype)


def _rwkv_prep(proj3, mu, w0, w2, a0, a2, g2, k_k, k_a, r_k, rows=256):
    b, s, _ = proj3.shape
    rows = min(rows, s)
    pad = lambda w, n: jnp.concatenate([w, jnp.zeros((n - w.shape[0],) + w.shape[1:], w.dtype)], axis=0)
    head_of = np.arange(GW) // HEAD_DIM
    ones_bd = jnp.asarray((head_of[:, None] == head_of[None, :]).astype(np.float32), dtype=BF16)
    tok = np.arange(rows)
    tri_bd = jnp.asarray(((tok[:, None] // RWKV_CHUNK == tok[None, :] // RWKV_CHUNK)
                          & (tok[None, :] <= tok[:, None])).astype(np.float32), dtype=BF16)
    vec = lambda a: a.reshape(1, -1)
    full = lambda shape: pl.BlockSpec(shape, lambda i, j: (0,) * len(shape))
    col = COL_D // RWKV_W
    sub = rows // 8
    sd = lambda dt: jax.ShapeDtypeStruct((b, N_PAIRS, s, LANES), dt)
    return pl.pallas_call(
        functools.partial(_rwkv_prep_kernel, rows=rows),
        out_shape=[sd(BF16)] * 7 + [sd(F32)] * 3,
        grid=(b, s // rows),
        in_specs=[pl.BlockSpec((1, rows, RWKV_W), lambda i, j: (i, j, col)),
                  pl.BlockSpec((1, 8, RWKV_W), lambda i, j: (i, jnp.maximum(j * sub - 1, 0), col)),
                  full((1, RWKV_W)), full((1, GW)), full((LORA_PAD, GW)), full((1, GW)),
                  full((LORA_PAD, GW)), full((G_LORA, GW)), full((1, GW)), full((1, GW)), full((1, GW)),
                  full((GW, GW)), full((rows, rows))],
        out_specs=[pl.BlockSpec((1, N_PAIRS, rows, LANES), lambda i, j: (i, 0, j, 0))] * 10,
        compiler_params=_cparams(("arbitrary", "arbitrary")),
        name="rwkv_prep",
    )(proj3, proj3, vec(mu), vec(w0), pad(w2, LORA_PAD).astype(BF16), vec(a0), pad(a2, LORA_PAD).astype(BF16),
      g2.astype(BF16), vec(k_k), vec(k_a), vec(r_k), ones_bd, tri_bd)


def _block_diag(y, first):
    zero = jnp.zeros_like(y)
    return jnp.concatenate([jnp.where(first, y, zero), jnp.where(first, zero, y)], axis=0)


def _pair_nn(x, y, first):
    return _dot(x, _block_diag(y, first))


def _pair_nt(x, y, first):
    return _dot_nt(x, _block_diag(y, first))


def _pair_tn(x, y, first):
    full = _dot_tn(x, y)
    return jnp.where(first, full[:HEAD_DIM], full[HEAD_DIM:])


def _rwkv_chunk_terms(probs, masks):
    first, strict, incl, same_sub, eye = masks
    c = RWKV_CHUNK
    bf = lambda xs: [x.astype(BF16) for x in xs]
    nn = lambda xs, ys: [_pair_nn(x, y, first) for x, y in zip(xs, ys)]
    ident = eye.astype(F32)
    at, rt, bt, kt, b_end, k_end, v, e_end = [list(x) for x in zip(*probs)]
    ar = [jnp.concatenate([a, r], axis=0) for a, r in zip(at, rt)]
    gb = [_pair_nt(x, y, first) for x, y in zip(ar, bt)]
    gk = [_pair_nt(x, y, first) for x, y in zip(ar, kt)]
    a_ab = [jnp.where(strict, g[:c], 0.0) for g in gb]
    a_rb = bf([jnp.where(incl, g[c:], 0.0) for g in gb])
    a_kr = bf([jnp.concatenate([jnp.where(strict, g[:c], 0.0), jnp.where(incl, g[c:], 0.0)], axis=0) for g in gk])
    ad = [jnp.where(same_sub, a, 0.0) for a in a_ab]
    an = bf([a - d for a, d in zip(a_ab, ad)])
    adb = bf(ad)
    p2 = nn(adb, adb)
    av = nn(a_kr, v)
    p2b = bf(p2)
    p4 = nn(p2b, p2b)
    x1 = nn(bf([ident + d for d in ad]), bf([ident + p for p in p2]))
    p4b = bf(p4)
    p8 = nn(p4b, p4b)
    x2 = nn(bf([ident + p for p in p4]), bf([ident + p for p in p8]))
    td = bf(nn(bf(x1), bf(x2)))
    m1 = nn(td, an)
    m1b = bf(m1)
    m2 = nn(m1b, m1b)
    x3 = nn(bf([ident + m for m in m1]), bf([ident + m for m in m2]))
    t_inv = bf(nn(bf(x3), td))
    a_hat = bf(nn(t_inv, at))
    u0 = bf(nn(t_inv, bf([w[:c] for w in av])))
    r_hat = [r.astype(F32) + x for r, x in zip(rt, nn(a_rb, a_hat))]
    o0 = [x + w[c:] for x, w in zip(nn(a_rb, u0), av)]
    p_mat = [jnp.where(eye, e, 0.0) + _pair_tn(h, b, first) for e, h, b in zip(e_end, a_hat, b_end)]
    z_mat = [_pair_tn(jnp.concatenate([u, w], axis=0), jnp.concatenate([b, k], axis=0), first)
             for u, w, b, k in zip(u0, v, b_end, k_end)]
    return list(zip(p_mat, z_mat, r_hat, o0))


def _rwkv_scan_kernel(at_ref, rt_ref, bt_ref, kt_ref, be_ref, ke_ref, v_ref, ee_ref, g_ref, bg_ref, lng_ref,
                      lnb_ref, o_ref, state, *, rows):
    c = RWKV_CHUNK
    ri = lax.broadcasted_iota(jnp.int32, (c, LANES), 0)
    lane = lax.broadcasted_iota(jnp.int32, (c, LANES), 1)
    first = lane < HEAD_DIM
    ci = jnp.where(first, lane, lane - HEAD_DIM)
    masks = (first, ci < ri, ci <= ri, (ri // RWKV_SUB) == (ci // RWKV_SUB), ri == ci)

    @pl.when(pl.program_id(1) == 0)
    def _():
        state[...] = jnp.zeros_like(state)

    def head_mean(x):
        lo = jnp.sum(jnp.where(first, x, 0.0), axis=-1, keepdims=True)
        hi = jnp.sum(jnp.where(first, 0.0, x), axis=-1, keepdims=True)
        return jnp.where(first, lo, hi) * (1.0 / HEAD_DIM)

    chunks = [slice(i * c, (i + 1) * c) for i in range(rows // c)]
    seq_refs = (at_ref, rt_ref, bt_ref, kt_ref, be_ref, ke_ref, v_ref)
    probs = [tuple(ref[0, pr, rs, :] for ref in seq_refs) + (ee_ref[0, pr, rs.start:rs.start + 1, :],)
             for rs in chunks for pr in range(N_PAIRS)]
    terms = _rwkv_chunk_terms(probs, masks)
    s_cur = [state[pr] for pr in range(N_PAIRS)]
    for i, rs in enumerate(chunks):
        for pr in range(N_PAIRS):
            p_mat, z_mat, r_hat, o0 = terms[i * N_PAIRS + pr]
            s_hi, s_lo = _split2(s_cur[pr])
            p_hi, p_lo = _split2(p_mat)
            o = _pair_nt(r_hat.astype(BF16), s_hi, first) + o0
            s_cur[pr] = (_pair_nn(s_hi, p_hi, first) + _pair_nn(s_hi, p_lo, first) + _pair_nn(s_lo, p_hi, first)
                         + z_mat)
            oc = o - head_mean(o)
            y = oc * lax.rsqrt(head_mean(oc * oc) + RWKV_LN_EPS) * lng_ref[pr:pr + 1, :] + lnb_ref[pr:pr + 1, :]
            o_ref[0, rs, pr * LANES:(pr + 1) * LANES] = y * g_ref[0, pr, rs, :] + bg_ref[0, pr, rs, :]
    for pr in range(N_PAIRS):
        state[pr] = s_cur[pr]


def _rwkv_scan(prep, lnx_g, lnx_b, rows=256):
    b, _, s, _ = prep[0].shape
    rows = min(rows, s)
    seq_spec = pl.BlockSpec((1, N_PAIRS, rows, LANES), lambda i, t: (i, 0, t, 0))
    par_spec = pl.BlockSpec((N_PAIRS, LANES), lambda i, t: (0, 0))
    par = lambda a: a.reshape(N_PAIRS, LANES)
    return pl.pallas_call(
        functools.partial(_rwkv_scan_kernel, rows=rows),
        out_shape=jax.ShapeDtypeStruct((b, s, GW), F32),
        grid=(b, s // rows),
        in_specs=[seq_spec] * 10 + [par_spec] * 2,
        out_specs=pl.BlockSpec((1, rows, GW), lambda i, t: (i, t, 0)),
        scratch_shapes=[pltpu.VMEM((N_PAIRS, HEAD_DIM, LANES), F32)],
        compiler_params=_cparams(("arbitrary", "arbitrary")),
        name="rwkv_scan",
    )(*prep, par(lnx_g), par(lnx_b))


def _outproj_kernel(ya_ref, yb_ref, yc_ref, yd_ref, g_ref, w_ref, x_ref, o_ref):
    acc = x_ref[...]
    for i, y_ref in enumerate((ya_ref, yb_ref, yc_ref, yd_ref)):
        yn = _rms(y_ref[...], g_ref[i:i + 1, :]).astype(BF16)
        acc = acc + _dot(yn, w_ref[i * GW:(i + 1) * GW, :])
    o_ref[...] = acc


def _outproj(ys, g, w_all, layer, x2d, tm=512):
    t, d = x2d.shape
    tm = min(tm, t)
    y_spec = pl.BlockSpec((tm, GW), lambda i: (i, 0))
    return pl.pallas_call(
        _outproj_kernel,
        out_shape=jax.ShapeDtypeStruct((t, d), F32),
        grid=(t // tm,),
        in_specs=[y_spec] * 4 + [pl.BlockSpec((4, GW), lambda i: (0, 0)),
                                 pl.BlockSpec((None, 4 * GW, d), lambda i: (layer, 0, 0)),
                                 pl.BlockSpec((tm, d), lambda i: (i, 0))],
        out_specs=pl.BlockSpec((tm, d), lambda i: (i, 0)),
        compiler_params=_cparams(("arbitrary",)),
        name="outproj",
    )(*[y.reshape(t, GW) for y in ys], g.reshape(4, GW), w_all, x2d)


def _ffn_kernel(x_ref, g_ref, wg_ref, wu_ref, wd_ref, gf_ref, o_ref, h_scr, *, final_norm):
    j = pl.program_id(1)

    @pl.when(j == 0)
    def _():
        x = x_ref[...]
        h_scr[...] = _rms(x, g_ref[...]).astype(BF16)
        o_ref[...] = x

    h = h_scr[...]
    gate = _dot(h, wg_ref[...])
    up = _dot(h, wu_ref[...])
    act = (gate * _sigmoid(gate) * up).astype(BF16)
    o_ref[...] += _dot(act, wd_ref[...])

    if final_norm:
        @pl.when(j == pl.num_programs(1) - 1)
        def _():
            o_ref[...] = _rms(o_ref[...], gf_ref[...])


def _ffn(x2d, g, wg, wu, wd, layer, g_final, final_norm, tm=1024, tf=512):
    t, d = x2d.shape
    f = wg.shape[2]
    tm = min(tm, t)
    return pl.pallas_call(
        functools.partial(_ffn_kernel, final_norm=final_norm),
        out_shape=jax.ShapeDtypeStruct((t, d), F32),
        grid=(t // tm, f // tf),
        in_specs=[pl.BlockSpec((tm, d), lambda i, j: (i, 0)),
                  pl.BlockSpec((1, d), lambda i, j: (0, 0)),
                  pl.BlockSpec((None, d, tf), lambda i, j: (layer, 0, j)),
                  pl.BlockSpec((None, d, tf), lambda i, j: (layer, 0, j)),
                  pl.BlockSpec((None, tf, d), lambda i, j: (layer, j, 0)),
                  pl.BlockSpec((1, d), lambda i, j: (0, 0))],
        out_specs=pl.BlockSpec((tm, d), lambda i, j: (i, 0)),
        scratch_shapes=[pltpu.VMEM((tm, d), BF16)],
        compiler_params=_cparams(("arbitrary", "arbitrary")),
        name="ffn",
    )(x2d, g.reshape(1, d), wg, wu, wd, g_final.reshape(1, d))


def _cast_kernel(x_ref, o_ref):
    o_ref[...] = x_ref[...].astype(o_ref.dtype)


def _to_bf16(w, rows=256):
    shape = w.shape
    w2 = w.reshape(-1, shape[-1])
    n, c = w2.shape
    out = pl.pallas_call(
        _cast_kernel,
        out_shape=jax.ShapeDtypeStruct((n, c), BF16),
        grid=(n // rows,),
        in_specs=[pl.BlockSpec((rows, c), lambda i: (i, 0))],
        out_specs=pl.BlockSpec((rows, c), lambda i: (i, 0)),
        compiler_params=_cparams(("arbitrary",)),
        name="cast_bf16",
    )(w2)
    return out.reshape(shape)


def _reorder_in_proj(w_in, mu):
    d0 = 3 * GW + 2 * GW + 3 * GW
    sizes = (GW, W_LORA, GW, GW, A_LORA, G_LORA)
    offs = np.concatenate([[0], np.cumsum(sizes)])
    piece = lambda a, i: a[..., int(offs[i]):int(offs[i + 1])]
    zeros = lambda a, n: jnp.zeros(a.shape[:-1] + (n,), a.dtype)

    def reorder(a):
        return jnp.concatenate([piece(a, 0), piece(a, 2), piece(a, 3),
                                piece(a, 1), zeros(a, LORA_PAD - W_LORA),
                                piece(a, 4), zeros(a, LORA_PAD - A_LORA), piece(a, 5)], axis=-1)

    w = jnp.concatenate([w_in[..., :d0].astype(BF16), reorder(w_in[..., d0:]).astype(BF16)], axis=-1)
    return w, reorder(mu)


def kernel(x, norm_mix_g, w_in, pos_bias, sgu_ln_g, sgu_w, sgu_b, rwkv_mu, rwkv_w0, rwkv_w2, rwkv_a0, rwkv_a2,
           rwkv_g2, rwkv_k_k, rwkv_k_a, rwkv_r_k, rwkv_lnx_g, rwkv_lnx_b, branch_norm_g, w_out, norm_ffn_g,
           w_gate, w_up, w_down, norm_final_g):
    b, s, d = x.shape
    depth = w_in.shape[0]
    assert s % (DIL_BLOCK * DIL_PATTERNS[-1][1]) == 0 and s % (2 * MOBA_BLOCK) == 0
    bias_a = _bias_tiles(pos_bias, _dil_bucket_tiles(), 0)
    bias_c = _bias_tiles(pos_bias, _moba_bucket_tiles(s // MOBA_BLOCK), N_HEADS)
    x2d = x.reshape(b * s, d)
    w_out_b, w_gate_b, w_up_b, w_down_b = (_to_bf16(w) for w in (w_out, w_gate, w_up, w_down))
    w_in_b, mu_all = _reorder_in_proj(w_in, rwkv_mu)
    for l in range(depth):
        mu_l = mu_all[l]
        proj3 = _inproj(x2d, norm_mix_g[l], w_in_b, l).reshape(b, s, D_PROJ)
        ya = _mixer_dilated(proj3, bias_a)
        yb = _mixer_sgu(proj3, sgu_ln_g[l], sgu_w[l], sgu_b[l])
        yc = _mixer_moba(proj3, bias_c)
        prep = _rwkv_prep(proj3, mu_l, rwkv_w0[l], rwkv_w2[l], rwkv_a0[l], rwkv_a2[l], rwkv_g2[l],
                          rwkv_k_k[l], rwkv_k_a[l], rwkv_r_k[l])
        yd = _rwkv_scan(prep, rwkv_lnx_g[l], rwkv_lnx_b[l])
        x2d = _outproj((ya, yb, yc, yd), branch_norm_g[l], w_out_b, l, x2d)
        x2d = _ffn(x2d, norm_ffn_g[l], w_gate_b, w_up_b, w_down_b, l, norm_final_g,
                   final_norm=(l == depth - 1))
    return x2d.reshape(b, s, d)
```

```python
import functools
import math

import jax
import jax.numpy as jnp
import numpy as np
from jax import lax
from jax.experimental import pallas as pl
from jax.experimental.pallas import tpu as pltpu

F32 = jnp.float32
BF16 = jnp.bfloat16

HEAD_DIM = 64
N_HEADS = 8
GW = N_HEADS * HEAD_DIM
LANES = 128
N_PAIRS = GW // LANES
DIL_PATTERNS = ((128, 1), (512, 4), (2048, 16))
DIL_BLOCK = 128
SGU_CHUNK = 128
SGU_LN_EPS = 1e-5
MOBA_BLOCK = 256
MOBA_TOPK = 3
W_LORA = 96
A_LORA = 96
G_LORA = 256
LORA_PAD = 128
RWKV_LN_EPS = 64e-5
RWKV_CHUNK = 64
RWKV_SUB = 16
NUM_BUCKETS = 32
MAX_DISTANCE = 2048
NORM_EPS = 1e-6
NEG_INF = -1e30
ATT_SCALE = HEAD_DIM ** -0.5
LOG2E = math.log2(math.e)

COL_A = 0
COL_B = 3 * GW
COL_C = COL_B + 2 * GW
COL_D = COL_C + 3 * GW
RWKV_W = 3 * GW + 2 * LORA_PAD + G_LORA
D_PROJ = COL_D + RWKV_W

VMEM_LIMIT = 56 * 1024 * 1024

HIGHEST = lax.Precision.HIGHEST


def _cparams(sem):
    return pltpu.CompilerParams(dimension_semantics=sem, vmem_limit_bytes=VMEM_LIMIT)


def _dot(a, b, precision=None):
    return lax.dot_general(a, b, (((1,), (0,)), ((), ())), precision=precision,
                           preferred_element_type=F32)


def _dot_nt(a, b, precision=None):
    return lax.dot_general(a, b, (((1,), (1,)), ((), ())), precision=precision,
                           preferred_element_type=F32)


def _dot_tn(a, b, precision=None):
    return lax.dot_general(a, b, (((0,), (0,)), ((), ())), precision=precision,
                           preferred_element_type=F32)


def _rms(x, g):
    return x * lax.rsqrt(jnp.mean(x * x, axis=-1, keepdims=True) + NORM_EPS) * g


def _t5_bucket_np(dist):
    dist = np.maximum(dist, 0)
    max_exact = NUM_BUCKETS // 2
    d = np.maximum(dist, 1).astype(np.float32)
    large = max_exact + (np.log(d / np.float32(max_exact)) / np.float32(math.log(MAX_DISTANCE / max_exact))
                         * np.float32(NUM_BUCKETS - max_exact)).astype(np.int32)
    large = np.minimum(large, NUM_BUCKETS - 1)
    return np.where(dist < max_exact, dist, large).astype(np.int32)


def _dil_bucket_tiles():
    qa = np.arange(DIL_BLOCK)[:, None]
    kj = np.arange(DIL_BLOCK)[None, :]
    tiles = []
    for _, dil in DIL_PATTERNS:
        tiles.append(_t5_bucket_np((qa - kj) * dil))
        tiles.append(_t5_bucket_np((qa + DIL_BLOCK - kj) * dil))
    return np.stack(tiles)


def _moba_bucket_tiles(nblk):
    ki = np.arange(MOBA_BLOCK)[:, None]
    qi = np.arange(MOBA_BLOCK)[None, :]
    return np.stack([_t5_bucket_np(db * MOBA_BLOCK + qi - ki) for db in range(nblk)])


def _bias_tile_kernel(tbl_ref, idx_ref, o_ref, *, head_offset, buckets):
    h = pl.program_id(0) + head_offset
    for t, present in enumerate(buckets):
        idx = idx_ref[t]
        acc = jnp.zeros(idx.shape, F32)
        for b in present:
            acc = jnp.where(idx == b, tbl_ref[b, h] * LOG2E, acc)
        o_ref[0, t] = acc


def _bias_tiles(pos_bias, idx_np, head_offset):
    nt, r, c = idx_np.shape
    buckets = tuple(tuple(int(b) for b in np.unique(idx_np[t])) for t in range(nt))
    return pl.pallas_call(
        functools.partial(_bias_tile_kernel, head_offset=head_offset, buckets=buckets),
        out_shape=jax.ShapeDtypeStruct((N_HEADS, nt, r, c), F32),
        grid=(N_HEADS,),
        in_specs=[pl.BlockSpec(memory_space=pltpu.SMEM),
                  pl.BlockSpec((nt, r, c), lambda h: (0, 0, 0))],
        out_specs=pl.BlockSpec((1, nt, r, c), lambda h: (h, 0, 0, 0)),
        compiler_params=_cparams(("arbitrary",)),
        name="bias_tiles",
    )(pos_bias, jnp.asarray(idx_np))


def _inproj_kernel(x_ref, g_ref, w_ref, o_ref, h_scr):
    @pl.when(pl.program_id(1) == 0)
    def _():
        h_scr[...] = _rms(x_ref[...], g_ref[...]).astype(BF16)

    o_ref[...] = _dot(h_scr[...], w_ref[...])


def _inproj(x2d, g, w_all, layer, tm=1024, tn=1024):
    t, d = x2d.shape
    n = w_all.shape[2]
    tm = min(tm, t)
    return pl.pallas_call(
        _inproj_kernel,
        out_shape=jax.ShapeDtypeStruct((t, n), F32),
        grid=(t // tm, n // tn),
        in_specs=[pl.BlockSpec((tm, d), lambda i, j: (i, 0)),
                  pl.BlockSpec((1, d), lambda i, j: (0, 0)),
                  pl.BlockSpec((None, d, tn), lambda i, j: (layer, 0, j))],
        out_specs=pl.BlockSpec((tm, tn), lambda i, j: (i, j)),
        scratch_shapes=[pltpu.VMEM((tm, d), BF16)],
        compiler_params=_cparams(("arbitrary", "arbitrary")),
        name="inproj",
    )(x2d, g.reshape(1, d), w_all)


def _dilated_kernel(q_ref, k_ref, v_ref, bias_ref, o_ref, m0, m1, l0, l1, acc, *, seq):
    c = DIL_BLOCK
    lane = lax.broadcasted_iota(jnp.int32, (c, LANES), 1)
    head0 = lane < HEAD_DIM
    row = lax.broadcasted_iota(jnp.int32, (c, c), 0)
    col = lax.broadcasted_iota(jnp.int32, (c, c), 1)
    cur_valid = col <= row
    prev_valid = col >= row

    stats = ((m0, l0), (m1, l1))
    head_lanes = (head0, ~head0)

    def tiles(pairs, pi, wide, first):
        rep = (lambda x: jnp.concatenate([x, x], axis=1)) if wide else (lambda x: x)
        valid = jnp.concatenate([prev_valid, cur_valid], axis=1) if wide else cur_valid
        ones = jnp.ones(((2 if wide else 1) * c, LANES), BF16)
        bias = [jnp.concatenate([bias_ref[h, 2 * pi + 1], bias_ref[h, 2 * pi]], axis=1) if wide
                else bias_ref[h, 2 * pi] for h in range(2)]
        q = [q_ref[0, qsl, :] * (ATT_SCALE * LOG2E) for qsl, _ in pairs]
        kb = [k_ref[0, ksl, :].astype(BF16) for _, ksl in pairs]
        vb = [jnp.concatenate([v_ref[0, ksl, :].astype(BF16), ones], axis=1) for _, ksl in pairs]
        idx = [(h, i) for i in range(len(pairs)) for h in range(2)]
        s = {(h, i): _dot_nt(jnp.where(head_lanes[h], q[i], 0.0).astype(BF16), kb[i]) for h, i in idx}
        s = {hi: jnp.where(valid, s[hi] + bias[hi[0]], NEG_INF) for hi in idx}
        m_new = {hi: jnp.broadcast_to(jnp.max(s[hi], axis=1, keepdims=True), (c, LANES)) for hi in idx}
        if not first:
            m_old = {(h, i): stats[h][0][pairs[i][0], :] for h, i in idx}
            m_new = {hi: jnp.maximum(m_old[hi], m_new[hi]) for hi in idx}
            alpha = {hi: jnp.exp2(m_old[hi] - m_new[hi]) for hi in idx}
        p = {hi: jnp.exp2(s[hi] - rep(m_new[hi])).astype(BF16) for hi in idx}
        o = {(h, i): _dot(p[h, i], vb[i]) for h, i in idx}
        for h, i in idx:
            m_ref, l_ref = stats[h]
            qsl = pairs[i][0]
            l_new = o[h, i][:, LANES:]
            l_ref[qsl, :] = l_new if first else alpha[h, i] * l_ref[qsl, :] + l_new
            m_ref[qsl, :] = m_new[h, i]
        for i, (qsl, _) in enumerate(pairs):
            o_new = jnp.where(head0, o[0, i][:, :LANES], o[1, i][:, :LANES])
            acc[qsl, :] = o_new if first else acc[qsl, :] * jnp.where(head0, alpha[0, i], alpha[1, i]) + o_new

    def group_size(n, cap):
        return max(g for g in range(1, cap + 1) if n % g == 0)

    order = sorted(range(len(DIL_PATTERNS)), key=lambda i: -DIL_PATTERNS[i][1])
    for pi in order:
        dil = DIL_PATTERNS[pi][1]
        first_pass = pi == order[0]
        nb = seq // dil // c

        def sl(r, n, blocks, dil=dil):
            start = r + n * (c * dil)
            if dil == 1:
                return pl.ds(pl.multiple_of(start, c), blocks * c)
            return pl.ds(start, blocks * c, stride=dil)

        g_head = group_size(dil, 8)

        def head_body(it, carry, sl=sl, pi=pi, g=g_head, first=first_pass):
            tiles([(sl(it * g + j, 0, 1),) * 2 for j in range(g)], pi, False, first)
            return carry

        lax.fori_loop(0, dil // g_head, head_body, 0)
        if nb > 1:
            n_body = dil * (nb - 1)
            g_body = group_size(n_body, 4)

            def body(it, carry, nb=nb, sl=sl, pi=pi, g=g_body, first=first_pass):
                pairs = []
                for j in range(g):
                    i = it * g + j
                    r, n = i // (nb - 1), i % (nb - 1) + 1
                    pairs.append((sl(r, n, 1), sl(r, n - 1, 2)))
                tiles(pairs, pi, True, first)
                return carry

            lax.fori_loop(0, n_body // g_body, body, 0)

    o_ref[0] = acc[...] / jnp.where(head0[:1], l0[...], l1[...])


def _mixer_dilated(proj3, bias_a):
    b, s, _ = proj3.shape
    blk = lambda off: pl.BlockSpec((1, s, LANES), lambda p, i, off=off: (i, 0, off + p))
    return pl.pallas_call(
        functools.partial(_dilated_kernel, seq=s),
        out_shape=jax.ShapeDtypeStruct((b, s, GW), F32),
        grid=(N_PAIRS, b),
        in_specs=[blk(COL_A // LANES), blk((COL_A + GW) // LANES), blk((COL_A + 2 * GW) // LANES),
                  pl.BlockSpec((2, 2 * len(DIL_PATTERNS), DIL_BLOCK, DIL_BLOCK),
                               lambda p, i: (p, 0, 0, 0))],
        out_specs=pl.BlockSpec((1, s, LANES), lambda p, i: (i, 0, p)),
        scratch_shapes=[pltpu.VMEM((s, LANES), F32)] * 5,
        compiler_params=_cparams(("arbitrary", "arbitrary")),
        name="mixer_dilated",
    )(proj3, proj3, proj3, bias_a)


def _gelu_tanh(x):
    return 0.5 * x * (1.0 + jnp.tanh(math.sqrt(2.0 / math.pi) * (x + 0.044715 * (x * x * x))))


def _sgu_kernel(u_ref, v_ref, lng_ref, w_ref, bias_ref, o_ref, *, rows):
    t = SGU_CHUNK
    u = _gelu_tanh(u_ref[0])
    v = _gelu_tanh(v_ref[0])
    mu = jnp.mean(v, axis=-1, keepdims=True)
    vc = v - mu
    var = jnp.mean(vc * vc, axis=-1, keepdims=True)
    vn = (vc * lax.rsqrt(var + SGU_LN_EPS) * lng_ref[...]).astype(BF16)
    r2 = lax.broadcasted_iota(jnp.int32, (2 * t, t), 0)
    c2 = lax.broadcasted_iota(jnp.int32, (2 * t, t), 1)
    causal = c2 <= jnp.where(r2 >= t, r2 - t, r2)
    first_group = lax.broadcasted_iota(jnp.int32, (t, LANES), 1) < HEAD_DIM
    for p in range(N_PAIRS):
        wp = jnp.where(causal, w_ref[p], 0.0).astype(BF16)
        for ci in range(rows // t):
            rs = slice(ci * t, (ci + 1) * t)
            cs = slice(p * LANES, (p + 1) * LANES)
            res = _dot(wp, vn[rs, cs])
            mixed = jnp.where(first_group, res[:t], res[t:]) + bias_ref[:, cs]
            o_ref[0, rs, cs] = u[rs, cs] * mixed


def _mixer_sgu(proj3, ln_g, w_s, b_s, rows=512):
    b, s, _ = proj3.shape
    rows = min(rows, s)
    t = SGU_CHUNK
    bias_full = jnp.repeat(b_s.T, HEAD_DIM, axis=1)
    w_pairs = w_s.reshape(N_PAIRS, 2 * t, t)
    return pl.pallas_call(
        functools.partial(_sgu_kernel, rows=rows),
        out_shape=jax.ShapeDtypeStruct((b, s, GW), F32),
        grid=(b, s // rows),
        in_specs=[pl.BlockSpec((1, rows, GW), lambda i, j: (i, j, COL_B // GW)),
                  pl.BlockSpec((1, rows, GW), lambda i, j: (i, j, COL_B // GW + 1)),
                  pl.BlockSpec((1, GW), lambda i, j: (0, 0)),
                  pl.BlockSpec((N_PAIRS, 2 * t, t), lambda i, j: (0, 0, 0)),
                  pl.BlockSpec((t, GW), lambda i, j: (0, 0))],
        out_specs=pl.BlockSpec((1, rows, GW), lambda i, j: (i, j, 0)),
        compiler_params=_cparams(("arbitrary", "arbitrary")),
        name="mixer_sgu",
    )(proj3, proj3, ln_g.reshape(1, GW), w_pairs, bias_full)


def _moba_kernel(q_ref, k_ref, v_ref, bias_ref, o_ref, kh_scr, vt_scr, ot_scr, *, seq):
    ones_rows = 16
    bs = MOBA_BLOCK
    nblk = seq // bs
    lane = lax.broadcasted_iota(jnp.int32, (1, LANES), 1)
    head_lanes = (lane < HEAD_DIM, lane >= HEAD_DIM)
    blk = lambda i: slice(i * bs, (i + 1) * bs)

    q_all = q_ref[0]
    kbar = jnp.concatenate([jnp.mean(k_ref[0, blk(j), :], axis=0, keepdims=True) for j in range(nblk)], axis=0)
    for j in range(nblk):
        kj = k_ref[0, blk(j), :]
        for h in range(2):
            kh_scr[h, j] = jnp.where(head_lanes[h], kj, 0.0).astype(BF16)
        vt = v_ref[0, blk(j), :].T.astype(BF16)
        for h in range(2):
            vt_scr[j, h] = jnp.concatenate(
                [vt[h * HEAD_DIM:(h + 1) * HEAD_DIM], jnp.ones((ones_rows, bs), BF16)], axis=0)

    jrow = lax.broadcasted_iota(jnp.int32, (nblk, seq), 0)
    own = lax.broadcasted_iota(jnp.int32, (nblk, seq), 1) // bs
    krow = lax.broadcasted_iota(jnp.int32, (bs, bs), 0)
    qcol = lax.broadcasted_iota(jnp.int32, (bs, bs), 1)
    causal = krow <= qcol

    sel = []
    for h in range(2):
        gate = _dot_nt(jnp.where(head_lanes[h], kbar, 0.0), q_all, precision=HIGHEST)
        gate = jnp.where(jrow < own, gate, NEG_INF)
        rank = jnp.zeros((nblk, seq), jnp.int32)
        for j2 in range(nblk):
            gj = gate[j2:j2 + 1, :]
            ahead = (gj > gate) | ((gj == gate) & (j2 < jrow))
            rank = rank + ahead.astype(jnp.int32)
        sel.append((rank < MOBA_TOPK) & (jrow < own))

    for grp in range(nblk // 2):
        probs = [(h, qb) for qb in (grp, nblk - 1 - grp) for h in range(2)]
        tiles = [(h, qb, j) for h, qb in probs for j in range(qb + 1)]
        qh = {qb: (q_ref[0, blk(qb), :] * (ATT_SCALE * LOG2E)).astype(BF16) for _, qb in probs}
        s = {(h, qb, j): _dot_nt(kh_scr[h, j], qh[qb]) + bias_ref[h, qb - j] for h, qb, j in tiles}
        for h, qb in probs:
            s[h, qb, qb] = jnp.where(causal, s[h, qb, qb], NEG_INF)
        picked = {(h, qb, j): sel[h][j:j + 1, blk(qb)] for h, qb, j in tiles if j != qb}
        cmax = {t: jnp.max(s[t], axis=0, keepdims=True) for t in tiles}
        m = {}
        for h, qb in probs:
            m[h, qb] = functools.reduce(
                jnp.maximum, [cmax[h, qb, qb]] + [jnp.where(picked[h, qb, j], cmax[h, qb, j], NEG_INF)
                                                  for j in range(qb)])
        shift = {t: m[t[0], t[1]] if t[2] == t[1] else jnp.where(picked[t], m[t[0], t[1]], -NEG_INF)
                 for t in tiles}
        p = {t: jnp.exp2(s[t] - shift[t]).astype(BF16) for t in tiles}
        for h, qb in probs:
            acc = sum(_dot(vt_scr[j, h], p[h, qb, j]) for j in range(qb + 1))
            ot_scr[qb, h * HEAD_DIM:(h + 1) * HEAD_DIM, :] = acc[:HEAD_DIM] / acc[HEAD_DIM:HEAD_DIM + 1]

    for qb in range(nblk):
        o_ref[0, blk(qb), :] = ot_scr[qb].T


def _mixer_moba(proj3, bias_c):
    b, s, _ = proj3.shape
    nblk = s // MOBA_BLOCK
    blk = lambda off: pl.BlockSpec((1, s, LANES), lambda p, i, off=off: (i, 0, off + p))
    return pl.pallas_call(
        functools.partial(_moba_kernel, seq=s),
        out_shape=jax.ShapeDtypeStruct((b, s, GW), F32),
        grid=(N_PAIRS, b),
        in_specs=[blk(COL_C // LANES), blk((COL_C + GW) // LANES), blk((COL_C + 2 * GW) // LANES),
                  pl.BlockSpec((2, nblk, MOBA_BLOCK, MOBA_BLOCK), lambda p, i: (p, 0, 0, 0))],
        out_specs=pl.BlockSpec((1, s, LANES), lambda p, i: (i, 0, p)),
        scratch_shapes=[pltpu.VMEM((2, nblk, MOBA_BLOCK, LANES), BF16),
                        pltpu.VMEM((nblk, 2, HEAD_DIM + 16, MOBA_BLOCK), BF16),
                        pltpu.VMEM((nblk, LANES, MOBA_BLOCK), F32)],
        compiler_params=_cparams(("arbitrary", "arbitrary")),
        name="mixer_moba",
    )(proj3, proj3, proj3, bias_c)


def _sigmoid(x):
    return 1.0 / (1.0 + jnp.exp(-x))


def _split2(x):
    hi = x.astype(BF16)
    return hi, (x - hi.astype(F32)).astype(BF16)


def _head_sum(x, ones_bd):
    hi, lo = _split2(x)
    return _dot(hi, ones_bd) + _dot(lo, ones_bd)


def _rwkv_prep_kernel(p_ref, prev_ref, mu_ref, w0_ref, w2_ref, a0_ref, a2_ref, g2_ref, kk_ref, ka_ref, rk_ref,
                      ones_ref, tri_ref, at_o, rt_o, bt_o, kt_o, be_o, ke_o, v_o, ee_o, g_o, bg_o, *, rows):
    c = RWKV_CHUNK
    p = p_ref[0]
    prev_row = jnp.where(pl.program_id(1) == 0, 0.0, prev_ref[0, 7:8, :])
    first_row = lax.broadcasted_iota(jnp.int32, (rows, 1), 0) == 0
    y_prev = jnp.where(first_row, prev_row, pltpu.roll(p, 1, axis=0))
    xs = p + (y_prev - p) * mu_ref[...]
    r = xs[:, 0:GW]
    k = xs[:, GW:2 * GW]
    v = xs[:, 2 * GW:3 * GW]
    wd = xs[:, 3 * GW:3 * GW + LORA_PAD]
    ad = xs[:, 3 * GW + LORA_PAD:3 * GW + 2 * LORA_PAD]
    gd = xs[:, 3 * GW + 2 * LORA_PAD:]
    nz = -(w0_ref[...] + _dot(jnp.tanh(wd).astype(BF16), w2_ref[...]))
    softplus = jnp.maximum(nz, 0.0) + jnp.log(1.0 + jnp.exp(-jnp.abs(nz)))
    log_decay = -jnp.exp(-softplus - 0.5)
    a_sig = _sigmoid(a0_ref[...] + _dot(ad.astype(BF16), a2_ref[...]))
    g = _dot(_sigmoid(gd).astype(BF16), g2_ref[...])
    kk = k * kk_ref[...]
    ss = _head_sum(kk * kk, ones_ref[...])
    kk = kk / jnp.maximum(jnp.sqrt(ss), 1e-12)
    k_mod = k * (1.0 + (a_sig - 1.0) * ka_ref[...])
    kb = kk * a_sig
    hi = log_decay.astype(BF16)
    rem = log_decay - hi.astype(F32)
    mid = rem.astype(BF16)
    lo = (rem - mid.astype(F32)).astype(BF16)
    tri = tri_ref[...]
    cum = _dot(tri, hi) + _dot(tri, mid) + _dot(tri, lo)
    cum_end = jnp.concatenate(
        [jnp.broadcast_to(cum[(i + 1) * c - 1:(i + 1) * c, :], (c, GW)) for i in range(rows // c)], axis=0)
    e_cum = jnp.exp(cum)
    e_inv = jnp.exp(-cum)
    e_rem = jnp.exp(cum_end - cum)
    coef = _head_sum(r * k_mod * rk_ref[...], ones_ref[...])
    outs = ((at_o, -kk * jnp.exp(cum - log_decay)), (rt_o, r * e_cum), (bt_o, kb * e_inv), (kt_o, k_mod * e_inv),
            (be_o, kb * e_rem), (ke_o, k_mod * e_rem), (v_o, v), (ee_o, jnp.exp(cum_end)), (g_o, g),
            (bg_o, coef * v * g))
    for ref, val in outs:
        for pr in range(N_PAIRS):
            ref[0, pr] = val[:, pr * LANES:(pr + 1) * LANES].astype(ref.dtype)


def _rwkv_prep(proj3, mu, w0, w2, a0, a2, g2, k_k, k_a, r_k, rows=256):
    b, s, _ = proj3.shape
    rows = min(rows, s)
    pad = lambda w, n: jnp.concatenate([w, jnp.zeros((n - w.shape[0],) + w.shape[1:], w.dtype)], axis=0)
    head_of = np.arange(GW) // HEAD_DIM
    ones_bd = jnp.asarray((head_of[:, None] == head_of[None, :]).astype(np.float32), dtype=BF16)
    tok = np.arange(rows)
    tri_bd = jnp.asarray(((tok[:, None] // RWKV_CHUNK == tok[None, :] // RWKV_CHUNK)
                          & (tok[None, :] <= tok[:, None])).astype(np.float32), dtype=BF16)
    vec = lambda a: a.reshape(1, -1)
    full = lambda shape: pl.BlockSpec(shape, lambda i, j: (0,) * len(shape))
    col = COL_D // RWKV_W
    sub = rows // 8
    sd = lambda dt: jax.ShapeDtypeStruct((b, N_PAIRS, s, LANES), dt)
    return pl.pallas_call(
        functools.partial(_rwkv_prep_kernel, rows=rows),
        out_shape=[sd(BF16)] * 7 + [sd(F32)] * 3,
        grid=(b, s // rows),
        in_specs=[pl.BlockSpec((1, rows, RWKV_W), lambda i, j: (i, j, col)),
                  pl.BlockSpec((1, 8, RWKV_W), lambda i, j: (i, jnp.maximum(j * sub - 1, 0), col)),
                  full((1, RWKV_W)), full((1, GW)), full((LORA_PAD, GW)), full((1, GW)),
                  full((LORA_PAD, GW)), full((G_LORA, GW)), full((1, GW)), full((1, GW)), full((1, GW)),
                  full((GW, GW)), full((rows, rows))],
        out_specs=[pl.BlockSpec((1, N_PAIRS, rows, LANES), lambda i, j: (i, 0, j, 0))] * 10,
        compiler_params=_cparams(("arbitrary", "arbitrary")),
        name="rwkv_prep",
    )(proj3, proj3, vec(mu), vec(w0), pad(w2, LORA_PAD).astype(BF16), vec(a0), pad(a2, LORA_PAD).astype(BF16),
      g2.astype(BF16), vec(k_k), vec(k_a), vec(r_k), ones_bd, tri_bd)


def _block_diag(y, first):
    zero = jnp.zeros_like(y)
    return jnp.concatenate([jnp.where(first, y, zero), jnp.where(first, zero, y)], axis=0)


def _pair_nn(x, y, first):
    return _dot(x, _block_diag(y, first))


def _pair_nt(x, y, first):
    return _dot_nt(x, _block_diag(y, first))


def _pair_tn(x, y, first):
    full = _dot_tn(x, y)
    return jnp.where(first, full[:HEAD_DIM], full[HEAD_DIM:])


def _rwkv_chunk_terms(probs, masks):
    first, strict, incl, same_sub, eye = masks
    c = RWKV_CHUNK
    bf = lambda xs: [x.astype(BF16) for x in xs]
    nn = lambda xs, ys: [_pair_nn(x, y, first) for x, y in zip(xs, ys)]
    ident = eye.astype(F32)
    at, rt, bt, kt, b_end, k_end, v, e_end = [list(x) for x in zip(*probs)]
    ar = [jnp.concatenate([a, r], axis=0) for a, r in zip(at, rt)]
    gb = [_pair_nt(x, y, first) for x, y in zip(ar, bt)]
    gk = [_pair_nt(x, y, first) for x, y in zip(ar, kt)]
    a_ab = [jnp.where(strict, g[:c], 0.0) for g in gb]
    a_rb = bf([jnp.where(incl, g[c:], 0.0) for g in gb])
    a_kr = bf([jnp.concatenate([jnp.where(strict, g[:c], 0.0), jnp.where(incl, g[c:], 0.0)], axis=0) for g in gk])
    ad = [jnp.where(same_sub, a, 0.0) for a in a_ab]
    an = bf([a - d for a, d in zip(a_ab, ad)])
    adb = bf(ad)
    p2 = nn(adb, adb)
    av = nn(a_kr, v)
    p2b = bf(p2)
    p4 = nn(p2b, p2b)
    x1 = nn(bf([ident + d for d in ad]), bf([ident + p for p in p2]))
    p4b = bf(p4)
    p8 = nn(p4b, p4b)
    x2 = nn(bf([ident + p for p in p4]), bf([ident + p for p in p8]))
    td = bf(nn(bf(x1), bf(x2)))
    m1 = nn(td, an)
    m1b = bf(m1)
    m2 = nn(m1b, m1b)
    x3 = nn(bf([ident + m for m in m1]), bf([ident + m for m in m2]))
    t_inv = bf(nn(bf(x3), td))
    a_hat = bf(nn(t_inv, at))
    u0 = bf(nn(t_inv, bf([w[:c] for w in av])))
    r_hat = [r.astype(F32) + x for r, x in zip(rt, nn(a_rb, a_hat))]
    o0 = [x + w[c:] for x, w in zip(nn(a_rb, u0), av)]
    p_mat = [jnp.where(eye, e, 0.0) + _pair_tn(h, b, first) for e, h, b in zip(e_end, a_hat, b_end)]
    z_mat = [_pair_tn(jnp.concatenate([u, w], axis=0), jnp.concatenate([b, k], axis=0), first)
             for u, w, b, k in zip(u0, v, b_end, k_end)]
    return list(zip(p_mat, z_mat, r_hat, o0))


def _rwkv_scan_kernel(at_ref, rt_ref, bt_ref, kt_ref, be_ref, ke_ref, v_ref, ee_ref, g_ref, bg_ref, lng_ref,
                      lnb_ref, o_ref, state, *, rows):
    c = RWKV_CHUNK
    ri = lax.broadcasted_iota(jnp.int32, (c, LANES), 0)
    lane = lax.broadcasted_iota(jnp.int32, (c, LANES), 1)
    first = lane < HEAD_DIM
    ci = jnp.where(first, lane, lane - HEAD_DIM)
    masks = (first, ci < ri, ci <= ri, (ri // RWKV_SUB) == (ci // RWKV_SUB), ri == ci)

    @pl.when(pl.program_id(1) == 0)
    def _():
        state[...] = jnp.zeros_like(state)

    def head_mean(x):
        lo = jnp.sum(jnp.where(first, x, 0.0), axis=-1, keepdims=True)
        hi = jnp.sum(jnp.where(first, 0.0, x), axis=-1, keepdims=True)
        return jnp.where(first, lo, hi) * (1.0 / HEAD_DIM)

    chunks = [slice(i * c, (i + 1) * c) for i in range(rows // c)]
    seq_refs = (at_ref, rt_ref, bt_ref, kt_ref, be_ref, ke_ref, v_ref)
    probs = [tuple(ref[0, pr, rs, :] for ref in seq_refs) + (ee_ref[0, pr, rs.start:rs.start + 1, :],)
             for rs in chunks for pr in range(N_PAIRS)]
    terms = _rwkv_chunk_terms(probs, masks)
    s_cur = [state[pr] for pr in range(N_PAIRS)]
    for i, rs in enumerate(chunks):
        for pr in range(N_PAIRS):
            p_mat, z_mat, r_hat, o0 = terms[i * N_PAIRS + pr]
            s_hi, s_lo = _split2(s_cur[pr])
            p_hi, p_lo = _split2(p_mat)
            o = _pair_nt(r_hat.astype(BF16), s_hi, first) + o0
            s_cur[pr] = (_pair_nn(s_hi, p_hi, first) + _pair_nn(s_hi, p_lo, first) + _pair_nn(s_lo, p_hi, first)
                         + z_mat)
            oc = o - head_mean(o)
            y = oc * lax.rsqrt(head_mean(oc * oc) + RWKV_LN_EPS) * lng_ref[pr:pr + 1, :] + lnb_ref[pr:pr + 1, :]
            o_ref[0, rs, pr * LANES:(pr + 1) * LANES] = y * g_ref[0, pr, rs, :] + bg_ref[0, pr, rs, :]
    for pr in range(N_PAIRS):
        state[pr] = s_cur[pr]


def _rwkv_scan(prep, lnx_g, lnx_b, rows=512):
    b, _, s, _ = prep[0].shape
    rows = min(rows, s)
    seq_spec = pl.BlockSpec((1, N_PAIRS, rows, LANES), lambda i, t: (i, 0, t, 0))
    par_spec = pl.BlockSpec((N_PAIRS, LANES), lambda i, t: (0, 0))
    par = lambda a: a.reshape(N_PAIRS, LANES)
    return pl.pallas_call(
        functools.partial(_rwkv_scan_kernel, rows=rows),
        out_shape=jax.ShapeDtypeStruct((b, s, GW), F32),
        grid=(b, s // rows),
        in_specs=[seq_spec] * 10 + [par_spec] * 2,
        out_specs=pl.BlockSpec((1, rows, GW), lambda i, t: (i, t, 0)),
        scratch_shapes=[pltpu.VMEM((N_PAIRS, HEAD_DIM, LANES), F32)],
        compiler_params=_cparams(("arbitrary", "arbitrary")),
        name="rwkv_scan",
    )(*prep, par(lnx_g), par(lnx_b))


def _outproj_kernel(ya_ref, yb_ref, yc_ref, yd_ref, g_ref, w_ref, x_ref, o_ref):
    acc = x_ref[...]
    for i, y_ref in enumerate((ya_ref, yb_ref, yc_ref, yd_ref)):
        yn = _rms(y_ref[...], g_ref[i:i + 1, :]).astype(BF16)
        acc = acc + _dot(yn, w_ref[i * GW:(i + 1) * GW, :])
    o_ref[...] = acc


def _outproj(ys, g, w_all, layer, x2d, tm=512):
    t, d = x2d.shape
    tm = min(tm, t)
    y_spec = pl.BlockSpec((tm, GW), lambda i: (i, 0))
    return pl.pallas_call(
        _outproj_kernel,
        out_shape=jax.ShapeDtypeStruct((t, d), F32),
        grid=(t // tm,),
        in_specs=[y_spec] * 4 + [pl.BlockSpec((4, GW), lambda i: (0, 0)),
                                 pl.BlockSpec((None, 4 * GW, d), lambda i: (layer, 0, 0)),
                                 pl.BlockSpec((tm, d), lambda i: (i, 0))],
        out_specs=pl.BlockSpec((tm, d), lambda i: (i, 0)),
        compiler_params=_cparams(("arbitrary",)),
        name="outproj",
    )(*[y.reshape(t, GW) for y in ys], g.reshape(4, GW), w_all, x2d)


def _ffn_kernel(x_ref, g_ref, wg_ref, wu_ref, wd_ref, gf_ref, o_ref, h_scr, *, final_norm):
    j = pl.program_id(1)

    @pl.when(j == 0)
    def _():
        x = x_ref[...]
        h_scr[...] = _rms(x, g_ref[...]).astype(BF16)
        o_ref[...] = x

    h = h_scr[...]
    gate = _dot(h, wg_ref[...].astype(BF16))
    up = _dot(h, wu_ref[...].astype(BF16))
    act = (gate * _sigmoid(gate) * up).astype(BF16)
    o_ref[...] += _dot(act, wd_ref[...].astype(BF16))

    if final_norm:
        @pl.when(j == pl.num_programs(1) - 1)
        def _():
            o_ref[...] = _rms(o_ref[...], gf_ref[...])


def _ffn(x2d, g, wg, wu, wd, layer, g_final, final_norm, tm=1024, tf=256):
    t, d = x2d.shape
    f = wg.shape[2]
    tm = min(tm, t)
    return pl.pallas_call(
        functools.partial(_ffn_kernel, final_norm=final_norm),
        out_shape=jax.ShapeDtypeStruct((t, d), F32),
        grid=(t // tm, f // tf),
        in_specs=[pl.BlockSpec((tm, d), lambda i, j: (i, 0)),
                  pl.BlockSpec((1, d), lambda i, j: (0, 0)),
                  pl.BlockSpec((None, d, tf), lambda i, j: (layer, 0, j)),
                  pl.BlockSpec((None, d, tf), lambda i, j: (layer, 0, j)),
                  pl.BlockSpec((None, tf, d), lambda i, j: (layer, j, 0)),
                  pl.BlockSpec((1, d), lambda i, j: (0, 0))],
        out_specs=pl.BlockSpec((tm, d), lambda i, j: (i, 0)),
        scratch_shapes=[pltpu.VMEM((tm, d), BF16)],
        compiler_params=_cparams(("arbitrary", "arbitrary")),
        name="ffn",
    )(x2d, g.reshape(1, d), wg, wu, wd, g_final.reshape(1, d))


def _cast_kernel(x_ref, o_ref):
    o_ref[...] = x_ref[...].astype(o_ref.dtype)


def _to_bf16(w, rows=256):
    shape = w.shape
    w2 = w.reshape(-1, shape[-1])
    n, c = w2.shape
    out = pl.pallas_call(
        _cast_kernel,
        out_shape=jax.ShapeDtypeStruct((n, c), BF16),
        grid=(n // rows,),
        in_specs=[pl.BlockSpec((rows, c), lambda i: (i, 0))],
        out_specs=pl.BlockSpec((rows, c), lambda i: (i, 0)),
        compiler_params=_cparams(("arbitrary",)),
        name="cast_bf16",
    )(w2)
    return out.reshape(shape)


_RWKV_SRC = np.concatenate([[0], np.cumsum((GW, W_LORA, GW, GW, A_LORA, G_LORA))])
_RWKV_DST = (0, 3 * GW, GW, 2 * GW, 3 * GW + LORA_PAD, 3 * GW + 2 * LORA_PAD)


def _reorder_rwkv(a):
    out = jnp.zeros(a.shape[:-1] + (RWKV_W,), a.dtype)
    for i, dst in enumerate(_RWKV_DST):
        lo, hi = int(_RWKV_SRC[i]), int(_RWKV_SRC[i + 1])
        out = out.at[..., dst:dst + hi - lo].set(a[..., lo:hi])
    return out


def _w_in_kernel(w_ref, o_ref):
    o_ref[:, :COL_D] = w_ref[:, :COL_D].astype(BF16)
    o_ref[:, COL_D:] = jnp.zeros((o_ref.shape[0], RWKV_W), BF16)
    for i, dst in enumerate(_RWKV_DST):
        lo, hi = int(_RWKV_SRC[i]), int(_RWKV_SRC[i + 1])
        o_ref[:, COL_D + dst:COL_D + dst + hi - lo] = w_ref[:, COL_D + lo:COL_D + hi].astype(BF16)


def _prepare_w_in(w_in, rows=256):
    depth, d, n = w_in.shape
    out = pl.pallas_call(
        _w_in_kernel,
        out_shape=jax.ShapeDtypeStruct((depth * d, D_PROJ), BF16),
        grid=(depth * d // rows,),
        in_specs=[pl.BlockSpec((rows, n), lambda i: (i, 0))],
        out_specs=pl.BlockSpec((rows, D_PROJ), lambda i: (i, 0)),
        compiler_params=_cparams(("arbitrary",)),
        name="w_in_layout",
    )(w_in.reshape(depth * d, n))
    return out.reshape(depth, d, D_PROJ)


def kernel(x, norm_mix_g, w_in, pos_bias, sgu_ln_g, sgu_w, sgu_b, rwkv_mu, rwkv_w0, rwkv_w2, rwkv_a0, rwkv_a2,
           rwkv_g2, rwkv_k_k, rwkv_k_a, rwkv_r_k, rwkv_lnx_g, rwkv_lnx_b, branch_norm_g, w_out, norm_ffn_g,
           w_gate, w_up, w_down, norm_final_g):
    b, s, d = x.shape
    depth = w_in.shape[0]
    assert s % (DIL_BLOCK * DIL_PATTERNS[-1][1]) == 0 and s % (2 * MOBA_BLOCK) == 0
    bias_a = _bias_tiles(pos_bias, _dil_bucket_tiles(), 0)
    bias_c = _bias_tiles(pos_bias, _moba_bucket_tiles(s // MOBA_BLOCK), N_HEADS)
    x2d = x.reshape(b * s, d)
    w_out_b = _to_bf16(w_out)
    w_in_b, mu_all = _prepare_w_in(w_in), _reorder_rwkv(rwkv_mu)
    for l in range(depth):
        mu_l = mu_all[l]
        proj3 = _inproj(x2d, norm_mix_g[l], w_in_b, l).reshape(b, s, D_PROJ)
        ya = _mixer_dilated(proj3, bias_a)
        yb = _mixer_sgu(proj3, sgu_ln_g[l], sgu_w[l], sgu_b[l])
        yc = _mixer_moba(proj3, bias_c)
        prep = _rwkv_prep(proj3, mu_l, rwkv_w0[l], rwkv_w2[l], rwkv_a0[l], rwkv_a2[l], rwkv_g2[l],
                          rwkv_k_k[l], rwkv_k_a[l], rwkv_r_k[l])
        yd = _rwkv_scan(prep, rwkv_lnx_g[l], rwkv_lnx_b[l])
        x2d = _outproj((ya, yb, yc, yd), branch_norm_g[l], w_out_b, l, x2d)
        x2d = _ffn(x2d, norm_ffn_g[l], w_gate, w_up, w_down, l, norm_final_g,
                   final_norm=(l == depth - 1))
    return x2d.reshape(b, s, d)
```

```python
import functools
import math

import jax
import jax.numpy as jnp
import numpy as np
from jax import lax
from jax.experimental import pallas as pl
from jax.experimental.pallas import tpu as pltpu

F32 = jnp.float32
BF16 = jnp.bfloat16

HEAD_DIM = 64
N_HEADS = 8
GW = N_HEADS * HEAD_DIM
LANES = 128
N_PAIRS = GW // LANES
DIL_PATTERNS = ((128, 1), (512, 4), (2048, 16))
DIL_BLOCK = 128
SGU_CHUNK = 128
SGU_LN_EPS = 1e-5
MOBA_BLOCK = 256
MOBA_TOPK = 3
W_LORA = 96
A_LORA = 96
G_LORA = 256
LORA_PAD = 128
RWKV_LN_EPS = 64e-5
RWKV_CHUNK = 64
RWKV_SUB = 16
NUM_BUCKETS = 32
MAX_DISTANCE = 2048
NORM_EPS = 1e-6
NEG_INF = -1e30
ATT_SCALE = HEAD_DIM ** -0.5
LOG2E = math.log2(math.e)

COL_A = 0
COL_B = 3 * GW
COL_C = COL_B + 2 * GW
COL_D = COL_C + 3 * GW
RWKV_W = 3 * GW + 2 * LORA_PAD + G_LORA
D_PROJ = COL_D + RWKV_W

VMEM_LIMIT = 56 * 1024 * 1024

HIGHEST = lax.Precision.HIGHEST


def _cparams(sem):
    return pltpu.CompilerParams(dimension_semantics=sem, vmem_limit_bytes=VMEM_LIMIT)


def _dot(a, b, precision=None):
    return lax.dot_general(a, b, (((1,), (0,)), ((), ())), precision=precision,
                           preferred_element_type=F32)


def _dot_nt(a, b, precision=None):
    return lax.dot_general(a, b, (((1,), (1,)), ((), ())), precision=precision,
                           preferred_element_type=F32)


def _dot_tn(a, b, precision=None):
    return lax.dot_general(a, b, (((0,), (0,)), ((), ())), precision=precision,
                           preferred_element_type=F32)


def _rms(x, g):
    return x * lax.rsqrt(jnp.mean(x * x, axis=-1, keepdims=True) + NORM_EPS) * g


def _t5_bucket_np(dist):
    dist = np.maximum(dist, 0)
    max_exact = NUM_BUCKETS // 2
    d = np.maximum(dist, 1).astype(np.float32)
    large = max_exact + (np.log(d / np.float32(max_exact)) / np.float32(math.log(MAX_DISTANCE / max_exact))
                         * np.float32(NUM_BUCKETS - max_exact)).astype(np.int32)
    large = np.minimum(large, NUM_BUCKETS - 1)
    return np.where(dist < max_exact, dist, large).astype(np.int32)


def _dil_bucket_tiles():
    qa = np.arange(DIL_BLOCK)[:, None]
    kj = np.arange(DIL_BLOCK)[None, :]
    tiles = []
    for _, dil in DIL_PATTERNS:
        tiles.append(_t5_bucket_np((qa - kj) * dil))
        tiles.append(_t5_bucket_np((qa + DIL_BLOCK - kj) * dil))
    return np.stack(tiles)


def _moba_bucket_tiles(nblk):
    ki = np.arange(MOBA_BLOCK)[:, None]
    qi = np.arange(MOBA_BLOCK)[None, :]
    return np.stack([_t5_bucket_np(db * MOBA_BLOCK + qi - ki) for db in range(nblk)])


def _bias_tile_kernel(tbl_ref, idx_ref, o_ref, *, head_offset, buckets):
    h = pl.program_id(0) + head_offset
    for t, present in enumerate(buckets):
        idx = idx_ref[t]
        acc = jnp.zeros(idx.shape, F32)
        for b in present:
            acc = jnp.where(idx == b, tbl_ref[b, h] * LOG2E, acc)
        o_ref[0, t] = acc


def _bias_tiles(pos_bias, idx_np, head_offset):
    nt, r, c = idx_np.shape
    buckets = tuple(tuple(int(b) for b in np.unique(idx_np[t])) for t in range(nt))
    return pl.pallas_call(
        functools.partial(_bias_tile_kernel, head_offset=head_offset, buckets=buckets),
        out_shape=jax.ShapeDtypeStruct((N_HEADS, nt, r, c), F32),
        grid=(N_HEADS,),
        in_specs=[pl.BlockSpec(memory_space=pltpu.SMEM),
                  pl.BlockSpec((nt, r, c), lambda h: (0, 0, 0))],
        out_specs=pl.BlockSpec((1, nt, r, c), lambda h: (h, 0, 0, 0)),
        compiler_params=_cparams(("arbitrary",)),
        name="bias_tiles",
    )(pos_bias, jnp.asarray(idx_np))


def _inproj_kernel(x_ref, g_ref, w_ref, o_ref, h_scr):
    @pl.when(pl.program_id(1) == 0)
    def _():
        h_scr[...] = _rms(x_ref[...], g_ref[...]).astype(BF16)

    o_ref[...] = _dot_nt(h_scr[...], w_ref[...])


def _inproj(x2d, g, wt_all, layer, tm=1024, tn=1024):
    t, d = x2d.shape
    n = wt_all.shape[1]
    tm = min(tm, t)
    return pl.pallas_call(
        _inproj_kernel,
        out_shape=jax.ShapeDtypeStruct((t, n), F32),
        grid=(t // tm, n // tn),
        in_specs=[pl.BlockSpec((tm, d), lambda i, j: (i, 0)),
                  pl.BlockSpec((1, d), lambda i, j: (0, 0)),
                  pl.BlockSpec((None, tn, d), lambda i, j: (layer, j, 0))],
        out_specs=pl.BlockSpec((tm, tn), lambda i, j: (i, j)),
        scratch_shapes=[pltpu.VMEM((tm, d), BF16)],
        compiler_params=_cparams(("arbitrary", "arbitrary")),
        name="inproj",
    )(x2d, g.reshape(1, d), wt_all)


def _dilated_kernel(q_ref, k_ref, v_ref, bias_ref, o_ref, m0, m1, l0, l1, acc, *, seq):
    c = DIL_BLOCK
    lane = lax.broadcasted_iota(jnp.int32, (c, LANES), 1)
    head0 = lane < HEAD_DIM
    row = lax.broadcasted_iota(jnp.int32, (c, c), 0)
    col = lax.broadcasted_iota(jnp.int32, (c, c), 1)
    cur_valid = col <= row
    prev_valid = col >= row

    stats = ((m0, l0), (m1, l1))
    head_lanes = (head0, ~head0)

    def logits(pairs, pi, wide, first):
        valid = jnp.concatenate([prev_valid, cur_valid], axis=1) if wide else cur_valid
        bias = [jnp.concatenate([bias_ref[h, 2 * pi + 1], bias_ref[h, 2 * pi]], axis=1) if wide
                else bias_ref[h, 2 * pi] for h in range(2)]
        q = [q_ref[0, qsl, :] * (ATT_SCALE * LOG2E) for qsl, _ in pairs]
        kb = [k_ref[0, ksl, :].astype(BF16) for _, ksl in pairs]
        idx = [(h, i) for i in range(len(pairs)) for h in range(2)]
        s = {(h, i): _dot_nt(jnp.where(head_lanes[h], q[i], 0.0).astype(BF16), kb[i]) for h, i in idx}
        s = {hi: jnp.where(valid, s[hi] + bias[hi[0]], NEG_INF) for hi in idx}
        return pairs, wide, first, idx, s

    def softmax_update(pairs, wide, first, idx, s):
        rep = (lambda x: jnp.concatenate([x, x], axis=1)) if wide else (lambda x: x)
        ones = jnp.ones(((2 if wide else 1) * c, LANES), BF16)
        vb = [jnp.concatenate([v_ref[0, ksl, :].astype(BF16), ones], axis=1) for _, ksl in pairs]
        m_new = {hi: jnp.broadcast_to(jnp.max(s[hi], axis=1, keepdims=True), (c, LANES)) for hi in idx}
        if not first:
            m_old = {(h, i): stats[h][0][pairs[i][0], :] for h, i in idx}
            m_new = {hi: jnp.maximum(m_old[hi], m_new[hi]) for hi in idx}
            alpha = {hi: jnp.exp2(m_old[hi] - m_new[hi]) for hi in idx}
        p = {hi: jnp.exp2(s[hi] - rep(m_new[hi])).astype(BF16) for hi in idx}
        o = {(h, i): _dot(p[h, i], vb[i]) for h, i in idx}
        for h, i in idx:
            m_ref, l_ref = stats[h]
            qsl = pairs[i][0]
            l_new = o[h, i][:, LANES:]
            l_ref[qsl, :] = l_new if first else alpha[h, i] * l_ref[qsl, :] + l_new
            m_ref[qsl, :] = m_new[h, i]
        for i, (qsl, _) in enumerate(pairs):
            o_new = jnp.where(head0, o[0, i][:, :LANES], o[1, i][:, :LANES])
            acc[qsl, :] = o_new if first else acc[qsl, :] * jnp.where(head0, alpha[0, i], alpha[1, i]) + o_new

    def group_size(n, cap):
        return max(g for g in range(1, cap + 1) if n % g == 0)

    order = sorted(range(len(DIL_PATTERNS)), key=lambda i: -DIL_PATTERNS[i][1])
    groups = []
    for pi in order:
        dil = DIL_PATTERNS[pi][1]
        first_pass = pi == order[0]
        nb = seq // dil // c

        def sl(r, n, blocks, dil=dil):
            start = r + n * (c * dil)
            return pl.ds(start, blocks * c) if dil == 1 else pl.ds(start, blocks * c, stride=dil)

        heads = [(sl(r, 0, 1),) * 2 for r in range(dil)]
        body = [(sl(r, n, 1), sl(r, n - 1, 2)) for r in range(dil) for n in range(1, nb)]
        for tiles, wide, cap in ((heads, False, 8), (body, True, 4)):
            g = group_size(len(tiles), cap) if tiles else 1
            groups += [(tiles[i:i + g], pi, wide, first_pass) for i in range(0, len(tiles), g)]

    pending = logits(*groups[0])
    for grp in groups[1:]:
        nxt = logits(*grp)
        softmax_update(*pending)
        pending = nxt
    softmax_update(*pending)

    o_ref[0] = acc[...] / jnp.where(head0[:1], l0[...], l1[...])


def _mixer_dilated(proj3, bias_a):
    b, s, _ = proj3.shape
    blk = lambda off: pl.BlockSpec((1, s, LANES), lambda p, i, off=off: (i, 0, off + p))
    return pl.pallas_call(
        functools.partial(_dilated_kernel, seq=s),
        out_shape=jax.ShapeDtypeStruct((b, s, GW), F32),
        grid=(N_PAIRS, b),
        in_specs=[blk(COL_A // LANES), blk((COL_A + GW) // LANES), blk((COL_A + 2 * GW) // LANES),
                  pl.BlockSpec((2, 2 * len(DIL_PATTERNS), DIL_BLOCK, DIL_BLOCK),
                               lambda p, i: (p, 0, 0, 0))],
        out_specs=pl.BlockSpec((1, s, LANES), lambda p, i: (i, 0, p)),
        scratch_shapes=[pltpu.VMEM((s, LANES), F32)] * 5,
        compiler_params=_cparams(("arbitrary", "arbitrary")),
        name="mixer_dilated",
    )(proj3, proj3, proj3, bias_a)


def _gelu_tanh(x):
    return 0.5 * x * (1.0 + jnp.tanh(math.sqrt(2.0 / math.pi) * (x + 0.044715 * (x * x * x))))


def _sgu_kernel(u_ref, v_ref, lng_ref, w_ref, bias_ref, o_ref, *, rows):
    t = SGU_CHUNK
    u = _gelu_tanh(u_ref[0])
    v = _gelu_tanh(v_ref[0])
    mu = jnp.mean(v, axis=-1, keepdims=True)
    vc = v - mu
    var = jnp.mean(vc * vc, axis=-1, keepdims=True)
    vn = (vc * lax.rsqrt(var + SGU_LN_EPS) * lng_ref[...]).astype(BF16)
    r2 = lax.broadcasted_iota(jnp.int32, (2 * t, t), 0)
    c2 = lax.broadcasted_iota(jnp.int32, (2 * t, t), 1)
    causal = c2 <= jnp.where(r2 >= t, r2 - t, r2)
    first_group = lax.broadcasted_iota(jnp.int32, (t, LANES), 1) < HEAD_DIM
    for p in range(N_PAIRS):
        wp = jnp.where(causal, w_ref[p], 0.0).astype(BF16)
        for ci in range(rows // t):
            rs = slice(ci * t, (ci + 1) * t)
            cs = slice(p * LANES, (p + 1) * LANES)
            res = _dot(wp, vn[rs, cs])
            mixed = jnp.where(first_group, res[:t], res[t:]) + bias_ref[:, cs]
            o_ref[0, rs, cs] = u[rs, cs] * mixed


def _mixer_sgu(proj3, ln_g, w_s, b_s, rows=512):
    b, s, _ = proj3.shape
    rows = min(rows, s)
    t = SGU_CHUNK
    bias_full = jnp.repeat(b_s.T, HEAD_DIM, axis=1)
    w_pairs = w_s.reshape(N_PAIRS, 2 * t, t)
    return pl.pallas_call(
        functools.partial(_sgu_kernel, rows=rows),
        out_shape=jax.ShapeDtypeStruct((b, s, GW), F32),
        grid=(b, s // rows),
        in_specs=[pl.BlockSpec((1, rows, GW), lambda i, j: (i, j, COL_B // GW)),
                  pl.BlockSpec((1, rows, GW), lambda i, j: (i, j, COL_B // GW + 1)),
                  pl.BlockSpec((1, GW), lambda i, j: (0, 0)),
                  pl.BlockSpec((N_PAIRS, 2 * t, t), lambda i, j: (0, 0, 0)),
                  pl.BlockSpec((t, GW), lambda i, j: (0, 0))],
        out_specs=pl.BlockSpec((1, rows, GW), lambda i, j: (i, j, 0)),
        compiler_params=_cparams(("arbitrary", "arbitrary")),
        name="mixer_sgu",
    )(proj3, proj3, ln_g.reshape(1, GW), w_pairs, bias_full)


def _moba_kernel(q_ref, k_ref, v_ref, bias_ref, o_ref, kh_scr, vt_scr, ot_scr, *, seq):
    ones_rows = 16
    bs = MOBA_BLOCK
    nblk = seq // bs
    lane = lax.broadcasted_iota(jnp.int32, (1, LANES), 1)
    head_lanes = (lane < HEAD_DIM, lane >= HEAD_DIM)
    blk = lambda i: slice(i * bs, (i + 1) * bs)

    q_all = q_ref[0]
    kbar = jnp.concatenate([jnp.mean(k_ref[0, blk(j), :], axis=0, keepdims=True) for j in range(nblk)], axis=0)
    for j in range(nblk):
        kj = k_ref[0, blk(j), :]
        for h in range(2):
            kh_scr[h, j] = jnp.where(head_lanes[h], kj, 0.0).astype(BF16)
        vt = v_ref[0, blk(j), :].T.astype(BF16)
        for h in range(2):
            vt_scr[j, h] = jnp.concatenate(
                [vt[h * HEAD_DIM:(h + 1) * HEAD_DIM], jnp.ones((ones_rows, bs), BF16)], axis=0)

    jrow = lax.broadcasted_iota(jnp.int32, (nblk, seq), 0)
    own = lax.broadcasted_iota(jnp.int32, (nblk, seq), 1) // bs
    krow = lax.broadcasted_iota(jnp.int32, (bs, bs), 0)
    qcol = lax.broadcasted_iota(jnp.int32, (bs, bs), 1)
    causal = krow <= qcol

    kb2 = jnp.concatenate([jnp.where(head_lanes[0], kbar, 0.0), jnp.where(head_lanes[1], kbar, 0.0)], axis=0)
    kb_hi, kb_lo = _split2(kb2)
    q_hi, q_lo = _split2(q_all)
    gates = _dot_nt(kb_hi, q_hi) + _dot_nt(kb_hi, q_lo) + _dot_nt(kb_lo, q_hi)
    sel = []
    for h in range(2):
        gate = jnp.where(jrow < own, gates[h * nblk:(h + 1) * nblk], NEG_INF)
        rank = jnp.zeros((nblk, seq), jnp.int32)
        for j2 in range(nblk):
            gj = gate[j2:j2 + 1, :]
            ahead = (gj > gate) | ((gj == gate) & (j2 < jrow))
            rank = rank + ahead.astype(jnp.int32)
        sel.append((rank < MOBA_TOPK) & (jrow < own))

    def logits(grp):
        probs = [(h, qb) for qb in (grp, nblk - 1 - grp) for h in range(2)]
        tiles = [(h, qb, j) for h, qb in probs for j in range(qb + 1)]
        qh = {qb: (q_ref[0, blk(qb), :] * (ATT_SCALE * LOG2E)).astype(BF16) for _, qb in probs}
        s = {(h, qb, j): _dot_nt(kh_scr[h, j], qh[qb]) + bias_ref[h, qb - j] for h, qb, j in tiles}
        for h, qb in probs:
            s[h, qb, qb] = jnp.where(causal, s[h, qb, qb], NEG_INF)
        return probs, tiles, s

    def softmax_pv(probs, tiles, s):
        picked = {(h, qb, j): sel[h][j:j + 1, blk(qb)] for h, qb, j in tiles if j != qb}
        cmax = {t: jnp.max(s[t], axis=0, keepdims=True) for t in tiles}
        m = {}
        for h, qb in probs:
            m[h, qb] = functools.reduce(
                jnp.maximum, [cmax[h, qb, qb]] + [jnp.where(picked[h, qb, j], cmax[h, qb, j], NEG_INF)
                                                  for j in range(qb)])
        shift = {t: m[t[0], t[1]] if t[2] == t[1] else jnp.where(picked[t], m[t[0], t[1]], -NEG_INF)
                 for t in tiles}
        p = {t: jnp.exp2(s[t] - shift[t]).astype(BF16) for t in tiles}
        for h, qb in probs:
            acc = sum(_dot(vt_scr[j, h], p[h, qb, j]) for j in range(qb + 1))
            ot_scr[qb, h * HEAD_DIM:(h + 1) * HEAD_DIM, :] = acc[:HEAD_DIM] / acc[HEAD_DIM:HEAD_DIM + 1]

    pending = logits(0)
    for grp in range(1, nblk // 2):
        nxt = logits(grp)
        softmax_pv(*pending)
        pending = nxt
    softmax_pv(*pending)

    for qb in range(nblk):
        o_ref[0, blk(qb), :] = ot_scr[qb].T


def _mixer_moba(proj3, bias_c):
    b, s, _ = proj3.shape
    nblk = s // MOBA_BLOCK
    blk = lambda off: pl.BlockSpec((1, s, LANES), lambda p, i, off=off: (i, 0, off + p))
    return pl.pallas_call(
        functools.partial(_moba_kernel, seq=s),
        out_shape=jax.ShapeDtypeStruct((b, s, GW), F32),
        grid=(N_PAIRS, b),
        in_specs=[blk(COL_C // LANES), blk((COL_C + GW) // LANES), blk((COL_C + 2 * GW) // LANES),
                  pl.BlockSpec((2, nblk, MOBA_BLOCK, MOBA_BLOCK), lambda p, i: (p, 0, 0, 0))],
        out_specs=pl.BlockSpec((1, s, LANES), lambda p, i: (i, 0, p)),
        scratch_shapes=[pltpu.VMEM((2, nblk, MOBA_BLOCK, LANES), BF16),
                        pltpu.VMEM((nblk, 2, HEAD_DIM + 16, MOBA_BLOCK), BF16),
                        pltpu.VMEM((nblk, LANES, MOBA_BLOCK), F32)],
        compiler_params=_cparams(("arbitrary", "arbitrary")),
        name="mixer_moba",
    )(proj3, proj3, proj3, bias_c)


def _sigmoid(x):
    return 1.0 / (1.0 + jnp.exp(-x))


def _split2(x):
    hi = x.astype(BF16)
    return hi, (x - hi.astype(F32)).astype(BF16)


def _head_sum(x, ones_bd):
    hi, lo = _split2(x)
    return _dot(hi, ones_bd) + _dot(lo, ones_bd)


def _rwkv_prep_kernel(p_ref, prev_ref, mu_ref, w0_ref, w2_ref, a0_ref, a2_ref, g2_ref, kk_ref, ka_ref, rk_ref,
                      ones_ref, tri_ref, at_o, rt_o, bt_o, kt_o, be_o, ke_o, v_o, ee_o, g_o, bg_o, *, rows):
    c = RWKV_CHUNK
    p = p_ref[0]
    prev_row = jnp.where(pl.program_id(1) == 0, 0.0, prev_ref[0, 7:8, :])
    first_row = lax.broadcasted_iota(jnp.int32, (rows, 1), 0) == 0
    y_prev = jnp.where(first_row, prev_row, pltpu.roll(p, 1, axis=0))
    xs = p + (y_prev - p) * mu_ref[...]
    r = xs[:, 0:GW]
    k = xs[:, GW:2 * GW]
    v = xs[:, 2 * GW:3 * GW]
    wd = xs[:, 3 * GW:3 * GW + LORA_PAD]
    ad = xs[:, 3 * GW + LORA_PAD:3 * GW + 2 * LORA_PAD]
    gd = xs[:, 3 * GW + 2 * LORA_PAD:]
    nz = -(w0_ref[...] + _dot(jnp.tanh(wd).astype(BF16), w2_ref[...]))
    softplus = jnp.maximum(nz, 0.0) + jnp.log(1.0 + jnp.exp(-jnp.abs(nz)))
    log_decay = -jnp.exp(-softplus - 0.5)
    a_sig = _sigmoid(a0_ref[...] + _dot(ad.astype(BF16), a2_ref[...]))
    g = _dot(_sigmoid(gd).astype(BF16), g2_ref[...])
    kk = k * kk_ref[...]
    ss = _head_sum(kk * kk, ones_ref[...])
    kk = kk / jnp.maximum(jnp.sqrt(ss), 1e-12)
    k_mod = k * (1.0 + (a_sig - 1.0) * ka_ref[...])
    kb = kk * a_sig
    hi = log_decay.astype(BF16)
    rem = log_decay - hi.astype(F32)
    mid = rem.astype(BF16)
    lo = (rem - mid.astype(F32)).astype(BF16)
    tri = tri_ref[...]
    cum = _dot(tri, hi) + _dot(tri, mid) + _dot(tri, lo)
    cum_end = jnp.concatenate(
        [jnp.broadcast_to(cum[(i + 1) * c - 1:(i + 1) * c, :], (c, GW)) for i in range(rows // c)], axis=0)
    e_cum = jnp.exp(cum)
    e_inv = jnp.exp(-cum)
    e_rem = jnp.exp(cum_end - cum)
    coef = _head_sum(r * k_mod * rk_ref[...], ones_ref[...])
    outs = ((at_o, -kk * jnp.exp(cum - log_decay)), (rt_o, r * e_cum), (bt_o, kb * e_inv), (kt_o, k_mod * e_inv),
            (be_o, kb * e_rem), (ke_o, k_mod * e_rem), (v_o, v), (ee_o, jnp.exp(cum_end)), (g_o, g),
            (bg_o, coef * v * g))
    for ref, val in outs:
        for pr in range(N_PAIRS):
            ref[0, pr] = val[:, pr * LANES:(pr + 1) * LANES].astype(ref.dtype)


def _rwkv_prep(proj3, mu, w0, w2, a0, a2, g2, k_k, k_a, r_k, rows=256):
    b, s, _ = proj3.shape
    rows = min(rows, s)
    pad = lambda w, n: jnp.concatenate([w, jnp.zeros((n - w.shape[0],) + w.shape[1:], w.dtype)], axis=0)
    head_of = np.arange(GW) // HEAD_DIM
    ones_bd = jnp.asarray((head_of[:, None] == head_of[None, :]).astype(np.float32), dtype=BF16)
    tok = np.arange(rows)
    tri_bd = jnp.asarray(((tok[:, None] // RWKV_CHUNK == tok[None, :] // RWKV_CHUNK)
                          & (tok[None, :] <= tok[:, None])).astype(np.float32), dtype=BF16)
    vec = lambda a: a.reshape(1, -1)
    full = lambda shape: pl.BlockSpec(shape, lambda i, j: (0,) * len(shape))
    col = COL_D // RWKV_W
    sub = rows // 8
    sd = lambda dt: jax.ShapeDtypeStruct((b, N_PAIRS, s, LANES), dt)
    return pl.pallas_call(
        functools.partial(_rwkv_prep_kernel, rows=rows),
        out_shape=[sd(BF16)] * 7 + [sd(F32)] * 3,
        grid=(b, s // rows),
        in_specs=[pl.BlockSpec((1, rows, RWKV_W), lambda i, j: (i, j, col)),
                  pl.BlockSpec((1, 8, RWKV_W), lambda i, j: (i, jnp.maximum(j * sub - 1, 0), col)),
                  full((1, RWKV_W)), full((1, GW)), full((LORA_PAD, GW)), full((1, GW)),
                  full((LORA_PAD, GW)), full((G_LORA, GW)), full((1, GW)), full((1, GW)), full((1, GW)),
                  full((GW, GW)), full((rows, rows))],
        out_specs=[pl.BlockSpec((1, N_PAIRS, rows, LANES), lambda i, j: (i, 0, j, 0))] * 10,
        compiler_params=_cparams(("arbitrary", "arbitrary")),
        name="rwkv_prep",
    )(proj3, proj3, vec(mu), vec(w0), pad(w2, LORA_PAD).astype(BF16), vec(a0), pad(a2, LORA_PAD).astype(BF16),
      g2.astype(BF16), vec(k_k), vec(k_a), vec(r_k), ones_bd, tri_bd)


def _block_diag(y, first):
    zero = jnp.zeros_like(y)
    return jnp.concatenate([jnp.where(first, y, zero), jnp.where(first, zero, y)], axis=0)


def _pair_nn(x, y, first):
    return _dot(x, _block_diag(y, first))


def _pair_nt(x, y, first):
    return _dot_nt(x, _block_diag(y, first))


def _pair_tn(x, y, first):
    full = _dot_tn(x, y)
    return jnp.where(first, full[:HEAD_DIM], full[HEAD_DIM:])


def _rwkv_chunk_terms(probs, masks):
    first, strict, incl, same_sub, eye = masks
    c = RWKV_CHUNK
    bf = lambda xs: [x.astype(BF16) for x in xs]
    nn = lambda xs, ys: [_pair_nn(x, y, first) for x, y in zip(xs, ys)]
    ident = eye.astype(F32)
    at, rt, bt, kt, b_end, k_end, v, e_end = [list(x) for x in zip(*probs)]
    ar = [jnp.concatenate([a, r], axis=0) for a, r in zip(at, rt)]
    gb = [_pair_nt(x, y, first) for x, y in zip(ar, bt)]
    gk = [_pair_nt(x, y, first) for x, y in zip(ar, kt)]
    a_ab = [jnp.where(strict, g[:c], 0.0) for g in gb]
    a_rb = bf([jnp.where(incl, g[c:], 0.0) for g in gb])
    a_kr = bf([jnp.concatenate([jnp.where(strict, g[:c], 0.0), jnp.where(incl, g[c:], 0.0)], axis=0) for g in gk])
    ad = [jnp.where(same_sub, a, 0.0) for a in a_ab]
    an = bf([a - d for a, d in zip(a_ab, ad)])
    adb = bf(ad)
    p2 = nn(adb, adb)
    av = nn(a_kr, v)
    p2b = bf(p2)
    p4 = nn(p2b, p2b)
    x1 = nn(bf([ident + d for d in ad]), bf([ident + p for p in p2]))
    p4b = bf(p4)
    p8 = nn(p4b, p4b)
    x2 = nn(bf([ident + p for p in p4]), bf([ident + p for p in p8]))
    td = bf(nn(bf(x1), bf(x2)))
    m1 = nn(td, an)
    m1b = bf(m1)
    m2 = nn(m1b, m1b)
    x3 = nn(bf([ident + m for m in m1]), bf([ident + m for m in m2]))
    t_inv = bf(nn(bf(x3), td))
    a_hat = bf(nn(t_inv, at))
    u0 = bf(nn(t_inv, bf([w[:c] for w in av])))
    r_hat = [r.astype(F32) + x for r, x in zip(rt, nn(a_rb, a_hat))]
    o0 = [x + w[c:] for x, w in zip(nn(a_rb, u0), av)]
    p_mat = [jnp.where(eye, e, 0.0) + _pair_tn(h, b, first) for e, h, b in zip(e_end, a_hat, b_end)]
    z_mat = [_pair_tn(jnp.concatenate([u, w], axis=0), jnp.concatenate([b, k], axis=0), first)
             for u, w, b, k in zip(u0, v, b_end, k_end)]
    return list(zip(p_mat, z_mat, r_hat, o0))


def _rwkv_scan_kernel(at_ref, rt_ref, bt_ref, kt_ref, be_ref, ke_ref, v_ref, ee_ref, g_ref, bg_ref, lng_ref,
                      lnb_ref, o_ref, state, *, rows):
    c = RWKV_CHUNK
    ri = lax.broadcasted_iota(jnp.int32, (c, LANES), 0)
    lane = lax.broadcasted_iota(jnp.int32, (c, LANES), 1)
    first = lane < HEAD_DIM
    ci = jnp.where(first, lane, lane - HEAD_DIM)
    masks = (first, ci < ri, ci <= ri, (ri // RWKV_SUB) == (ci // RWKV_SUB), ri == ci)

    @pl.when(pl.program_id(1) == 0)
    def _():
        state[...] = jnp.zeros_like(state)

    def head_mean(x):
        lo = jnp.sum(jnp.where(first, x, 0.0), axis=-1, keepdims=True)
        hi = jnp.sum(jnp.where(first, 0.0, x), axis=-1, keepdims=True)
        return jnp.where(first, lo, hi) * (1.0 / HEAD_DIM)

    chunks = [slice(i * c, (i + 1) * c) for i in range(rows // c)]
    seq_refs = (at_ref, rt_ref, bt_ref, kt_ref, be_ref, ke_ref, v_ref)
    probs = [tuple(ref[0, pr, rs, :] for ref in seq_refs) + (ee_ref[0, pr, rs.start:rs.start + 1, :],)
             for rs in chunks for pr in range(N_PAIRS)]
    terms = _rwkv_chunk_terms(probs, masks)
    s_cur = [state[pr] for pr in range(N_PAIRS)]
    for i, rs in enumerate(chunks):
        for pr in range(N_PAIRS):
            p_mat, z_mat, r_hat, o0 = terms[i * N_PAIRS + pr]
            s_hi, s_lo = _split2(s_cur[pr])
            p_hi, p_lo = _split2(p_mat)
            o = _pair_nt(r_hat.astype(BF16), s_hi, first) + o0
            s_cur[pr] = (_pair_nn(s_hi, p_hi, first) + _pair_nn(s_hi, p_lo, first) + _pair_nn(s_lo, p_hi, first)
                         + z_mat)
            oc = o - head_mean(o)
            y = oc * lax.rsqrt(head_mean(oc * oc) + RWKV_LN_EPS) * lng_ref[pr:pr + 1, :] + lnb_ref[pr:pr + 1, :]
            o_ref[0, rs, pr * LANES:(pr + 1) * LANES] = y * g_ref[0, pr, rs, :] + bg_ref[0, pr, rs, :]
    for pr in range(N_PAIRS):
        state[pr] = s_cur[pr]


def _rwkv_scan(prep, lnx_g, lnx_b, rows=512):
    b, _, s, _ = prep[0].shape
    rows = min(rows, s)
    seq_spec = pl.BlockSpec((1, N_PAIRS, rows, LANES), lambda i, t: (i, 0, t, 0))
    par_spec = pl.BlockSpec((N_PAIRS, LANES), lambda i, t: (0, 0))
    par = lambda a: a.reshape(N_PAIRS, LANES)
    return pl.pallas_call(
        functools.partial(_rwkv_scan_kernel, rows=rows),
        out_shape=jax.ShapeDtypeStruct((b, s, GW), F32),
        grid=(b, s // rows),
        in_specs=[seq_spec] * 10 + [par_spec] * 2,
        out_specs=pl.BlockSpec((1, rows, GW), lambda i, t: (i, t, 0)),
        scratch_shapes=[pltpu.VMEM((N_PAIRS, HEAD_DIM, LANES), F32)],
        compiler_params=_cparams(("arbitrary", "arbitrary")),
        name="rwkv_scan",
    )(*prep, par(lnx_g), par(lnx_b))


def _outproj_kernel(ya_ref, yb_ref, yc_ref, yd_ref, g_ref, w_ref, x_ref, o_ref):
    acc = x_ref[...]
    for i, y_ref in enumerate((ya_ref, yb_ref, yc_ref, yd_ref)):
        yn = _rms(y_ref[...], g_ref[i:i + 1, :]).astype(BF16)
        acc = acc + _dot(yn, w_ref[i * GW:(i + 1) * GW, :])
    o_ref[...] = acc


def _outproj(ys, g, w_all, layer, x2d, tm=512):
    t, d = x2d.shape
    tm = min(tm, t)
    y_spec = pl.BlockSpec((tm, GW), lambda i: (i, 0))
    return pl.pallas_call(
        _outproj_kernel,
        out_shape=jax.ShapeDtypeStruct((t, d), F32),
        grid=(t // tm,),
        in_specs=[y_spec] * 4 + [pl.BlockSpec((4, GW), lambda i: (0, 0)),
                                 pl.BlockSpec((None, 4 * GW, d), lambda i: (layer, 0, 0)),
                                 pl.BlockSpec((tm, d), lambda i: (i, 0))],
        out_specs=pl.BlockSpec((tm, d), lambda i: (i, 0)),
        compiler_params=_cparams(("arbitrary",)),
        name="outproj",
    )(*[y.reshape(t, GW) for y in ys], g.reshape(4, GW), w_all, x2d)


def _ffn_kernel(x_ref, g_ref, wg_ref, wu_ref, wd_ref, gf_ref, o_ref, h_scr, *, final_norm):
    j = pl.program_id(1)

    @pl.when(j == 0)
    def _():
        x = x_ref[...]
        h_scr[...] = _rms(x, g_ref[...]).astype(BF16)
        o_ref[...] = x

    h = h_scr[...]
    gate = _dot(h, wg_ref[...].astype(BF16))
    up = _dot(h, wu_ref[...].astype(BF16))
    act = (gate * _sigmoid(gate) * up).astype(BF16)
    o_ref[...] += _dot(act, wd_ref[...].astype(BF16))

    if final_norm:
        @pl.when(j == pl.num_programs(1) - 1)
        def _():
            o_ref[...] = _rms(o_ref[...], gf_ref[...])


def _ffn(x2d, g, wg, wu, wd, layer, g_final, final_norm, tm=1024, tf=256):
    t, d = x2d.shape
    f = wg.shape[2]
    tm = min(tm, t)
    return pl.pallas_call(
        functools.partial(_ffn_kernel, final_norm=final_norm),
        out_shape=jax.ShapeDtypeStruct((t, d), F32),
        grid=(t // tm, f // tf),
        in_specs=[pl.BlockSpec((tm, d), lambda i, j: (i, 0)),
                  pl.BlockSpec((1, d), lambda i, j: (0, 0)),
                  pl.BlockSpec((None, d, tf), lambda i, j: (layer, 0, j)),
                  pl.BlockSpec((None, d, tf), lambda i, j: (layer, 0, j)),
                  pl.BlockSpec((None, tf, d), lambda i, j: (layer, j, 0)),
                  pl.BlockSpec((1, d), lambda i, j: (0, 0))],
        out_specs=pl.BlockSpec((tm, d), lambda i, j: (i, 0)),
        scratch_shapes=[pltpu.VMEM((tm, d), BF16)],
        compiler_params=_cparams(("arbitrary", "arbitrary")),
        name="ffn",
    )(x2d, g.reshape(1, d), wg, wu, wd, g_final.reshape(1, d))


def _cast_kernel(x_ref, o_ref):
    o_ref[...] = x_ref[...].astype(o_ref.dtype)


def _to_bf16(w, rows=256):
    shape = w.shape
    w2 = w.reshape(-1, shape[-1])
    n, c = w2.shape
    out = pl.pallas_call(
        _cast_kernel,
        out_shape=jax.ShapeDtypeStruct((n, c), BF16),
        grid=(n // rows,),
        in_specs=[pl.BlockSpec((rows, c), lambda i: (i, 0))],
        out_specs=pl.BlockSpec((rows, c), lambda i: (i, 0)),
        compiler_params=_cparams(("arbitrary",)),
        name="cast_bf16",
    )(w2)
    return out.reshape(shape)


_RWKV_SRC = np.concatenate([[0], np.cumsum((GW, W_LORA, GW, GW, A_LORA, G_LORA))])
_RWKV_DST = (0, 3 * GW, GW, 2 * GW, 3 * GW + LORA_PAD, 3 * GW + 2 * LORA_PAD)


def _reorder_rwkv(a):
    out = jnp.zeros(a.shape[:-1] + (RWKV_W,), a.dtype)
    for i, dst in enumerate(_RWKV_DST):
        lo, hi = int(_RWKV_SRC[i]), int(_RWKV_SRC[i + 1])
        out = out.at[..., dst:dst + hi - lo].set(a[..., lo:hi])
    return out


def _prepare_w_in(w_in):
    wt = jnp.swapaxes(w_in, 1, 2)
    zeros = lambda n: jnp.zeros(wt.shape[:1] + (n, wt.shape[2]), BF16)
    piece = lambda i: wt[:, COL_D + int(_RWKV_SRC[i]):COL_D + int(_RWKV_SRC[i + 1])].astype(BF16)
    return jnp.concatenate([wt[:, :COL_D].astype(BF16), piece(0), piece(2), piece(3),
                            piece(1), zeros(LORA_PAD - W_LORA), piece(4), zeros(LORA_PAD - A_LORA), piece(5)],
                           axis=1)


def kernel(x, norm_mix_g, w_in, pos_bias, sgu_ln_g, sgu_w, sgu_b, rwkv_mu, rwkv_w0, rwkv_w2, rwkv_a0, rwkv_a2,
           rwkv_g2, rwkv_k_k, rwkv_k_a, rwkv_r_k, rwkv_lnx_g, rwkv_lnx_b, branch_norm_g, w_out, norm_ffn_g,
           w_gate, w_up, w_down, norm_final_g):
    b, s, d = x.shape
    depth = w_in.shape[0]
    assert s % (DIL_BLOCK * DIL_PATTERNS[-1][1]) == 0 and s % (2 * MOBA_BLOCK) == 0
    bias_a = _bias_tiles(pos_bias, _dil_bucket_tiles(), 0)
    bias_c = _bias_tiles(pos_bias, _moba_bucket_tiles(s // MOBA_BLOCK), N_HEADS)
    x2d = x.reshape(b * s, d)
    w_out_b = _to_bf16(w_out)
    w_in_b, mu_all = _prepare_w_in(w_in), _reorder_rwkv(rwkv_mu)
    for l in range(depth):
        mu_l = mu_all[l]
        proj3 = _inproj(x2d, norm_mix_g[l], w_in_b, l).reshape(b, s, D_PROJ)
        ya = _mixer_dilated(proj3, bias_a)
        yb = _mixer_sgu(proj3, sgu_ln_g[l], sgu_w[l], sgu_b[l])
        yc = _mixer_moba(proj3, bias_c)
        prep = _rwkv_prep(proj3, mu_l, rwkv_w0[l], rwkv_w2[l], rwkv_a0[l], rwkv_a2[l], rwkv_g2[l],
                          rwkv_k_k[l], rwkv_k_a[l], rwkv_r_k[l])
        yd = _rwkv_scan(prep, rwkv_lnx_g[l], rwkv_lnx_b[l])
        x2d = _outproj((ya, yb, yc, yd), branch_norm_g[l], w_out_b, l, x2d)
        x2d = _ffn(x2d, norm_ffn_g[l], w_gate, w_up, w_down, l, norm_final_g,
                   final_norm=(l == depth - 1))
    return x2d.reshape(b, s, d)
```

```python
import functools
import math

import jax
import jax.numpy as jnp
import numpy as np
from jax import lax
from jax.experimental import pallas as pl
from jax.experimental.pallas import tpu as pltpu

F32 = jnp.float32
BF16 = jnp.bfloat16

HEAD_DIM = 64
N_HEADS = 8
GW = N_HEADS * HEAD_DIM
LANES = 128
N_PAIRS = GW // LANES
DIL_PATTERNS = ((128, 1), (512, 4), (2048, 16))
DIL_BLOCK = 128
SGU_CHUNK = 128
SGU_LN_EPS = 1e-5
MOBA_BLOCK = 256
MOBA_TOPK = 3
W_LORA = 96
A_LORA = 96
G_LORA = 256
LORA_PAD = 128
RWKV_LN_EPS = 64e-5
RWKV_CHUNK = 64
RWKV_SUB = 16
NUM_BUCKETS = 32
MAX_DISTANCE = 2048
NORM_EPS = 1e-6
NEG_INF = -1e30
ATT_SCALE = HEAD_DIM ** -0.5
LOG2E = math.log2(math.e)

RWKV_W = 3 * GW + 2 * LORA_PAD + G_LORA
A_W = 3 * GW
SRC_B = A_W
SRC_C = SRC_B + 2 * GW
SRC_D = SRC_C + 3 * GW
COL_D = 0
COL_B = RWKV_W
COL_C = COL_B + 2 * GW
BCD_W = COL_C + 3 * GW
D_PROJ = A_W + BCD_W

VMEM_LIMIT = 56 * 1024 * 1024

HIGHEST = lax.Precision.HIGHEST


def _cparams(sem):
    return pltpu.CompilerParams(dimension_semantics=sem, vmem_limit_bytes=VMEM_LIMIT)


def _dot(a, b, precision=None):
    return lax.dot_general(a, b, (((1,), (0,)), ((), ())), precision=precision,
                           preferred_element_type=F32)


def _dot_nt(a, b, precision=None):
    return lax.dot_general(a, b, (((1,), (1,)), ((), ())), precision=precision,
                           preferred_element_type=F32)


def _dot_tn(a, b, precision=None):
    return lax.dot_general(a, b, (((0,), (0,)), ((), ())), precision=precision,
                           preferred_element_type=F32)


def _rms(x, g):
    return x * lax.rsqrt(jnp.mean(x * x, axis=-1, keepdims=True) + NORM_EPS) * g


def _t5_bucket_np(dist):
    dist = np.maximum(dist, 0)
    max_exact = NUM_BUCKETS // 2
    d = np.maximum(dist, 1).astype(np.float32)
    large = max_exact + (np.log(d / np.float32(max_exact)) / np.float32(math.log(MAX_DISTANCE / max_exact))
                         * np.float32(NUM_BUCKETS - max_exact)).astype(np.int32)
    large = np.minimum(large, NUM_BUCKETS - 1)
    return np.where(dist < max_exact, dist, large).astype(np.int32)


def _dil_bucket_tiles():
    qa = np.arange(DIL_BLOCK)[:, None]
    kj = np.arange(DIL_BLOCK)[None, :]
    tiles = []
    for _, dil in DIL_PATTERNS:
        tiles.append(_t5_bucket_np((qa - kj) * dil))
        tiles.append(_t5_bucket_np((qa + DIL_BLOCK - kj) * dil))
    return np.stack(tiles)


def _moba_bucket_tiles(nblk):
    ki = np.arange(MOBA_BLOCK)[:, None]
    qi = np.arange(MOBA_BLOCK)[None, :]
    return np.stack([_t5_bucket_np(db * MOBA_BLOCK + qi - ki) for db in range(nblk)])


def _bias_tile_kernel(tbl_ref, idx_ref, o_ref, *, head_offset, buckets):
    h = pl.program_id(0) + head_offset
    for t, present in enumerate(buckets):
        idx = idx_ref[t]
        acc = jnp.zeros(idx.shape, F32)
        for b in present:
            acc = jnp.where(idx == b, tbl_ref[b, h] * LOG2E, acc)
        o_ref[0, t] = acc


def _bias_tiles(pos_bias, idx_np, head_offset):
    nt, r, c = idx_np.shape
    buckets = tuple(tuple(int(b) for b in np.unique(idx_np[t])) for t in range(nt))
    return pl.pallas_call(
        functools.partial(_bias_tile_kernel, head_offset=head_offset, buckets=buckets),
        out_shape=jax.ShapeDtypeStruct((N_HEADS, nt, r, c), F32),
        grid=(N_HEADS,),
        in_specs=[pl.BlockSpec(memory_space=pltpu.SMEM),
                  pl.BlockSpec((nt, r, c), lambda h: (0, 0, 0))],
        out_specs=pl.BlockSpec((1, nt, r, c), lambda h: (h, 0, 0, 0)),
        compiler_params=_cparams(("arbitrary",)),
        name="bias_tiles",
    )(pos_bias, jnp.asarray(idx_np))


def _inproj_kernel(x_ref, g_ref, w_ref, oa_ref, ob_ref, h_scr):
    j = pl.program_id(1)

    @pl.when(j == 0)
    def _():
        h_scr[...] = _rms(x_ref[...], g_ref[...]).astype(BF16)
        oa_ref[...] = _dot_nt(h_scr[...], w_ref[...])

    @pl.when(j > 0)
    def _():
        ob_ref[...] = _dot_nt(h_scr[...], w_ref[...]).astype(BF16)


def _inproj(x2d, g, wt_all, layer, tm=1024):
    t, d = x2d.shape
    tn = A_W
    tm = min(tm, t)
    return pl.pallas_call(
        _inproj_kernel,
        out_shape=[jax.ShapeDtypeStruct((t, A_W), F32), jax.ShapeDtypeStruct((t, BCD_W), BF16)],
        grid=(t // tm, D_PROJ // tn),
        in_specs=[pl.BlockSpec((tm, d), lambda i, j: (i, 0)),
                  pl.BlockSpec((1, d), lambda i, j: (0, 0)),
                  pl.BlockSpec((None, tn, d), lambda i, j: (layer, j, 0))],
        out_specs=[pl.BlockSpec((tm, tn), lambda i, j: (i, 0)),
                   pl.BlockSpec((tm, tn), lambda i, j: (i, jnp.maximum(j - 1, 0)))],
        scratch_shapes=[pltpu.VMEM((tm, d), BF16)],
        compiler_params=_cparams(("arbitrary", "arbitrary")),
        name="inproj",
    )(x2d, g.reshape(1, d), wt_all)


def _dilated_kernel(q_ref, k_ref, v_ref, bias_ref, o_ref, m0, m1, l0, l1, acc, *, seq):
    c = DIL_BLOCK
    lane = lax.broadcasted_iota(jnp.int32, (c, LANES), 1)
    head0 = lane < HEAD_DIM
    row = lax.broadcasted_iota(jnp.int32, (c, c), 0)
    col = lax.broadcasted_iota(jnp.int32, (c, c), 1)
    cur_valid = col <= row
    prev_valid = col >= row

    stats = ((m0, l0), (m1, l1))
    head_lanes = (head0, ~head0)

    def logits(pairs, pi, wide, first):
        valid = jnp.concatenate([prev_valid, cur_valid], axis=1) if wide else cur_valid
        bias = [jnp.concatenate([bias_ref[h, 2 * pi + 1], bias_ref[h, 2 * pi]], axis=1) if wide
                else bias_ref[h, 2 * pi] for h in range(2)]
        q = [q_ref[0, qsl, :] * (ATT_SCALE * LOG2E) for qsl, _ in pairs]
        kb = [k_ref[0, ksl, :].astype(BF16) for _, ksl in pairs]
        idx = [(h, i) for i in range(len(pairs)) for h in range(2)]
        s = {(h, i): _dot_nt(jnp.where(head_lanes[h], q[i], 0.0).astype(BF16), kb[i]) for h, i in idx}
        s = {hi: jnp.where(valid, s[hi] + bias[hi[0]], NEG_INF) for hi in idx}
        return pairs, wide, first, idx, s

    def softmax_update(pairs, wide, first, idx, s):
        rep = (lambda x: jnp.concatenate([x, x], axis=1)) if wide else (lambda x: x)
        ones = jnp.ones(((2 if wide else 1) * c, LANES), BF16)
        vb = [jnp.concatenate([v_ref[0, ksl, :].astype(BF16), ones], axis=1) for _, ksl in pairs]
        m_new = {hi: jnp.broadcast_to(jnp.max(s[hi], axis=1, keepdims=True), (c, LANES)) for hi in idx}
        if not first:
            m_old = {(h, i): stats[h][0][pairs[i][0], :] for h, i in idx}
            m_new = {hi: jnp.maximum(m_old[hi], m_new[hi]) for hi in idx}
            alpha = {hi: jnp.exp2(m_old[hi] - m_new[hi]) for hi in idx}
        p = {hi: jnp.exp2(s[hi] - rep(m_new[hi])).astype(BF16) for hi in idx}
        o = {(h, i): _dot(p[h, i], vb[i]) for h, i in idx}
        for h, i in idx:
            m_ref, l_ref = stats[h]
            qsl = pairs[i][0]
            l_new = o[h, i][:, LANES:]
            l_ref[qsl, :] = l_new if first else alpha[h, i] * l_ref[qsl, :] + l_new
            m_ref[qsl, :] = m_new[h, i]
        for i, (qsl, _) in enumerate(pairs):
            o_new = jnp.where(head0, o[0, i][:, :LANES], o[1, i][:, :LANES])
            acc[qsl, :] = o_new if first else acc[qsl, :] * jnp.where(head0, alpha[0, i], alpha[1, i]) + o_new

    def group_size(n, cap):
        return max(g for g in range(1, cap + 1) if n % g == 0)

    order = sorted(range(len(DIL_PATTERNS)), key=lambda i: -DIL_PATTERNS[i][1])
    groups = []
    for pi in order:
        dil = DIL_PATTERNS[pi][1]
        first_pass = pi == order[0]
        nb = seq // dil // c

        def sl(r, n, blocks, dil=dil):
            start = r + n * (c * dil)
            return pl.ds(start, blocks * c) if dil == 1 else pl.ds(start, blocks * c, stride=dil)

        heads = [(sl(r, 0, 1),) * 2 for r in range(dil)]
        body = [(sl(r, n, 1), sl(r, n - 1, 2)) for r in range(dil) for n in range(1, nb)]
        for tiles, wide, cap in ((heads, False, 8), (body, True, 4)):
            g = group_size(len(tiles), cap) if tiles else 1
            groups += [(tiles[i:i + g], pi, wide, first_pass) for i in range(0, len(tiles), g)]

    pending = logits(*groups[0])
    for grp in groups[1:]:
        nxt = logits(*grp)
        softmax_update(*pending)
        pending = nxt
    softmax_update(*pending)

    o_ref[0] = (acc[...] / jnp.where(head0[:1], l0[...], l1[...])).astype(o_ref.dtype)


def _mixer_dilated(proj_a, bias_a):
    b, s, _ = proj_a.shape
    proj3 = proj_a
    blk = lambda off: pl.BlockSpec((1, s, LANES), lambda p, i, off=off: (i, 0, off + p))
    return pl.pallas_call(
        functools.partial(_dilated_kernel, seq=s),
        out_shape=jax.ShapeDtypeStruct((b, s, GW), BF16),
        grid=(N_PAIRS, b),
        in_specs=[blk(0), blk(GW // LANES), blk(2 * GW // LANES),
                  pl.BlockSpec((2, 2 * len(DIL_PATTERNS), DIL_BLOCK, DIL_BLOCK),
                               lambda p, i: (p, 0, 0, 0))],
        out_specs=pl.BlockSpec((1, s, LANES), lambda p, i: (i, 0, p)),
        scratch_shapes=[pltpu.VMEM((s, LANES), F32)] * 5,
        compiler_params=_cparams(("arbitrary", "arbitrary")),
        name="mixer_dilated",
    )(proj3, proj3, proj3, bias_a)


def _gelu_tanh(x):
    return 0.5 * x * (1.0 + jnp.tanh(math.sqrt(2.0 / math.pi) * (x + 0.044715 * (x * x * x))))


def _sgu_kernel(u_ref, v_ref, lng_ref, w_ref, bias_ref, o_ref, *, rows):
    t = SGU_CHUNK
    u = _gelu_tanh(u_ref[0].astype(F32))
    v = _gelu_tanh(v_ref[0].astype(F32))
    mu = jnp.mean(v, axis=-1, keepdims=True)
    vc = v - mu
    var = jnp.mean(vc * vc, axis=-1, keepdims=True)
    vn = (vc * lax.rsqrt(var + SGU_LN_EPS) * lng_ref[...]).astype(BF16)
    r2 = lax.broadcasted_iota(jnp.int32, (2 * t, t), 0)
    c2 = lax.broadcasted_iota(jnp.int32, (2 * t, t), 1)
    causal = c2 <= jnp.where(r2 >= t, r2 - t, r2)
    first_group = lax.broadcasted_iota(jnp.int32, (t, LANES), 1) < HEAD_DIM
    for p in range(N_PAIRS):
        wp = jnp.where(causal, w_ref[p], 0.0).astype(BF16)
        for ci in range(rows // t):
            rs = slice(ci * t, (ci + 1) * t)
            cs = slice(p * LANES, (p + 1) * LANES)
            res = _dot(wp, vn[rs, cs])
            mixed = jnp.where(first_group, res[:t], res[t:]) + bias_ref[:, cs]
            o_ref[0, rs, cs] = (u[rs, cs] * mixed).astype(o_ref.dtype)


def _mixer_sgu(proj3, ln_g, w_s, b_s, rows=512):
    b, s, _ = proj3.shape
    rows = min(rows, s)
    t = SGU_CHUNK
    bias_full = jnp.repeat(b_s.T, HEAD_DIM, axis=1)
    w_pairs = w_s.reshape(N_PAIRS, 2 * t, t)
    return pl.pallas_call(
        functools.partial(_sgu_kernel, rows=rows),
        out_shape=jax.ShapeDtypeStruct((b, s, GW), BF16),
        grid=(b, s // rows),
        in_specs=[pl.BlockSpec((1, rows, GW), lambda i, j: (i, j, COL_B // GW)),
                  pl.BlockSpec((1, rows, GW), lambda i, j: (i, j, COL_B // GW + 1)),
                  pl.BlockSpec((1, GW), lambda i, j: (0, 0)),
                  pl.BlockSpec((N_PAIRS, 2 * t, t), lambda i, j: (0, 0, 0)),
                  pl.BlockSpec((t, GW), lambda i, j: (0, 0))],
        out_specs=pl.BlockSpec((1, rows, GW), lambda i, j: (i, j, 0)),
        compiler_params=_cparams(("arbitrary", "arbitrary")),
        name="mixer_sgu",
    )(proj3, proj3, ln_g.reshape(1, GW), w_pairs, bias_full)


def _moba_kernel(q_ref, k_ref, v_ref, bias_ref, o_ref, kh_scr, vt_scr, ot_scr, *, seq):
    ones_rows = 16
    bs = MOBA_BLOCK
    nblk = seq // bs
    lane = lax.broadcasted_iota(jnp.int32, (1, LANES), 1)
    head_lanes = (lane < HEAD_DIM, lane >= HEAD_DIM)
    blk = lambda i: slice(i * bs, (i + 1) * bs)

    q_all = q_ref[0]
    kbar = jnp.concatenate(
        [jnp.mean(k_ref[0, blk(j), :].astype(F32), axis=0, keepdims=True) for j in range(nblk)], axis=0)
    for j in range(nblk):
        kj = k_ref[0, blk(j), :]
        for h in range(2):
            kh_scr[h, j] = jnp.where(head_lanes[h], kj, jnp.zeros_like(kj))
        vt = v_ref[0, blk(j), :].astype(F32).T.astype(BF16)
        for h in range(2):
            vt_scr[j, h] = jnp.concatenate(
                [vt[h * HEAD_DIM:(h + 1) * HEAD_DIM], jnp.ones((ones_rows, bs), BF16)], axis=0)

    jrow = lax.broadcasted_iota(jnp.int32, (nblk, seq), 0)
    own = lax.broadcasted_iota(jnp.int32, (nblk, seq), 1) // bs
    krow = lax.broadcasted_iota(jnp.int32, (bs, bs), 0)
    qcol = lax.broadcasted_iota(jnp.int32, (bs, bs), 1)
    causal = krow <= qcol

    kb2 = jnp.concatenate([jnp.where(head_lanes[0], kbar, 0.0), jnp.where(head_lanes[1], kbar, 0.0)], axis=0)
    kb_hi, kb_lo = _split2(kb2)
    gates = _dot_nt(kb_hi, q_all) + _dot_nt(kb_lo, q_all)
    sel = []
    for h in range(2):
        gate = jnp.where(jrow < own, gates[h * nblk:(h + 1) * nblk], NEG_INF)
        rank = jnp.zeros((nblk, seq), jnp.int32)
        for j2 in range(nblk):
            gj = gate[j2:j2 + 1, :]
            ahead = (gj > gate) | ((gj == gate) & (j2 < jrow))
            rank = rank + ahead.astype(jnp.int32)
        sel.append((rank < MOBA_TOPK) & (jrow < own))

    def logits(grp):
        probs = [(h, qb) for qb in (grp, nblk - 1 - grp) for h in range(2)]
        tiles = [(h, qb, j) for h, qb in probs for j in range(qb + 1)]
        qh = {qb: q_ref[0, blk(qb), :] for _, qb in probs}
        s = {(h, qb, j): _dot_nt(kh_scr[h, j], qh[qb]) + bias_ref[h, qb - j] for h, qb, j in tiles}
        for h, qb in probs:
            s[h, qb, qb] = jnp.where(causal, s[h, qb, qb], NEG_INF)
        return probs, tiles, s

    def softmax_pv(probs, tiles, s):
        picked = {(h, qb, j): sel[h][j:j + 1, blk(qb)] for h, qb, j in tiles if j != qb}
        cmax = {t: jnp.max(s[t], axis=0, keepdims=True) for t in tiles}
        m = {}
        for h, qb in probs:
            m[h, qb] = functools.reduce(
                jnp.maximum, [cmax[h, qb, qb]] + [jnp.where(picked[h, qb, j], cmax[h, qb, j], NEG_INF)
                                                  for j in range(qb)])
        shift = {t: m[t[0], t[1]] if t[2] == t[1] else jnp.where(picked[t], m[t[0], t[1]], -NEG_INF)
                 for t in tiles}
        p = {t: jnp.exp2(s[t] - shift[t]).astype(BF16) for t in tiles}
        for h, qb in probs:
            acc = sum(_dot(vt_scr[j, h], p[h, qb, j]) for j in range(qb + 1))
            ot_scr[qb, h * HEAD_DIM:(h + 1) * HEAD_DIM, :] = acc[:HEAD_DIM] / acc[HEAD_DIM:HEAD_DIM + 1]

    pending = logits(0)
    for grp in range(1, nblk // 2):
        nxt = logits(grp)
        softmax_pv(*pending)
        pending = nxt
    softmax_pv(*pending)

    for qb in range(nblk):
        o_ref[0, blk(qb), :] = ot_scr[qb].T.astype(o_ref.dtype)


def _mixer_moba(proj3, bias_c):
    b, s, _ = proj3.shape
    nblk = s // MOBA_BLOCK
    blk = lambda off: pl.BlockSpec((1, s, LANES), lambda p, i, off=off: (i, 0, off + p))
    return pl.pallas_call(
        functools.partial(_moba_kernel, seq=s),
        out_shape=jax.ShapeDtypeStruct((b, s, GW), BF16),
        grid=(N_PAIRS, b),
        in_specs=[blk(COL_C // LANES), blk((COL_C + GW) // LANES), blk((COL_C + 2 * GW) // LANES),
                  pl.BlockSpec((2, nblk, MOBA_BLOCK, MOBA_BLOCK), lambda p, i: (p, 0, 0, 0))],
        out_specs=pl.BlockSpec((1, s, LANES), lambda p, i: (i, 0, p)),
        scratch_shapes=[pltpu.VMEM((2, nblk, MOBA_BLOCK, LANES), BF16),
                        pltpu.VMEM((nblk, 2, HEAD_DIM + 16, MOBA_BLOCK), BF16),
                        pltpu.VMEM((nblk, LANES, MOBA_BLOCK), F32)],
        compiler_params=_cparams(("arbitrary", "arbitrary")),
        name="mixer_moba",
    )(proj3, proj3, proj3, bias_c)


def _sigmoid(x):
    return 1.0 / (1.0 + jnp.exp(-x))


def _split2(x):
    hi = x.astype(BF16)
    return hi, (x - hi.astype(F32)).astype(BF16)


def _head_sum(x, ones_bd):
    hi, lo = _split2(x)
    return _dot(hi, ones_bd) + _dot(lo, ones_bd)


def _rwkv_prep_kernel(p_ref, prev_ref, mu_ref, w0_ref, w2_ref, a0_ref, a2_ref, g2_ref, kk_ref, ka_ref, rk_ref,
                      ones_ref, tri_ref, at_o, rt_o, bt_o, kt_o, be_o, ke_o, v_o, ee_o, g_o, bg_o, *, rows):
    c = RWKV_CHUNK
    p = p_ref[0].astype(F32)
    prev_row = jnp.where(pl.program_id(1) == 0, 0.0, prev_ref[0, 15:16, :].astype(F32))
    first_row = lax.broadcasted_iota(jnp.int32, (rows, 1), 0) == 0
    y_prev = jnp.where(first_row, prev_row, pltpu.roll(p, 1, axis=0))
    xs = p + (y_prev - p) * mu_ref[...]
    r = xs[:, 0:GW]
    k = xs[:, GW:2 * GW]
    v = xs[:, 2 * GW:3 * GW]
    wd = xs[:, 3 * GW:3 * GW + LORA_PAD]
    ad = xs[:, 3 * GW + LORA_PAD:3 * GW + 2 * LORA_PAD]
    gd = xs[:, 3 * GW + 2 * LORA_PAD:]
    nz = -(w0_ref[...] + _dot(jnp.tanh(wd).astype(BF16), w2_ref[...]))
    softplus = jnp.maximum(nz, 0.0) + jnp.log(1.0 + jnp.exp(-jnp.abs(nz)))
    log_decay = -jnp.exp(-softplus - 0.5)
    a_sig = _sigmoid(a0_ref[...] + _dot(ad.astype(BF16), a2_ref[...]))
    g = _dot(_sigmoid(gd).astype(BF16), g2_ref[...])
    kk = k * kk_ref[...]
    ss = _head_sum(kk * kk, ones_ref[...])
    kk = kk / jnp.maximum(jnp.sqrt(ss), 1e-12)
    k_mod = k * (1.0 + (a_sig - 1.0) * ka_ref[...])
    kb = kk * a_sig
    hi = log_decay.astype(BF16)
    rem = log_decay - hi.astype(F32)
    mid = rem.astype(BF16)
    lo = (rem - mid.astype(F32)).astype(BF16)
    tri = tri_ref[...]
    cum = _dot(tri, hi) + _dot(tri, mid) + _dot(tri, lo)
    cum_end = jnp.concatenate(
        [jnp.broadcast_to(cum[(i + 1) * c - 1:(i + 1) * c, :], (c, GW)) for i in range(rows // c)], axis=0)
    e_cum = jnp.exp(cum)
    e_inv = jnp.exp(-cum)
    e_rem = jnp.exp(cum_end - cum)
    coef = _head_sum(r * k_mod * rk_ref[...], ones_ref[...])
    outs = ((at_o, -kk * jnp.exp(cum - log_decay)), (rt_o, r * e_cum), (bt_o, kb * e_inv), (kt_o, k_mod * e_inv),
            (be_o, kb * e_rem), (ke_o, k_mod * e_rem), (v_o, v), (ee_o, jnp.exp(cum_end)), (g_o, g),
            (bg_o, coef * v * g))
    for ref, val in outs:
        for pr in range(N_PAIRS):
            ref[0, pr] = val[:, pr * LANES:(pr + 1) * LANES].astype(ref.dtype)


def _rwkv_prep(proj3, mu, w0, w2, a0, a2, g2, k_k, k_a, r_k, rows=256):
    b, s, _ = proj3.shape
    rows = min(rows, s)
    pad = lambda w, n: jnp.concatenate([w, jnp.zeros((n - w.shape[0],) + w.shape[1:], w.dtype)], axis=0)
    head_of = np.arange(GW) // HEAD_DIM
    ones_bd = jnp.asarray((head_of[:, None] == head_of[None, :]).astype(np.float32), dtype=BF16)
    tok = np.arange(rows)
    tri_bd = jnp.asarray(((tok[:, None] // RWKV_CHUNK == tok[None, :] // RWKV_CHUNK)
                          & (tok[None, :] <= tok[:, None])).astype(np.float32), dtype=BF16)
    vec = lambda a: a.reshape(1, -1)
    full = lambda shape: pl.BlockSpec(shape, lambda i, j: (0,) * len(shape))
    col = COL_D // RWKV_W
    sub = rows // 16
    sd = lambda dt: jax.ShapeDtypeStruct((b, N_PAIRS, s, LANES), dt)
    return pl.pallas_call(
        functools.partial(_rwkv_prep_kernel, rows=rows),
        out_shape=[sd(BF16)] * 7 + [sd(F32)] * 3,
        grid=(b, s // rows),
        in_specs=[pl.BlockSpec((1, rows, RWKV_W), lambda i, j: (i, j, col)),
                  pl.BlockSpec((1, 16, RWKV_W), lambda i, j: (i, jnp.maximum(j * sub - 1, 0), col)),
                  full((1, RWKV_W)), full((1, GW)), full((LORA_PAD, GW)), full((1, GW)),
                  full((LORA_PAD, GW)), full((G_LORA, GW)), full((1, GW)), full((1, GW)), full((1, GW)),
                  full((GW, GW)), full((rows, rows))],
        out_specs=[pl.BlockSpec((1, N_PAIRS, rows, LANES), lambda i, j: (i, 0, j, 0))] * 10,
        compiler_params=_cparams(("arbitrary", "arbitrary")),
        name="rwkv_prep",
    )(proj3, proj3, vec(mu), vec(w0), pad(w2, LORA_PAD).astype(BF16), vec(a0), pad(a2, LORA_PAD).astype(BF16),
      g2.astype(BF16), vec(k_k), vec(k_a), vec(r_k), ones_bd, tri_bd)


def _block_diag(y, first):
    zero = jnp.zeros_like(y)
    return jnp.concatenate([jnp.where(first, y, zero), jnp.where(first, zero, y)], axis=0)


def _pair_nn(x, y, first):
    return _dot(x, _block_diag(y, first))


def _pair_nt(x, y, first):
    return _dot_nt(x, _block_diag(y, first))


def _pair_tn(x, y, first):
    full = _dot_tn(x, y)
    return jnp.where(first, full[:HEAD_DIM], full[HEAD_DIM:])


def _rwkv_chunk_terms(probs, masks):
    first, strict, incl, same_sub, eye = masks
    c = RWKV_CHUNK
    bf = lambda xs: [x.astype(BF16) for x in xs]
    nn = lambda xs, ys: [_pair_nn(x, y, first) for x, y in zip(xs, ys)]
    ident = eye.astype(F32)
    at, rt, bt, kt, b_end, k_end, v, e_end = [list(x) for x in zip(*probs)]
    ar = [jnp.concatenate([a, r], axis=0) for a, r in zip(at, rt)]
    gb = [_pair_nt(x, y, first) for x, y in zip(ar, bt)]
    gk = [_pair_nt(x, y, first) for x, y in zip(ar, kt)]
    a_ab = [jnp.where(strict, g[:c], 0.0) for g in gb]
    a_rb = bf([jnp.where(incl, g[c:], 0.0) for g in gb])
    a_kr = bf([jnp.concatenate([jnp.where(strict, g[:c], 0.0), jnp.where(incl, g[c:], 0.0)], axis=0) for g in gk])
    ad = [jnp.where(same_sub, a, 0.0) for a in a_ab]
    an = bf([a - d for a, d in zip(a_ab, ad)])
    adb = bf(ad)
    p2 = nn(adb, adb)
    av = nn(a_kr, v)
    p2b = bf(p2)
    p4 = nn(p2b, p2b)
    x1 = nn(bf([ident + d for d in ad]), bf([ident + p for p in p2]))
    p4b = bf(p4)
    p8 = nn(p4b, p4b)
    x2 = nn(bf([ident + p for p in p4]), bf([ident + p for p in p8]))
    td = bf(nn(bf(x1), bf(x2)))
    m1 = nn(td, an)
    m1b = bf(m1)
    m2 = nn(m1b, m1b)
    x3 = nn(bf([ident + m for m in m1]), bf([ident + m for m in m2]))
    t_inv = bf(nn(bf(x3), td))
    a_hat = bf(nn(t_inv, at))
    u0 = bf(nn(t_inv, bf([w[:c] for w in av])))
    r_hat = [r.astype(F32) + x for r, x in zip(rt, nn(a_rb, a_hat))]
    o0 = [x + w[c:] for x, w in zip(nn(a_rb, u0), av)]
    p_mat = [jnp.where(eye, e, 0.0) + _pair_tn(h, b, first) for e, h, b in zip(e_end, a_hat, b_end)]
    z_mat = [_pair_tn(jnp.concatenate([u, w], axis=0), jnp.concatenate([b, k], axis=0), first)
             for u, w, b, k in zip(u0, v, b_end, k_end)]
    return list(zip(p_mat, z_mat, r_hat, o0))


def _rwkv_scan_kernel(at_ref, rt_ref, bt_ref, kt_ref, be_ref, ke_ref, v_ref, ee_ref, g_ref, bg_ref, lng_ref,
                      lnb_ref, o_ref, state, *, rows):
    c = RWKV_CHUNK
    ri = lax.broadcasted_iota(jnp.int32, (c, LANES), 0)
    lane = lax.broadcasted_iota(jnp.int32, (c, LANES), 1)
    first = lane < HEAD_DIM
    ci = jnp.where(first, lane, lane - HEAD_DIM)
    masks = (first, ci < ri, ci <= ri, (ri // RWKV_SUB) == (ci // RWKV_SUB), ri == ci)

    @pl.when(pl.program_id(1) == 0)
    def _():
        state[...] = jnp.zeros_like(state)

    def head_mean(x):
        lo = jnp.sum(jnp.where(first, x, 0.0), axis=-1, keepdims=True)
        hi = jnp.sum(jnp.where(first, 0.0, x), axis=-1, keepdims=True)
        return jnp.where(first, lo, hi) * (1.0 / HEAD_DIM)

    chunks = [slice(i * c, (i + 1) * c) for i in range(rows // c)]
    seq_refs = (at_ref, rt_ref, bt_ref, kt_ref, be_ref, ke_ref, v_ref)
    probs = [tuple(ref[0, pr, rs, :] for ref in seq_refs) + (ee_ref[0, pr, rs.start:rs.start + 1, :],)
             for rs in chunks for pr in range(N_PAIRS)]
    terms = _rwkv_chunk_terms(probs, masks)
    s_cur = [state[pr] for pr in range(N_PAIRS)]
    for i, rs in enumerate(chunks):
        for pr in range(N_PAIRS):
            p_mat, z_mat, r_hat, o0 = terms[i * N_PAIRS + pr]
            s_hi, s_lo = _split2(s_cur[pr])
            p_hi, p_lo = _split2(p_mat)
            o = _pair_nt(r_hat.astype(BF16), s_hi, first) + o0
            s_cur[pr] = (_pair_nn(s_hi, p_hi, first) + _pair_nn(s_hi, p_lo, first) + _pair_nn(s_lo, p_hi, first)
                         + z_mat)
            oc = o - head_mean(o)
            y = oc * lax.rsqrt(head_mean(oc * oc) + RWKV_LN_EPS) * lng_ref[pr:pr + 1, :] + lnb_ref[pr:pr + 1, :]
            o_ref[0, rs, pr * LANES:(pr + 1) * LANES] = (y * g_ref[0, pr, rs, :]
                                                         + bg_ref[0, pr, rs, :]).astype(o_ref.dtype)
    for pr in range(N_PAIRS):
        state[pr] = s_cur[pr]


def _rwkv_scan(prep, lnx_g, lnx_b, rows=512):
    b, _, s, _ = prep[0].shape
    rows = min(rows, s)
    seq_spec = pl.BlockSpec((1, N_PAIRS, rows, LANES), lambda i, t: (i, 0, t, 0))
    par_spec = pl.BlockSpec((N_PAIRS, LANES), lambda i, t: (0, 0))
    par = lambda a: a.reshape(N_PAIRS, LANES)
    return pl.pallas_call(
        functools.partial(_rwkv_scan_kernel, rows=rows),
        out_shape=jax.ShapeDtypeStruct((b, s, GW), BF16),
        grid=(b, s // rows),
        in_specs=[seq_spec] * 10 + [par_spec] * 2,
        out_specs=pl.BlockSpec((1, rows, GW), lambda i, t: (i, t, 0)),
        scratch_shapes=[pltpu.VMEM((N_PAIRS, HEAD_DIM, LANES), F32)],
        compiler_params=_cparams(("arbitrary", "arbitrary")),
        name="rwkv_scan",
    )(*prep, par(lnx_g), par(lnx_b))


def _outproj_kernel(ya_ref, yb_ref, yc_ref, yd_ref, g_ref, w_ref, x_ref, o_ref):
    acc = x_ref[...]
    for i, y_ref in enumerate((ya_ref, yb_ref, yc_ref, yd_ref)):
        yn = _rms(y_ref[...].astype(F32), g_ref[i:i + 1, :]).astype(BF16)
        acc = acc + _dot(yn, w_ref[i * GW:(i + 1) * GW, :])
    o_ref[...] = acc


def _outproj(ys, g, w_all, layer, x2d, tm=512):
    t, d = x2d.shape
    tm = min(tm, t)
    y_spec = pl.BlockSpec((tm, GW), lambda i: (i, 0))
    return pl.pallas_call(
        _outproj_kernel,
        out_shape=jax.ShapeDtypeStruct((t, d), F32),
        grid=(t // tm,),
        in_specs=[y_spec] * 4 + [pl.BlockSpec((4, GW), lambda i: (0, 0)),
                                 pl.BlockSpec((None, 4 * GW, d), lambda i: (layer, 0, 0)),
                                 pl.BlockSpec((tm, d), lambda i: (i, 0))],
        out_specs=pl.BlockSpec((tm, d), lambda i: (i, 0)),
        compiler_params=_cparams(("arbitrary",)),
        name="outproj",
    )(*[y.reshape(t, GW) for y in ys], g.reshape(4, GW), w_all, x2d)


def _ffn_kernel(x_ref, g_ref, wg_ref, wu_ref, wd_ref, gf_ref, o_ref, h_scr, *, final_norm):
    j = pl.program_id(1)

    @pl.when(j == 0)
    def _():
        x = x_ref[...]
        h_scr[...] = _rms(x, g_ref[...]).astype(BF16)
        o_ref[...] = x

    h = h_scr[...]
    gate = _dot(h, wg_ref[...].astype(BF16))
    up = _dot(h, wu_ref[...].astype(BF16))
    act = (gate * _sigmoid(gate) * up).astype(BF16)
    o_ref[...] += _dot(act, wd_ref[...].astype(BF16))

    if final_norm:
        @pl.when(j == pl.num_programs(1) - 1)
        def _():
            o_ref[...] = _rms(o_ref[...], gf_ref[...])


def _ffn(x2d, g, wg, wu, wd, layer, g_final, final_norm, tm=1024, tf=256):
    t, d = x2d.shape
    f = wg.shape[2]
    tm = min(tm, t)
    return pl.pallas_call(
        functools.partial(_ffn_kernel, final_norm=final_norm),
        out_shape=jax.ShapeDtypeStruct((t, d), F32),
        grid=(t // tm, f // tf),
        in_specs=[pl.BlockSpec((tm, d), lambda i, j: (i, 0)),
                  pl.BlockSpec((1, d), lambda i, j: (0, 0)),
                  pl.BlockSpec((None, d, tf), lambda i, j: (layer, 0, j)),
                  pl.BlockSpec((None, d, tf), lambda i, j: (layer, 0, j)),
                  pl.BlockSpec((None, tf, d), lambda i, j: (layer, j, 0)),
                  pl.BlockSpec((1, d), lambda i, j: (0, 0))],
        out_specs=pl.BlockSpec((tm, d), lambda i, j: (i, 0)),
        scratch_shapes=[pltpu.VMEM((tm, d), BF16)],
        compiler_params=_cparams(("arbitrary", "arbitrary")),
        name="ffn",
    )(x2d, g.reshape(1, d), wg, wu, wd, g_final.reshape(1, d))


def _cast_kernel(x_ref, o_ref):
    o_ref[...] = x_ref[...].astype(o_ref.dtype)


def _to_bf16(w, rows=256):
    shape = w.shape
    w2 = w.reshape(-1, shape[-1])
    n, c = w2.shape
    out = pl.pallas_call(
        _cast_kernel,
        out_shape=jax.ShapeDtypeStruct((n, c), BF16),
        grid=(n // rows,),
        in_specs=[pl.BlockSpec((rows, c), lambda i: (i, 0))],
        out_specs=pl.BlockSpec((rows, c), lambda i: (i, 0)),
        compiler_params=_cparams(("arbitrary",)),
        name="cast_bf16",
    )(w2)
    return out.reshape(shape)


_RWKV_SRC = np.concatenate([[0], np.cumsum((GW, W_LORA, GW, GW, A_LORA, G_LORA))])
_RWKV_DST = (0, 3 * GW, GW, 2 * GW, 3 * GW + LORA_PAD, 3 * GW + 2 * LORA_PAD)


def _reorder_rwkv(a):
    out = jnp.zeros(a.shape[:-1] + (RWKV_W,), a.dtype)
    for i, dst in enumerate(_RWKV_DST):
        lo, hi = int(_RWKV_SRC[i]), int(_RWKV_SRC[i + 1])
        out = out.at[..., dst:dst + hi - lo].set(a[..., lo:hi])
    return out


def _prepare_w_in(w_in):
    wt = jnp.swapaxes(w_in, 1, 2)
    zeros = lambda n: jnp.zeros(wt.shape[:1] + (n, wt.shape[2]), BF16)
    rows = lambda lo, hi: wt[:, lo:hi].astype(BF16)
    piece = lambda i: rows(SRC_D + int(_RWKV_SRC[i]), SRC_D + int(_RWKV_SRC[i + 1]))
    c_q = (wt[:, SRC_C:SRC_C + GW] * (ATT_SCALE * LOG2E)).astype(BF16)
    return jnp.concatenate([rows(0, A_W),
                            piece(0), piece(2), piece(3), piece(1), zeros(LORA_PAD - W_LORA),
                            piece(4), zeros(LORA_PAD - A_LORA), piece(5),
                            rows(SRC_B, SRC_C), c_q, rows(SRC_C + GW, SRC_D)], axis=1)


def kernel(x, norm_mix_g, w_in, pos_bias, sgu_ln_g, sgu_w, sgu_b, rwkv_mu, rwkv_w0, rwkv_w2, rwkv_a0, rwkv_a2,
           rwkv_g2, rwkv_k_k, rwkv_k_a, rwkv_r_k, rwkv_lnx_g, rwkv_lnx_b, branch_norm_g, w_out, norm_ffn_g,
           w_gate, w_up, w_down, norm_final_g):
    b, s, d = x.shape
    depth = w_in.shape[0]
    assert s % (DIL_BLOCK * DIL_PATTERNS[-1][1]) == 0 and s % (2 * MOBA_BLOCK) == 0
    bias_a = _bias_tiles(pos_bias, _dil_bucket_tiles(), 0)
    bias_c = _bias_tiles(pos_bias, _moba_bucket_tiles(s // MOBA_BLOCK), N_HEADS)
    x2d = x.reshape(b * s, d)
    w_out_b = _to_bf16(w_out)
    w_in_b, mu_all = _prepare_w_in(w_in), _reorder_rwkv(rwkv_mu)
    for l in range(depth):
        mu_l = mu_all[l]
        proj_a, proj_bcd = _inproj(x2d, norm_mix_g[l], w_in_b, l)
        proj3 = proj_bcd.reshape(b, s, BCD_W)
        ya = _mixer_dilated(proj_a.reshape(b, s, A_W), bias_a)
        yb = _mixer_sgu(proj3, sgu_ln_g[l], sgu_w[l], sgu_b[l])
        yc = _mixer_moba(proj3, bias_c)
        prep = _rwkv_prep(proj3, mu_l, rwkv_w0[l], rwkv_w2[l], rwkv_a0[l], rwkv_a2[l], rwkv_g2[l],
                          rwkv_k_k[l], rwkv_k_a[l], rwkv_r_k[l])
        yd = _rwkv_scan(prep, rwkv_lnx_g[l], rwkv_lnx_b[l])
        x2d = _outproj((ya, yb, yc, yd), branch_norm_g[l], w_out_b, l, x2d)
        x2d = _ffn(x2d, norm_ffn_g[l], w_gate, w_up, w_down, l, norm_final_g,
                   final_norm=(l == depth - 1))
    return x2d.reshape(b, s, d)
```

```python
import functools
import math

import jax
import jax.numpy as jnp
import numpy as np
from jax import lax
from jax.experimental import pallas as pl
from jax.experimental.pallas import tpu as pltpu

F32 = jnp.float32
BF16 = jnp.bfloat16

HEAD_DIM = 64
N_HEADS = 8
GW = N_HEADS * HEAD_DIM
LANES = 128
N_PAIRS = GW // LANES
DIL_PATTERNS = ((128, 1), (512, 4), (2048, 16))
DIL_BLOCK = 128
SGU_CHUNK = 128
SGU_LN_EPS = 1e-5
MOBA_BLOCK = 256
MOBA_TOPK = 3
W_LORA = 96
A_LORA = 96
G_LORA = 256
LORA_PAD = 128
RWKV_LN_EPS = 64e-5
RWKV_CHUNK = 64
RWKV_SUB = 16
NUM_BUCKETS = 32
MAX_DISTANCE = 2048
NORM_EPS = 1e-6
NEG_INF = -1e30
ATT_SCALE = HEAD_DIM ** -0.5
LOG2E = math.log2(math.e)

RWKV_W = 3 * GW + 2 * LORA_PAD + G_LORA
A_W = 3 * GW
SRC_B = A_W
SRC_C = SRC_B + 2 * GW
SRC_D = SRC_C + 3 * GW
COL_D = 0
COL_B = RWKV_W
COL_C = COL_B + 2 * GW
BCD_W = COL_C + 3 * GW
D_PROJ = A_W + BCD_W

VMEM_LIMIT = 56 * 1024 * 1024

HIGHEST = lax.Precision.HIGHEST


def _cparams(sem):
    return pltpu.CompilerParams(dimension_semantics=sem, vmem_limit_bytes=VMEM_LIMIT)


def _dot(a, b, precision=None):
    return lax.dot_general(a, b, (((1,), (0,)), ((), ())), precision=precision,
                           preferred_element_type=F32)


def _dot_nt(a, b, precision=None):
    return lax.dot_general(a, b, (((1,), (1,)), ((), ())), precision=precision,
                           preferred_element_type=F32)


def _dot_tn(a, b, precision=None):
    return lax.dot_general(a, b, (((0,), (0,)), ((), ())), precision=precision,
                           preferred_element_type=F32)


def _rms(x, g):
    return x * lax.rsqrt(jnp.mean(x * x, axis=-1, keepdims=True) + NORM_EPS) * g


def _t5_bucket_np(dist):
    dist = np.maximum(dist, 0)
    max_exact = NUM_BUCKETS // 2
    d = np.maximum(dist, 1).astype(np.float32)
    large = max_exact + (np.log(d / np.float32(max_exact)) / np.float32(math.log(MAX_DISTANCE / max_exact))
                         * np.float32(NUM_BUCKETS - max_exact)).astype(np.int32)
    large = np.minimum(large, NUM_BUCKETS - 1)
    return np.where(dist < max_exact, dist, large).astype(np.int32)


def _dil_bucket_tiles():
    qa = np.arange(DIL_BLOCK)[:, None]
    kj = np.arange(DIL_BLOCK)[None, :]
    tiles = []
    for _, dil in DIL_PATTERNS:
        tiles.append(_t5_bucket_np((qa - kj) * dil))
        tiles.append(_t5_bucket_np((qa + DIL_BLOCK - kj) * dil))
    return np.stack(tiles)


def _moba_bucket_tiles(nblk):
    ki = np.arange(MOBA_BLOCK)[:, None]
    qi = np.arange(MOBA_BLOCK)[None, :]
    return np.stack([_t5_bucket_np(db * MOBA_BLOCK + qi - ki) for db in range(nblk)])


def _bias_tile_kernel(tbl_ref, idx_ref, o_ref, *, head_offset, buckets):
    h = pl.program_id(0) + head_offset
    for t, present in enumerate(buckets):
        idx = idx_ref[t]
        acc = jnp.zeros(idx.shape, F32)
        for b in present:
            acc = jnp.where(idx == b, tbl_ref[b, h] * LOG2E, acc)
        o_ref[0, t] = acc


def _bias_tiles(pos_bias, idx_np, head_offset):
    nt, r, c = idx_np.shape
    buckets = tuple(tuple(int(b) for b in np.unique(idx_np[t])) for t in range(nt))
    return pl.pallas_call(
        functools.partial(_bias_tile_kernel, head_offset=head_offset, buckets=buckets),
        out_shape=jax.ShapeDtypeStruct((N_HEADS, nt, r, c), F32),
        grid=(N_HEADS,),
        in_specs=[pl.BlockSpec(memory_space=pltpu.SMEM),
                  pl.BlockSpec((nt, r, c), lambda h: (0, 0, 0))],
        out_specs=pl.BlockSpec((1, nt, r, c), lambda h: (h, 0, 0, 0)),
        compiler_params=_cparams(("arbitrary",)),
        name="bias_tiles",
    )(pos_bias, jnp.asarray(idx_np))


def _inproj_kernel(x_ref, g_ref, w_ref, oa_ref, ob_ref, h_scr):
    j = pl.program_id(1)

    @pl.when(j == 0)
    def _():
        h_scr[...] = _rms(x_ref[...], g_ref[...]).astype(BF16)
        oa_ref[...] = _dot_nt(h_scr[...], w_ref[...])

    @pl.when(j > 0)
    def _():
        ob_ref[...] = _dot_nt(h_scr[...], w_ref[...]).astype(BF16)


def _inproj(x2d, g, wt_all, layer, tm=1024):
    t, d = x2d.shape
    tn = A_W
    tm = min(tm, t)
    return pl.pallas_call(
        _inproj_kernel,
        out_shape=[jax.ShapeDtypeStruct((t, A_W), F32), jax.ShapeDtypeStruct((t, BCD_W), BF16)],
        grid=(t // tm, D_PROJ // tn),
        in_specs=[pl.BlockSpec((tm, d), lambda i, j: (i, 0)),
                  pl.BlockSpec((1, d), lambda i, j: (0, 0)),
                  pl.BlockSpec((None, tn, d), lambda i, j: (layer, j, 0))],
        out_specs=[pl.BlockSpec((tm, tn), lambda i, j: (i, 0)),
                   pl.BlockSpec((tm, tn), lambda i, j: (i, jnp.maximum(j - 1, 0)))],
        scratch_shapes=[pltpu.VMEM((tm, d), BF16)],
        compiler_params=_cparams(("arbitrary", "arbitrary")),
        name="inproj",
    )(x2d, g.reshape(1, d), wt_all)


def _dilated_kernel(q_ref, k_ref, v_ref, bias_ref, o_ref, m0, m1, l0, l1, acc, *, seq):
    c = DIL_BLOCK
    lane = lax.broadcasted_iota(jnp.int32, (c, LANES), 1)
    head0 = lane < HEAD_DIM
    row = lax.broadcasted_iota(jnp.int32, (c, c), 0)
    col = lax.broadcasted_iota(jnp.int32, (c, c), 1)
    cur_valid = col <= row
    prev_valid = col >= row

    stats = ((m0, l0), (m1, l1))
    head_lanes = (head0, ~head0)

    def logits(pairs, pi, wide, first):
        valid = jnp.concatenate([prev_valid, cur_valid], axis=1) if wide else cur_valid
        bias = [jnp.concatenate([bias_ref[h, 2 * pi + 1], bias_ref[h, 2 * pi]], axis=1) if wide
                else bias_ref[h, 2 * pi] for h in range(2)]
        q = [q_ref[0, qsl, :] * (ATT_SCALE * LOG2E) for qsl, _ in pairs]
        kb = [k_ref[0, ksl, :].astype(BF16) for _, ksl in pairs]
        idx = [(h, i) for i in range(len(pairs)) for h in range(2)]
        s = {(h, i): _dot_nt(jnp.where(head_lanes[h], q[i], 0.0).astype(BF16), kb[i]) for h, i in idx}
        s = {hi: jnp.where(valid, s[hi] + bias[hi[0]], NEG_INF) for hi in idx}
        return pairs, wide, first, idx, s

    def softmax_update(pairs, wide, first, idx, s):
        rep = (lambda x: jnp.concatenate([x, x], axis=1)) if wide else (lambda x: x)
        ones = jnp.ones(((2 if wide else 1) * c, LANES), BF16)
        vb = [jnp.concatenate([v_ref[0, ksl, :].astype(BF16), ones], axis=1) for _, ksl in pairs]
        m_new = {hi: jnp.broadcast_to(jnp.max(s[hi], axis=1, keepdims=True), (c, LANES)) for hi in idx}
        if not first:
            m_old = {(h, i): stats[h][0][pairs[i][0], :] for h, i in idx}
            m_new = {hi: jnp.maximum(m_old[hi], m_new[hi]) for hi in idx}
            alpha = {hi: jnp.exp2(m_old[hi] - m_new[hi]) for hi in idx}
        p = {hi: jnp.exp2(s[hi] - rep(m_new[hi])).astype(BF16) for hi in idx}
        o = {(h, i): _dot(p[h, i], vb[i]) for h, i in idx}
        for h, i in idx:
            m_ref, l_ref = stats[h]
            qsl = pairs[i][0]
            l_new = o[h, i][:, LANES:]
            l_ref[qsl, :] = l_new if first else alpha[h, i] * l_ref[qsl, :] + l_new
            m_ref[qsl, :] = m_new[h, i]
        for i, (qsl, _) in enumerate(pairs):
            o_new = jnp.where(head0, o[0, i][:, :LANES], o[1, i][:, :LANES])
            acc[qsl, :] = o_new if first else acc[qsl, :] * jnp.where(head0, alpha[0, i], alpha[1, i]) + o_new

    def group_size(n, cap):
        return max(g for g in range(1, cap + 1) if n % g == 0)

    order = sorted(range(len(DIL_PATTERNS)), key=lambda i: -DIL_PATTERNS[i][1])
    groups = []
    for pi in order:
        dil = DIL_PATTERNS[pi][1]
        first_pass = pi == order[0]
        nb = seq // dil // c

        def sl(r, n, blocks, dil=dil):
            start = r + n * (c * dil)
            return pl.ds(start, blocks * c) if dil == 1 else pl.ds(start, blocks * c, stride=dil)

        heads = [(sl(r, 0, 1),) * 2 for r in range(dil)]
        body = [(sl(r, n, 1), sl(r, n - 1, 2)) for r in range(dil) for n in range(1, nb)]
        for tiles, wide, cap in ((heads, False, 8), (body, True, 4)):
            g = group_size(len(tiles), cap) if tiles else 1
            groups += [(tiles[i:i + g], pi, wide, first_pass) for i in range(0, len(tiles), g)]

    pending = logits(*groups[0])
    for grp in groups[1:]:
        nxt = logits(*grp)
        softmax_update(*pending)
        pending = nxt
    softmax_update(*pending)

    o_ref[0] = (acc[...] / jnp.where(head0[:1], l0[...], l1[...])).astype(o_ref.dtype)


def _mixer_dilated(proj_a, bias_a):
    b, s, _ = proj_a.shape
    proj3 = proj_a
    blk = lambda off: pl.BlockSpec((1, s, LANES), lambda p, i, off=off: (i, 0, off + p))
    return pl.pallas_call(
        functools.partial(_dilated_kernel, seq=s),
        out_shape=jax.ShapeDtypeStruct((b, s, GW), BF16),
        grid=(N_PAIRS, b),
        in_specs=[blk(0), blk(GW // LANES), blk(2 * GW // LANES),
                  pl.BlockSpec((2, 2 * len(DIL_PATTERNS), DIL_BLOCK, DIL_BLOCK),
                               lambda p, i: (p, 0, 0, 0))],
        out_specs=pl.BlockSpec((1, s, LANES), lambda p, i: (i, 0, p)),
        scratch_shapes=[pltpu.VMEM((s, LANES), F32)] * 5,
        compiler_params=_cparams(("arbitrary", "arbitrary")),
        name="mixer_dilated",
    )(proj3, proj3, proj3, bias_a)


def _gelu_tanh(x):
    return 0.5 * x * (1.0 + jnp.tanh(math.sqrt(2.0 / math.pi) * (x + 0.044715 * (x * x * x))))


def _sgu_kernel(u_ref, v_ref, lng_ref, w_ref, bias_ref, o_ref, *, rows):
    t = SGU_CHUNK
    u = _gelu_tanh(u_ref[0].astype(F32))
    v = _gelu_tanh(v_ref[0].astype(F32))
    mu = jnp.mean(v, axis=-1, keepdims=True)
    vc = v - mu
    var = jnp.mean(vc * vc, axis=-1, keepdims=True)
    vn = (vc * lax.rsqrt(var + SGU_LN_EPS) * lng_ref[...]).astype(BF16)
    r2 = lax.broadcasted_iota(jnp.int32, (2 * t, t), 0)
    c2 = lax.broadcasted_iota(jnp.int32, (2 * t, t), 1)
    causal = c2 <= jnp.where(r2 >= t, r2 - t, r2)
    first_group = lax.broadcasted_iota(jnp.int32, (t, LANES), 1) < HEAD_DIM
    for p in range(N_PAIRS):
        wp = jnp.where(causal, w_ref[p], 0.0).astype(BF16)
        for ci in range(rows // t):
            rs = slice(ci * t, (ci + 1) * t)
            cs = slice(p * LANES, (p + 1) * LANES)
            res = _dot(wp, vn[rs, cs])
            mixed = jnp.where(first_group, res[:t], res[t:]) + bias_ref[:, cs]
            o_ref[0, rs, cs] = (u[rs, cs] * mixed).astype(o_ref.dtype)


def _mixer_sgu(proj3, ln_g, w_s, b_s, rows=512):
    b, s, _ = proj3.shape
    rows = min(rows, s)
    t = SGU_CHUNK
    bias_full = jnp.repeat(b_s.T, HEAD_DIM, axis=1)
    w_pairs = w_s.reshape(N_PAIRS, 2 * t, t)
    return pl.pallas_call(
        functools.partial(_sgu_kernel, rows=rows),
        out_shape=jax.ShapeDtypeStruct((b, s, GW), BF16),
        grid=(b, s // rows),
        in_specs=[pl.BlockSpec((1, rows, GW), lambda i, j: (i, j, COL_B // GW)),
                  pl.BlockSpec((1, rows, GW), lambda i, j: (i, j, COL_B // GW + 1)),
                  pl.BlockSpec((1, GW), lambda i, j: (0, 0)),
                  pl.BlockSpec((N_PAIRS, 2 * t, t), lambda i, j: (0, 0, 0)),
                  pl.BlockSpec((t, GW), lambda i, j: (0, 0))],
        out_specs=pl.BlockSpec((1, rows, GW), lambda i, j: (i, j, 0)),
        compiler_params=_cparams(("arbitrary", "arbitrary")),
        name="mixer_sgu",
    )(proj3, proj3, ln_g.reshape(1, GW), w_pairs, bias_full)


def _moba_kernel(q_ref, k_ref, v_ref, bias_ref, wg_ref, wu_ref, wd_ref, o_ref, wgb_ref, wub_ref, wdb_ref,
                 kh_scr, vt_scr, ot_scr, *, seq):
    for src, dst in ((wg_ref, wgb_ref), (wu_ref, wub_ref), (wd_ref, wdb_ref)):
        dst[...] = src[...].astype(dst.dtype)

    ones_rows = 16
    bs = MOBA_BLOCK
    nblk = seq // bs
    lane = lax.broadcasted_iota(jnp.int32, (1, LANES), 1)
    head_lanes = (lane < HEAD_DIM, lane >= HEAD_DIM)
    blk = lambda i: slice(i * bs, (i + 1) * bs)

    q_all = q_ref[0]
    kbar = jnp.concatenate(
        [jnp.mean(k_ref[0, blk(j), :].astype(F32), axis=0, keepdims=True) for j in range(nblk)], axis=0)
    for j in range(nblk):
        kj = k_ref[0, blk(j), :]
        for h in range(2):
            kh_scr[h, j] = jnp.where(head_lanes[h], kj, jnp.zeros_like(kj))
        vt = v_ref[0, blk(j), :].astype(F32).T.astype(BF16)
        for h in range(2):
            vt_scr[j, h] = jnp.concatenate(
                [vt[h * HEAD_DIM:(h + 1) * HEAD_DIM], jnp.ones((ones_rows, bs), BF16)], axis=0)

    jrow = lax.broadcasted_iota(jnp.int32, (nblk, seq), 0)
    own = lax.broadcasted_iota(jnp.int32, (nblk, seq), 1) // bs
    krow = lax.broadcasted_iota(jnp.int32, (bs, bs), 0)
    qcol = lax.broadcasted_iota(jnp.int32, (bs, bs), 1)
    causal = krow <= qcol

    kb2 = jnp.concatenate([jnp.where(head_lanes[0], kbar, 0.0), jnp.where(head_lanes[1], kbar, 0.0)], axis=0)
    kb_hi, kb_lo = _split2(kb2)
    gates = _dot_nt(kb_hi, q_all) + _dot_nt(kb_lo, q_all)
    sel = []
    for h in range(2):
        gate = jnp.where(jrow < own, gates[h * nblk:(h + 1) * nblk], NEG_INF)
        rank = jnp.zeros((nblk, seq), jnp.int32)
        for j2 in range(nblk):
            gj = gate[j2:j2 + 1, :]
            ahead = (gj > gate) | ((gj == gate) & (j2 < jrow))
            rank = rank + ahead.astype(jnp.int32)
        sel.append((rank < MOBA_TOPK) & (jrow < own))

    def logits(grp):
        probs = [(h, qb) for qb in (grp, nblk - 1 - grp) for h in range(2)]
        tiles = [(h, qb, j) for h, qb in probs for j in range(qb + 1)]
        qh = {qb: q_ref[0, blk(qb), :] for _, qb in probs}
        s = {(h, qb, j): _dot_nt(kh_scr[h, j], qh[qb]) + bias_ref[h, qb - j] for h, qb, j in tiles}
        for h, qb in probs:
            s[h, qb, qb] = jnp.where(causal, s[h, qb, qb], NEG_INF)
        return probs, tiles, s

    def softmax_pv(probs, tiles, s):
        picked = {(h, qb, j): sel[h][j:j + 1, blk(qb)] for h, qb, j in tiles if j != qb}
        cmax = {t: jnp.max(s[t], axis=0, keepdims=True) for t in tiles}
        m = {}
        for h, qb in probs:
            m[h, qb] = functools.reduce(
                jnp.maximum, [cmax[h, qb, qb]] + [jnp.where(picked[h, qb, j], cmax[h, qb, j], NEG_INF)
                                                  for j in range(qb)])
        shift = {t: m[t[0], t[1]] if t[2] == t[1] else jnp.where(picked[t], m[t[0], t[1]], -NEG_INF)
                 for t in tiles}
        p = {t: jnp.exp2(s[t] - shift[t]).astype(BF16) for t in tiles}
        for h, qb in probs:
            acc = sum(_dot(vt_scr[j, h], p[h, qb, j]) for j in range(qb + 1))
            ot_scr[qb, h * HEAD_DIM:(h + 1) * HEAD_DIM, :] = acc[:HEAD_DIM] / acc[HEAD_DIM:HEAD_DIM + 1]

    pending = logits(0)
    for grp in range(1, nblk // 2):
        nxt = logits(grp)
        softmax_pv(*pending)
        pending = nxt
    softmax_pv(*pending)

    for qb in range(nblk):
        o_ref[0, blk(qb), :] = ot_scr[qb].T.astype(o_ref.dtype)


def _mixer_moba(proj3, bias_c, ffn_w, layer):
    b, s, _ = proj3.shape
    nblk = s // MOBA_BLOCK
    steps = N_PAIRS * b
    blk = lambda off: pl.BlockSpec((1, s, LANES), lambda p, i, off=off: (i, 0, off + p))
    w_shapes = [w.shape[1:] for w in ffn_w]
    assert all(r % (16 * steps) == 0 for r, _ in w_shapes)
    w_in_specs = [pl.BlockSpec((None, r // steps, c), lambda p, i: (layer, p * b + i, 0)) for r, c in w_shapes]
    w_out_specs = [pl.BlockSpec((r // steps, c), lambda p, i: (p * b + i, 0)) for r, c in w_shapes]
    out = pl.pallas_call(
        functools.partial(_moba_kernel, seq=s),
        out_shape=[jax.ShapeDtypeStruct((b, s, GW), BF16)] + [jax.ShapeDtypeStruct(sh, BF16) for sh in w_shapes],
        grid=(N_PAIRS, b),
        in_specs=[blk(COL_C // LANES), blk((COL_C + GW) // LANES), blk((COL_C + 2 * GW) // LANES),
                  pl.BlockSpec((2, nblk, MOBA_BLOCK, MOBA_BLOCK), lambda p, i: (p, 0, 0, 0))] + w_in_specs,
        out_specs=[pl.BlockSpec((1, s, LANES), lambda p, i: (i, 0, p))] + w_out_specs,
        scratch_shapes=[pltpu.VMEM((2, nblk, MOBA_BLOCK, LANES), BF16),
                        pltpu.VMEM((nblk, 2, HEAD_DIM + 16, MOBA_BLOCK), BF16),
                        pltpu.VMEM((nblk, LANES, MOBA_BLOCK), F32)],
        compiler_params=_cparams(("arbitrary", "arbitrary")),
        name="mixer_moba",
    )(proj3, proj3, proj3, bias_c, *ffn_w)
    return out[0], tuple(out[1:])


def _sigmoid(x):
    return 1.0 / (1.0 + jnp.exp(-x))


def _split2(x):
    hi = x.astype(BF16)
    return hi, (x - hi.astype(F32)).astype(BF16)


def _head_sum(x, ones_bd):
    hi, lo = _split2(x)
    return _dot(hi, ones_bd) + _dot(lo, ones_bd)


def _rwkv_prep_kernel(p_ref, prev_ref, mu_ref, w0_ref, w2_ref, a0_ref, a2_ref, g2_ref, kk_ref, ka_ref, rk_ref,
                      ones_ref, tri_ref, at_o, rt_o, bt_o, kt_o, be_o, ke_o, v_o, ee_o, g_o, bg_o, *, rows):
    c = RWKV_CHUNK
    p = p_ref[0].astype(F32)
    prev_row = jnp.where(pl.program_id(1) == 0, 0.0, prev_ref[0, 15:16, :].astype(F32))
    first_row = lax.broadcasted_iota(jnp.int32, (rows, 1), 0) == 0
    y_prev = jnp.where(first_row, prev_row, pltpu.roll(p, 1, axis=0))
    xs = p + (y_prev - p) * mu_ref[...]
    r = xs[:, 0:GW]
    k = xs[:, GW:2 * GW]
    v = xs[:, 2 * GW:3 * GW]
    wd = xs[:, 3 * GW:3 * GW + LORA_PAD]
    ad = xs[:, 3 * GW + LORA_PAD:3 * GW + 2 * LORA_PAD]
    gd = xs[:, 3 * GW + 2 * LORA_PAD:]
    nz = -(w0_ref[...] + _dot(jnp.tanh(wd).astype(BF16), w2_ref[...]))
    softplus = jnp.maximum(nz, 0.0) + jnp.log(1.0 + jnp.exp(-jnp.abs(nz)))
    log_decay = -jnp.exp(-softplus - 0.5)
    a_sig = _sigmoid(a0_ref[...] + _dot(ad.astype(BF16), a2_ref[...]))
    g = _dot(_sigmoid(gd).astype(BF16), g2_ref[...])
    kk = k * kk_ref[...]
    ss = _head_sum(kk * kk, ones_ref[...])
    kk = kk / jnp.maximum(jnp.sqrt(ss), 1e-12)
    k_mod = k * (1.0 + (a_sig - 1.0) * ka_ref[...])
    kb = kk * a_sig
    hi = log_decay.astype(BF16)
    rem = log_decay - hi.astype(F32)
    mid = rem.astype(BF16)
    lo = (rem - mid.astype(F32)).astype(BF16)
    tri = tri_ref[...]
    cum = _dot(tri, hi) + _dot(tri, mid) + _dot(tri, lo)
    cum_end = jnp.concatenate(
        [jnp.broadcast_to(cum[(i + 1) * c - 1:(i + 1) * c, :], (c, GW)) for i in range(rows // c)], axis=0)
    e_cum = jnp.exp(cum)
    e_inv = jnp.exp(-cum)
    e_rem = jnp.exp(cum_end - cum)
    coef = _head_sum(r * k_mod * rk_ref[...], ones_ref[...])
    outs = ((at_o, -kk * jnp.exp(cum - log_decay)), (rt_o, r * e_cum), (bt_o, kb * e_inv), (kt_o, k_mod * e_inv),
            (be_o, kb * e_rem), (ke_o, k_mod * e_rem), (v_o, v), (ee_o, jnp.exp(cum_end)), (g_o, g),
            (bg_o, coef * v * g))
    for ref, val in outs:
        for pr in range(N_PAIRS):
            ref[0, pr] = val[:, pr * LANES:(pr + 1) * LANES].astype(ref.dtype)


def _rwkv_prep(proj3, mu, w0, w2, a0, a2, g2, k_k, k_a, r_k, rows=256):
    b, s, _ = proj3.shape
    rows = min(rows, s)
    pad = lambda w, n: jnp.concatenate([w, jnp.zeros((n - w.shape[0],) + w.shape[1:], w.dtype)], axis=0)
    head_of = np.arange(GW) // HEAD_DIM
    ones_bd = jnp.asarray((head_of[:, None] == head_of[None, :]).astype(np.float32), dtype=BF16)
    tok = np.arange(rows)
    tri_bd = jnp.asarray(((tok[:, None] // RWKV_CHUNK == tok[None, :] // RWKV_CHUNK)
                          & (tok[None, :] <= tok[:, None])).astype(np.float32), dtype=BF16)
    vec = lambda a: a.reshape(1, -1)
    full = lambda shape: pl.BlockSpec(shape, lambda i, j: (0,) * len(shape))
    col = COL_D // RWKV_W
    sub = rows // 16
    sd = lambda dt: jax.ShapeDtypeStruct((b, N_PAIRS, s, LANES), dt)
    return pl.pallas_call(
        functools.partial(_rwkv_prep_kernel, rows=rows),
        out_shape=[sd(BF16)] * 7 + [sd(F32)] * 3,
        grid=(b, s // rows),
        in_specs=[pl.BlockSpec((1, rows, RWKV_W), lambda i, j: (i, j, col)),
                  pl.BlockSpec((1, 16, RWKV_W), lambda i, j: (i, jnp.maximum(j * sub - 1, 0), col)),
                  full((1, RWKV_W)), full((1, GW)), full((LORA_PAD, GW)), full((1, GW)),
                  full((LORA_PAD, GW)), full((G_LORA, GW)), full((1, GW)), full((1, GW)), full((1, GW)),
                  full((GW, GW)), full((rows, rows))],
        out_specs=[pl.BlockSpec((1, N_PAIRS, rows, LANES), lambda i, j: (i, 0, j, 0))] * 10,
        compiler_params=_cparams(("arbitrary", "arbitrary")),
        name="rwkv_prep",
    )(proj3, proj3, vec(mu), vec(w0), pad(w2, LORA_PAD).astype(BF16), vec(a0), pad(a2, LORA_PAD).astype(BF16),
      g2.astype(BF16), vec(k_k), vec(k_a), vec(r_k), ones_bd, tri_bd)


def _block_diag(y, first):
    zero = jnp.zeros_like(y)
    return jnp.concatenate([jnp.where(first, y, zero), jnp.where(first, zero, y)], axis=0)


def _pair_nn(x, y, first):
    return _dot(x, _block_diag(y, first))


def _pair_nt(x, y, first):
    return _dot_nt(x, _block_diag(y, first))


def _pair_tn(x, y, first):
    full = _dot_tn(x, y)
    return jnp.where(first, full[:HEAD_DIM], full[HEAD_DIM:])


def _rwkv_chunk_terms(probs, masks):
    first, strict, incl, same_sub, eye = masks
    c = RWKV_CHUNK
    bf = lambda xs: [x.astype(BF16) for x in xs]
    nn = lambda xs, ys: [_pair_nn(x, y, first) for x, y in zip(xs, ys)]
    ident = eye.astype(F32)
    at, rt, bt, kt, b_end, k_end, v, e_end = [list(x) for x in zip(*probs)]
    ar = [jnp.concatenate([a, r], axis=0) for a, r in zip(at, rt)]
    gb = [_pair_nt(x, y, first) for x, y in zip(ar, bt)]
    gk = [_pair_nt(x, y, first) for x, y in zip(ar, kt)]
    a_ab = [jnp.where(strict, g[:c], 0.0) for g in gb]
    a_rb = bf([jnp.where(incl, g[c:], 0.0) for g in gb])
    a_kr = bf([jnp.concatenate([jnp.where(strict, g[:c], 0.0), jnp.where(incl, g[c:], 0.0)], axis=0) for g in gk])
    ad = [jnp.where(same_sub, a, 0.0) for a in a_ab]
    an = bf([a - d for a, d in zip(a_ab, ad)])
    adb = bf(ad)
    p2 = nn(adb, adb)
    av = nn(a_kr, v)
    p2b = bf(p2)
    p4 = nn(p2b, p2b)
    x1 = nn(bf([ident + d for d in ad]), bf([ident + p for p in p2]))
    p4b = bf(p4)
    p8 = nn(p4b, p4b)
    x2 = nn(bf([ident + p for p in p4]), bf([ident + p for p in p8]))
    td = bf(nn(bf(x1), bf(x2)))
    m1 = nn(td, an)
    m1b = bf(m1)
    m2 = nn(m1b, m1b)
    x3 = nn(bf([ident + m for m in m1]), bf([ident + m for m in m2]))
    t_inv = bf(nn(bf(x3), td))
    a_hat = bf(nn(t_inv, at))
    u0 = bf(nn(t_inv, bf([w[:c] for w in av])))
    r_hat = [r.astype(F32) + x for r, x in zip(rt, nn(a_rb, a_hat))]
    o0 = [x + w[c:] for x, w in zip(nn(a_rb, u0), av)]
    p_mat = [jnp.where(eye, e, 0.0) + _pair_tn(h, b, first) for e, h, b in zip(e_end, a_hat, b_end)]
    z_mat = [_pair_tn(jnp.concatenate([u, w], axis=0), jnp.concatenate([b, k], axis=0), first)
             for u, w, b, k in zip(u0, v, b_end, k_end)]
    return list(zip(p_mat, z_mat, r_hat, o0))


def _rwkv_scan_kernel(at_ref, rt_ref, bt_ref, kt_ref, be_ref, ke_ref, v_ref, ee_ref, g_ref, bg_ref, lng_ref,
                      lnb_ref, o_ref, state, *, rows):
    c = RWKV_CHUNK
    ri = lax.broadcasted_iota(jnp.int32, (c, LANES), 0)
    lane = lax.broadcasted_iota(jnp.int32, (c, LANES), 1)
    first = lane < HEAD_DIM
    ci = jnp.where(first, lane, lane - HEAD_DIM)
    masks = (first, ci < ri, ci <= ri, (ri // RWKV_SUB) == (ci // RWKV_SUB), ri == ci)

    @pl.when(pl.program_id(1) == 0)
    def _():
        state[...] = jnp.zeros_like(state)

    def head_mean(x):
        lo = jnp.sum(jnp.where(first, x, 0.0), axis=-1, keepdims=True)
        hi = jnp.sum(jnp.where(first, 0.0, x), axis=-1, keepdims=True)
        return jnp.where(first, lo, hi) * (1.0 / HEAD_DIM)

    chunks = [slice(i * c, (i + 1) * c) for i in range(rows // c)]
    seq_refs = (at_ref, rt_ref, bt_ref, kt_ref, be_ref, ke_ref, v_ref)
    probs = [tuple(ref[0, pr, rs, :] for ref in seq_refs) + (ee_ref[0, pr, rs.start:rs.start + 1, :],)
             for rs in chunks for pr in range(N_PAIRS)]
    terms = _rwkv_chunk_terms(probs, masks)
    s_cur = [state[pr] for pr in range(N_PAIRS)]
    for i, rs in enumerate(chunks):
        for pr in range(N_PAIRS):
            p_mat, z_mat, r_hat, o0 = terms[i * N_PAIRS + pr]
            s_hi, s_lo = _split2(s_cur[pr])
            p_hi, p_lo = _split2(p_mat)
            o = _pair_nt(r_hat.astype(BF16), s_hi, first) + o0
            s_cur[pr] = (_pair_nn(s_hi, p_hi, first) + _pair_nn(s_hi, p_lo, first) + _pair_nn(s_lo, p_hi, first)
                         + z_mat)
            oc = o - head_mean(o)
            y = oc * lax.rsqrt(head_mean(oc * oc) + RWKV_LN_EPS) * lng_ref[pr:pr + 1, :] + lnb_ref[pr:pr + 1, :]
            o_ref[0, rs, pr * LANES:(pr + 1) * LANES] = (y * g_ref[0, pr, rs, :]
                                                         + bg_ref[0, pr, rs, :]).astype(o_ref.dtype)
    for pr in range(N_PAIRS):
        state[pr] = s_cur[pr]


def _rwkv_scan(prep, lnx_g, lnx_b, rows=512):
    b, _, s, _ = prep[0].shape
    rows = min(rows, s)
    seq_spec = pl.BlockSpec((1, N_PAIRS, rows, LANES), lambda i, t: (i, 0, t, 0))
    par_spec = pl.BlockSpec((N_PAIRS, LANES), lambda i, t: (0, 0))
    par = lambda a: a.reshape(N_PAIRS, LANES)
    return pl.pallas_call(
        functools.partial(_rwkv_scan_kernel, rows=rows),
        out_shape=jax.ShapeDtypeStruct((b, s, GW), BF16),
        grid=(b, s // rows),
        in_specs=[seq_spec] * 10 + [par_spec] * 2,
        out_specs=pl.BlockSpec((1, rows, GW), lambda i, t: (i, t, 0)),
        scratch_shapes=[pltpu.VMEM((N_PAIRS, HEAD_DIM, LANES), F32)],
        compiler_params=_cparams(("arbitrary", "arbitrary")),
        name="rwkv_scan",
    )(*prep, par(lnx_g), par(lnx_b))


def _outproj_kernel(ya_ref, yb_ref, yc_ref, yd_ref, g_ref, w_ref, x_ref, o_ref):
    acc = x_ref[...]
    for i, y_ref in enumerate((ya_ref, yb_ref, yc_ref, yd_ref)):
        yn = _rms(y_ref[...].astype(F32), g_ref[i:i + 1, :]).astype(BF16)
        acc = acc + _dot(yn, w_ref[i * GW:(i + 1) * GW, :])
    o_ref[...] = acc


def _outproj(ys, g, w_all, layer, x2d, tm=512):
    t, d = x2d.shape
    tm = min(tm, t)
    y_spec = pl.BlockSpec((tm, GW), lambda i: (i, 0))
    return pl.pallas_call(
        _outproj_kernel,
        out_shape=jax.ShapeDtypeStruct((t, d), F32),
        grid=(t // tm,),
        in_specs=[y_spec] * 4 + [pl.BlockSpec((4, GW), lambda i: (0, 0)),
                                 pl.BlockSpec((None, 4 * GW, d), lambda i: (layer, 0, 0)),
                                 pl.BlockSpec((tm, d), lambda i: (i, 0))],
        out_specs=pl.BlockSpec((tm, d), lambda i: (i, 0)),
        compiler_params=_cparams(("arbitrary",)),
        name="outproj",
    )(*[y.reshape(t, GW) for y in ys], g.reshape(4, GW), w_all, x2d)


def _ffn_kernel(x_ref, g_ref, wg_ref, wu_ref, wd_ref, gf_ref, o_ref, h_scr, *, final_norm):
    j = pl.program_id(1)

    @pl.when(j == 0)
    def _():
        x = x_ref[...]
        h_scr[...] = _rms(x, g_ref[...]).astype(BF16)
        o_ref[...] = x

    h = h_scr[...]
    gate = _dot(h, wg_ref[...])
    up = _dot(h, wu_ref[...])
    act = (gate * _sigmoid(gate) * up).astype(BF16)
    o_ref[...] += _dot(act, wd_ref[...])

    if final_norm:
        @pl.when(j == pl.num_programs(1) - 1)
        def _():
            o_ref[...] = _rms(o_ref[...], gf_ref[...])


def _ffn(x2d, g, wg, wu, wd, g_final, final_norm, tm=1024, tf=512):
    t, d = x2d.shape
    f = wg.shape[1]
    tm = min(tm, t)
    return pl.pallas_call(
        functools.partial(_ffn_kernel, final_norm=final_norm),
        out_shape=jax.ShapeDtypeStruct((t, d), F32),
        grid=(t // tm, f // tf),
        in_specs=[pl.BlockSpec((tm, d), lambda i, j: (i, 0)),
                  pl.BlockSpec((1, d), lambda i, j: (0, 0)),
                  pl.BlockSpec((d, tf), lambda i, j: (0, j)),
                  pl.BlockSpec((d, tf), lambda i, j: (0, j)),
                  pl.BlockSpec((tf, d), lambda i, j: (j, 0)),
                  pl.BlockSpec((1, d), lambda i, j: (0, 0))],
        out_specs=pl.BlockSpec((tm, d), lambda i, j: (i, 0)),
        scratch_shapes=[pltpu.VMEM((tm, d), BF16)],
        compiler_params=_cparams(("arbitrary", "arbitrary")),
        name="ffn",
    )(x2d, g.reshape(1, d), wg, wu, wd, g_final.reshape(1, d))


def _cast_kernel(x_ref, o_ref):
    o_ref[...] = x_ref[...].astype(o_ref.dtype)


def _to_bf16(w, rows=256):
    shape = w.shape
    w2 = w.reshape(-1, shape[-1])
    n, c = w2.shape
    out = pl.pallas_call(
        _cast_kernel,
        out_shape=jax.ShapeDtypeStruct((n, c), BF16),
        grid=(n // rows,),
        in_specs=[pl.BlockSpec((rows, c), lambda i: (i, 0))],
        out_specs=pl.BlockSpec((rows, c), lambda i: (i, 0)),
        compiler_params=_cparams(("arbitrary",)),
        name="cast_bf16",
    )(w2)
    return out.reshape(shape)


_RWKV_SRC = np.concatenate([[0], np.cumsum((GW, W_LORA, GW, GW, A_LORA, G_LORA))])
_RWKV_DST = (0, 3 * GW, GW, 2 * GW, 3 * GW + LORA_PAD, 3 * GW + 2 * LORA_PAD)


def _reorder_rwkv(a):
    out = jnp.zeros(a.shape[:-1] + (RWKV_W,), a.dtype)
    for i, dst in enumerate(_RWKV_DST):
        lo, hi = int(_RWKV_SRC[i]), int(_RWKV_SRC[i + 1])
        out = out.at[..., dst:dst + hi - lo].set(a[..., lo:hi])
    return out


def _prepare_w_in(w_in):
    wt = jnp.swapaxes(w_in, 1, 2)
    zeros = lambda n: jnp.zeros(wt.shape[:1] + (n, wt.shape[2]), BF16)
    rows = lambda lo, hi: wt[:, lo:hi].astype(BF16)
    piece = lambda i: rows(SRC_D + int(_RWKV_SRC[i]), SRC_D + int(_RWKV_SRC[i + 1]))
    c_q = (wt[:, SRC_C:SRC_C + GW] * (ATT_SCALE * LOG2E)).astype(BF16)
    return jnp.concatenate([rows(0, A_W),
                            piece(0), piece(2), piece(3), piece(1), zeros(LORA_PAD - W_LORA),
                            piece(4), zeros(LORA_PAD - A_LORA), piece(5),
                            rows(SRC_B, SRC_C), c_q, rows(SRC_C + GW, SRC_D)], axis=1)


def kernel(x, norm_mix_g, w_in, pos_bias, sgu_ln_g, sgu_w, sgu_b, rwkv_mu, rwkv_w0, rwkv_w2, rwkv_a0, rwkv_a2,
           rwkv_g2, rwkv_k_k, rwkv_k_a, rwkv_r_k, rwkv_lnx_g, rwkv_lnx_b, branch_norm_g, w_out, norm_ffn_g,
           w_gate, w_up, w_down, norm_final_g):
    b, s, d = x.shape
    depth = w_in.shape[0]
    assert s % (DIL_BLOCK * DIL_PATTERNS[-1][1]) == 0 and s % (2 * MOBA_BLOCK) == 0
    bias_a = _bias_tiles(pos_bias, _dil_bucket_tiles(), 0)
    bias_c = _bias_tiles(pos_bias, _moba_bucket_tiles(s // MOBA_BLOCK), N_HEADS)
    x2d = x.reshape(b * s, d)
    w_out_b = _to_bf16(w_out)
    w_in_b, mu_all = _prepare_w_in(w_in), _reorder_rwkv(rwkv_mu)
    for l in range(depth):
        mu_l = mu_all[l]
        proj_a, proj_bcd = _inproj(x2d, norm_mix_g[l], w_in_b, l)
        proj3 = proj_bcd.reshape(b, s, BCD_W)
        ya = _mixer_dilated(proj_a.reshape(b, s, A_W), bias_a)
        yb = _mixer_sgu(proj3, sgu_ln_g[l], sgu_w[l], sgu_b[l])
        yc, ffn_w = _mixer_moba(proj3, bias_c, (w_gate, w_up, w_down), l)
        prep = _rwkv_prep(proj3, mu_l, rwkv_w0[l], rwkv_w2[l], rwkv_a0[l], rwkv_a2[l], rwkv_g2[l],
                          rwkv_k_k[l], rwkv_k_a[l], rwkv_r_k[l])
        yd = _rwkv_scan(prep, rwkv_lnx_g[l], rwkv_lnx_b[l])
        x2d = _outproj((ya, yb, yc, yd), branch_norm_g[l], w_out_b, l, x2d)
        x2d = _ffn(x2d, norm_ffn_g[l], *ffn_w, norm_final_g,
                   final_norm=(l == depth - 1))
    return x2d.reshape(b, s, d)
```

```python
import functools
import math

import jax
import jax.numpy as jnp
import numpy as np
from jax import lax
from jax.experimental import pallas as pl
from jax.experimental.pallas import tpu as pltpu

F32 = jnp.float32
BF16 = jnp.bfloat16

HEAD_DIM = 64
N_HEADS = 8
GW = N_HEADS * HEAD_DIM
LANES = 128
N_PAIRS = GW // LANES
DIL_PATTERNS = ((128, 1), (512, 4), (2048, 16))
DIL_BLOCK = 128
SGU_CHUNK = 128
SGU_LN_EPS = 1e-5
MOBA_BLOCK = 256
MOBA_TOPK = 3
W_LORA = 96
A_LORA = 96
G_LORA = 256
LORA_PAD = 128
RWKV_LN_EPS = 64e-5
RWKV_CHUNK = 64
RWKV_SUB = 16
NUM_BUCKETS = 32
MAX_DISTANCE = 2048
NORM_EPS = 1e-6
NEG_INF = -1e30
ATT_SCALE = HEAD_DIM ** -0.5
LOG2E = math.log2(math.e)

RWKV_W = 3 * GW + 2 * LORA_PAD + G_LORA
A_W = 3 * GW
SRC_B = A_W
SRC_C = SRC_B + 2 * GW
SRC_D = SRC_C + 3 * GW
COL_D = 0
COL_B = RWKV_W
COL_C = COL_B + 2 * GW
BCD_W = COL_C + 3 * GW
D_PROJ = A_W + BCD_W

VMEM_LIMIT = 56 * 1024 * 1024

HIGHEST = lax.Precision.HIGHEST


def _cparams(sem):
    return pltpu.CompilerParams(dimension_semantics=sem, vmem_limit_bytes=VMEM_LIMIT)


def _dot(a, b, precision=None):
    return lax.dot_general(a, b, (((1,), (0,)), ((), ())), precision=precision,
                           preferred_element_type=F32)


def _dot_nt(a, b, precision=None):
    return lax.dot_general(a, b, (((1,), (1,)), ((), ())), precision=precision,
                           preferred_element_type=F32)


def _dot_tn(a, b, precision=None):
    return lax.dot_general(a, b, (((0,), (0,)), ((), ())), precision=precision,
                           preferred_element_type=F32)


def _rms(x, g):
    return x * lax.rsqrt(jnp.mean(x * x, axis=-1, keepdims=True) + NORM_EPS) * g


def _t5_bucket_np(dist):
    dist = np.maximum(dist, 0)
    max_exact = NUM_BUCKETS // 2
    d = np.maximum(dist, 1).astype(np.float32)
    large = max_exact + (np.log(d / np.float32(max_exact)) / np.float32(math.log(MAX_DISTANCE / max_exact))
                         * np.float32(NUM_BUCKETS - max_exact)).astype(np.int32)
    large = np.minimum(large, NUM_BUCKETS - 1)
    return np.where(dist < max_exact, dist, large).astype(np.int32)


def _dil_bucket_tiles():
    qa = np.arange(DIL_BLOCK)[:, None]
    kj = np.arange(DIL_BLOCK)[None, :]
    tiles = []
    for _, dil in DIL_PATTERNS:
        tiles.append(_t5_bucket_np((qa - kj) * dil))
        tiles.append(_t5_bucket_np((qa + DIL_BLOCK - kj) * dil))
    return np.stack(tiles)


def _moba_bucket_tiles(nblk):
    ki = np.arange(MOBA_BLOCK)[:, None]
    qi = np.arange(MOBA_BLOCK)[None, :]
    return np.stack([_t5_bucket_np(db * MOBA_BLOCK + qi - ki) for db in range(nblk)])


def _bias_tile_kernel(tbl_ref, idx_ref, o_ref, *, head_offset, buckets):
    h = pl.program_id(0) + head_offset
    for t, present in enumerate(buckets):
        idx = idx_ref[t]
        acc = jnp.zeros(idx.shape, F32)
        for b in present:
            acc = jnp.where(idx == b, tbl_ref[b, h] * LOG2E, acc)
        o_ref[0, t] = acc


def _bias_tiles(pos_bias, idx_np, head_offset):
    nt, r, c = idx_np.shape
    buckets = tuple(tuple(int(b) for b in np.unique(idx_np[t])) for t in range(nt))
    return pl.pallas_call(
        functools.partial(_bias_tile_kernel, head_offset=head_offset, buckets=buckets),
        out_shape=jax.ShapeDtypeStruct((N_HEADS, nt, r, c), F32),
        grid=(N_HEADS,),
        in_specs=[pl.BlockSpec(memory_space=pltpu.SMEM),
                  pl.BlockSpec((nt, r, c), lambda h: (0, 0, 0))],
        out_specs=pl.BlockSpec((1, nt, r, c), lambda h: (h, 0, 0, 0)),
        compiler_params=_cparams(("arbitrary",)),
        name="bias_tiles",
    )(pos_bias, jnp.asarray(idx_np))


def _inproj_kernel(x_ref, g_ref, w_ref, oa_ref, ob_ref, h_scr):
    j = pl.program_id(1)

    @pl.when(j == 0)
    def _():
        h_scr[...] = _rms(x_ref[...], g_ref[...]).astype(BF16)
        oa_ref[...] = _dot_nt(h_scr[...], w_ref[...])

    @pl.when(j > 0)
    def _():
        ob_ref[...] = _dot_nt(h_scr[...], w_ref[...]).astype(BF16)


def _inproj(x2d, g, wt_all, layer, tm=1024):
    t, d = x2d.shape
    tn = A_W
    tm = min(tm, t)
    return pl.pallas_call(
        _inproj_kernel,
        out_shape=[jax.ShapeDtypeStruct((t, A_W), F32), jax.ShapeDtypeStruct((t, BCD_W), BF16)],
        grid=(t // tm, D_PROJ // tn),
        in_specs=[pl.BlockSpec((tm, d), lambda i, j: (i, 0)),
                  pl.BlockSpec((1, d), lambda i, j: (0, 0)),
                  pl.BlockSpec((None, tn, d), lambda i, j: (layer, j, 0))],
        out_specs=[pl.BlockSpec((tm, tn), lambda i, j: (i, 0)),
                   pl.BlockSpec((tm, tn), lambda i, j: (i, jnp.maximum(j - 1, 0)))],
        scratch_shapes=[pltpu.VMEM((tm, d), BF16)],
        compiler_params=_cparams(("arbitrary", "arbitrary")),
        name="inproj",
    )(x2d, g.reshape(1, d), wt_all)


def _dilated_kernel(q_ref, k_ref, v_ref, bias_ref, o_ref, m0, m1, l0, l1, acc, *, seq):
    c = DIL_BLOCK
    lane = lax.broadcasted_iota(jnp.int32, (c, LANES), 1)
    head0 = lane < HEAD_DIM
    row = lax.broadcasted_iota(jnp.int32, (c, c), 0)
    col = lax.broadcasted_iota(jnp.int32, (c, c), 1)
    cur_valid = col <= row
    prev_valid = col >= row

    stats = ((m0, l0), (m1, l1))
    head_lanes = (head0, ~head0)

    def logits(pairs, pi, wide, first):
        valid = jnp.concatenate([prev_valid, cur_valid], axis=1) if wide else cur_valid
        bias = [jnp.concatenate([bias_ref[h, 2 * pi + 1], bias_ref[h, 2 * pi]], axis=1) if wide
                else bias_ref[h, 2 * pi] for h in range(2)]
        q = [q_ref[0, qsl, :] * (ATT_SCALE * LOG2E) for qsl, _ in pairs]
        kb = [k_ref[0, ksl, :].astype(BF16) for _, ksl in pairs]
        idx = [(h, i) for i in range(len(pairs)) for h in range(2)]
        s = {(h, i): _dot_nt(jnp.where(head_lanes[h], q[i], 0.0).astype(BF16), kb[i]) for h, i in idx}
        s = {hi: jnp.where(valid, s[hi] + bias[hi[0]], NEG_INF) for hi in idx}
        return pairs, wide, first, idx, s

    def softmax_update(pairs, wide, first, idx, s):
        rep = (lambda x: jnp.concatenate([x, x], axis=1)) if wide else (lambda x: x)
        ones = jnp.ones(((2 if wide else 1) * c, LANES), BF16)
        vb = [jnp.concatenate([v_ref[0, ksl, :].astype(BF16), ones], axis=1) for _, ksl in pairs]
        m_new = {hi: jnp.broadcast_to(jnp.max(s[hi], axis=1, keepdims=True), (c, LANES)) for hi in idx}
        if not first:
            m_old = {(h, i): stats[h][0][pairs[i][0], :] for h, i in idx}
            m_new = {hi: jnp.maximum(m_old[hi], m_new[hi]) for hi in idx}
            alpha = {hi: jnp.exp2(m_old[hi] - m_new[hi]) for hi in idx}
        p = {hi: jnp.exp2(s[hi] - rep(m_new[hi])).astype(BF16) for hi in idx}
        o = {(h, i): _dot(p[h, i], vb[i]) for h, i in idx}
        for h, i in idx:
            m_ref, l_ref = stats[h]
            qsl = pairs[i][0]
            l_new = o[h, i][:, LANES:]
            l_ref[qsl, :] = l_new if first else alpha[h, i] * l_ref[qsl, :] + l_new
            m_ref[qsl, :] = m_new[h, i]
        for i, (qsl, _) in enumerate(pairs):
            o_new = jnp.where(head0, o[0, i][:, :LANES], o[1, i][:, :LANES])
            acc[qsl, :] = o_new if first else acc[qsl, :] * jnp.where(head0, alpha[0, i], alpha[1, i]) + o_new

    def group_size(n, cap):
        return max(g for g in range(1, cap + 1) if n % g == 0)

    order = sorted(range(len(DIL_PATTERNS)), key=lambda i: -DIL_PATTERNS[i][1])
    groups = []
    for pi in order:
        dil = DIL_PATTERNS[pi][1]
        first_pass = pi == order[0]
        nb = seq // dil // c

        def sl(r, n, blocks, dil=dil):
            start = r + n * (c * dil)
            return pl.ds(start, blocks * c) if dil == 1 else pl.ds(start, blocks * c, stride=dil)

        heads = [(sl(r, 0, 1),) * 2 for r in range(dil)]
        body = [(sl(r, n, 1), sl(r, n - 1, 2)) for r in range(dil) for n in range(1, nb)]
        for tiles, wide, cap in ((heads, False, 8), (body, True, 4)):
            g = group_size(len(tiles), cap) if tiles else 1
            groups += [(tiles[i:i + g], pi, wide, first_pass) for i in range(0, len(tiles), g)]

    pending = logits(*groups[0])
    for grp in groups[1:]:
        nxt = logits(*grp)
        softmax_update(*pending)
        pending = nxt
    softmax_update(*pending)

    o_ref[0] = (acc[...] / jnp.where(head0[:1], l0[...], l1[...])).astype(o_ref.dtype)


def _mixer_dilated(proj_a, bias_a):
    b, s, _ = proj_a.shape
    proj3 = proj_a
    blk = lambda off: pl.BlockSpec((1, s, LANES), lambda p, i, off=off: (i, 0, off + p))
    return pl.pallas_call(
        functools.partial(_dilated_kernel, seq=s),
        out_shape=jax.ShapeDtypeStruct((b, s, GW), BF16),
        grid=(N_PAIRS, b),
        in_specs=[blk(0), blk(GW // LANES), blk(2 * GW // LANES),
                  pl.BlockSpec((2, 2 * len(DIL_PATTERNS), DIL_BLOCK, DIL_BLOCK),
                               lambda p, i: (p, 0, 0, 0))],
        out_specs=pl.BlockSpec((1, s, LANES), lambda p, i: (i, 0, p)),
        scratch_shapes=[pltpu.VMEM((s, LANES), F32)] * 5,
        compiler_params=_cparams(("arbitrary", "arbitrary")),
        name="mixer_dilated",
    )(proj3, proj3, proj3, bias_a)


def _gelu_tanh(x):
    return 0.5 * x * (1.0 + jnp.tanh(math.sqrt(2.0 / math.pi) * (x + 0.044715 * (x * x * x))))


def _sgu_kernel(u_ref, v_ref, lng_ref, w_ref, bias_ref, o_ref, *, rows):
    t = SGU_CHUNK
    u = _gelu_tanh(u_ref[0].astype(F32))
    v = _gelu_tanh(v_ref[0].astype(F32))
    mu = jnp.mean(v, axis=-1, keepdims=True)
    vc = v - mu
    var = jnp.mean(vc * vc, axis=-1, keepdims=True)
    vn = (vc * lax.rsqrt(var + SGU_LN_EPS) * lng_ref[...]).astype(BF16)
    r2 = lax.broadcasted_iota(jnp.int32, (2 * t, t), 0)
    c2 = lax.broadcasted_iota(jnp.int32, (2 * t, t), 1)
    causal = c2 <= jnp.where(r2 >= t, r2 - t, r2)
    first_group = lax.broadcasted_iota(jnp.int32, (t, LANES), 1) < HEAD_DIM
    for p in range(N_PAIRS):
        wp = jnp.where(causal, w_ref[p], 0.0).astype(BF16)
        for ci in range(rows // t):
            rs = slice(ci * t, (ci + 1) * t)
            cs = slice(p * LANES, (p + 1) * LANES)
            res = _dot(wp, vn[rs, cs])
            mixed = jnp.where(first_group, res[:t], res[t:]) + bias_ref[:, cs]
            o_ref[0, rs, cs] = (u[rs, cs] * mixed).astype(o_ref.dtype)


def _mixer_sgu(proj3, ln_g, w_s, b_s, rows=512):
    b, s, _ = proj3.shape
    rows = min(rows, s)
    t = SGU_CHUNK
    bias_full = jnp.repeat(b_s.T, HEAD_DIM, axis=1)
    w_pairs = w_s.reshape(N_PAIRS, 2 * t, t)
    return pl.pallas_call(
        functools.partial(_sgu_kernel, rows=rows),
        out_shape=jax.ShapeDtypeStruct((b, s, GW), BF16),
        grid=(b, s // rows),
        in_specs=[pl.BlockSpec((1, rows, GW), lambda i, j: (i, j, COL_B // GW)),
                  pl.BlockSpec((1, rows, GW), lambda i, j: (i, j, COL_B // GW + 1)),
                  pl.BlockSpec((1, GW), lambda i, j: (0, 0)),
                  pl.BlockSpec((N_PAIRS, 2 * t, t), lambda i, j: (0, 0, 0)),
                  pl.BlockSpec((t, GW), lambda i, j: (0, 0))],
        out_specs=pl.BlockSpec((1, rows, GW), lambda i, j: (i, j, 0)),
        compiler_params=_cparams(("arbitrary", "arbitrary")),
        name="mixer_sgu",
    )(proj3, proj3, ln_g.reshape(1, GW), w_pairs, bias_full)


def _moba_kernel(q_ref, k_ref, v_ref, bias_ref, wg_ref, wu_ref, wd_ref, o_ref, wgb_ref, wub_ref, wdb_ref,
                 kh_scr, vt_scr, ot_scr, *, seq):
    for src, dst in ((wg_ref, wgb_ref), (wu_ref, wub_ref), (wd_ref, wdb_ref)):
        dst[...] = src[...].astype(dst.dtype)

    ones_rows = 16
    bs = MOBA_BLOCK
    nblk = seq // bs
    lane = lax.broadcasted_iota(jnp.int32, (1, LANES), 1)
    head_lanes = (lane < HEAD_DIM, lane >= HEAD_DIM)
    blk = lambda i: slice(i * bs, (i + 1) * bs)

    q_all = q_ref[0]
    kbar = jnp.concatenate(
        [jnp.mean(k_ref[0, blk(j), :].astype(F32), axis=0, keepdims=True) for j in range(nblk)], axis=0)
    for j in range(nblk):
        kj = k_ref[0, blk(j), :]
        for h in range(2):
            kh_scr[h, j] = jnp.where(head_lanes[h], kj, jnp.zeros_like(kj))
        vt = v_ref[0, blk(j), :].astype(F32).T.astype(BF16)
        for h in range(2):
            vt_scr[j, h] = jnp.concatenate(
                [vt[h * HEAD_DIM:(h + 1) * HEAD_DIM], jnp.ones((ones_rows, bs), BF16)], axis=0)

    jrow = lax.broadcasted_iota(jnp.int32, (nblk, seq), 0)
    own = lax.broadcasted_iota(jnp.int32, (nblk, seq), 1) // bs
    krow = lax.broadcasted_iota(jnp.int32, (bs, bs), 0)
    qcol = lax.broadcasted_iota(jnp.int32, (bs, bs), 1)
    causal = krow <= qcol

    kb2 = jnp.concatenate([jnp.where(head_lanes[0], kbar, 0.0), jnp.where(head_lanes[1], kbar, 0.0)], axis=0)
    kb_hi, kb_lo = _split2(kb2)
    gates = _dot_nt(kb_hi, q_all) + _dot_nt(kb_lo, q_all)
    sel = []
    for h in range(2):
        gate = jnp.where(jrow < own, gates[h * nblk:(h + 1) * nblk], NEG_INF)
        rank = jnp.zeros((nblk, seq), jnp.int32)
        for j2 in range(nblk):
            gj = gate[j2:j2 + 1, :]
            ahead = (gj > gate) | ((gj == gate) & (j2 < jrow))
            rank = rank + ahead.astype(jnp.int32)
        sel.append((rank < MOBA_TOPK) & (jrow < own))

    def logits(grp):
        probs = [(h, qb) for qb in (grp, nblk - 1 - grp) for h in range(2)]
        tiles = [(h, qb, j) for h, qb in probs for j in range(qb + 1)]
        qh = {qb: q_ref[0, blk(qb), :] for _, qb in probs}
        s = {(h, qb, j): _dot_nt(kh_scr[h, j], qh[qb]) + bias_ref[h, qb - j] for h, qb, j in tiles}
        for h, qb in probs:
            s[h, qb, qb] = jnp.where(causal, s[h, qb, qb], NEG_INF)
        return probs, tiles, s

    def softmax_pv(probs, tiles, s):
        picked = {(h, qb, j): sel[h][j:j + 1, blk(qb)] for h, qb, j in tiles if j != qb}
        cmax = {t: jnp.max(s[t], axis=0, keepdims=True) for t in tiles}
        m = {}
        for h, qb in probs:
            m[h, qb] = functools.reduce(
                jnp.maximum, [cmax[h, qb, qb]] + [jnp.where(picked[h, qb, j], cmax[h, qb, j], NEG_INF)
                                                  for j in range(qb)])
        shift = {t: m[t[0], t[1]] if t[2] == t[1] else jnp.where(picked[t], m[t[0], t[1]], -NEG_INF)
                 for t in tiles}
        p = {t: jnp.exp2(s[t] - shift[t]).astype(BF16) for t in tiles}
        for h, qb in probs:
            acc = sum(_dot(vt_scr[j, h], p[h, qb, j]) for j in range(qb + 1))
            ot_scr[qb, h * HEAD_DIM:(h + 1) * HEAD_DIM, :] = acc[:HEAD_DIM] / acc[HEAD_DIM:HEAD_DIM + 1]

    pending = logits(0)
    for grp in range(1, nblk // 2):
        nxt = logits(grp)
        softmax_pv(*pending)
        pending = nxt
    softmax_pv(*pending)

    for qb in range(nblk):
        o_ref[0, blk(qb), :] = ot_scr[qb].T.astype(o_ref.dtype)


def _mixer_moba(proj3, bias_c, ffn_w, layer):
    b, s, _ = proj3.shape
    nblk = s // MOBA_BLOCK
    steps = N_PAIRS * b
    blk = lambda off: pl.BlockSpec((1, s, LANES), lambda p, i, off=off: (i, 0, off + p))
    w_shapes = [w.shape[1:] for w in ffn_w]
    assert all(r % (16 * steps) == 0 for r, _ in w_shapes)
    w_in_specs = [pl.BlockSpec((None, r // steps, c), lambda p, i: (layer, p * b + i, 0)) for r, c in w_shapes]
    w_out_specs = [pl.BlockSpec((r // steps, c), lambda p, i: (p * b + i, 0)) for r, c in w_shapes]
    out = pl.pallas_call(
        functools.partial(_moba_kernel, seq=s),
        out_shape=[jax.ShapeDtypeStruct((b, s, GW), BF16)] + [jax.ShapeDtypeStruct(sh, BF16) for sh in w_shapes],
        grid=(N_PAIRS, b),
        in_specs=[blk(COL_C // LANES), blk((COL_C + GW) // LANES), blk((COL_C + 2 * GW) // LANES),
                  pl.BlockSpec((2, nblk, MOBA_BLOCK, MOBA_BLOCK), lambda p, i: (p, 0, 0, 0))] + w_in_specs,
        out_specs=[pl.BlockSpec((1, s, LANES), lambda p, i: (i, 0, p))] + w_out_specs,
        scratch_shapes=[pltpu.VMEM((2, nblk, MOBA_BLOCK, LANES), BF16),
                        pltpu.VMEM((nblk, 2, HEAD_DIM + 16, MOBA_BLOCK), BF16),
                        pltpu.VMEM((nblk, LANES, MOBA_BLOCK), F32)],
        compiler_params=_cparams(("arbitrary", "arbitrary")),
        name="mixer_moba",
    )(proj3, proj3, proj3, bias_c, *ffn_w)
    return out[0], tuple(out[1:])


def _sigmoid(x):
    return 1.0 / (1.0 + jnp.exp(-x))


def _split2(x):
    hi = x.astype(BF16)
    return hi, (x - hi.astype(F32)).astype(BF16)


def _head_sum(x, ones_bd):
    hi, lo = _split2(x)
    cols = [slice(i * LANES, (i + 1) * LANES) for i in range(x.shape[1] // LANES)]
    return jnp.concatenate([_dot(hi[:, cs], ones_bd) + _dot(lo[:, cs], ones_bd) for cs in cols], axis=1)


def _rwkv_token_terms(p_ref, prev_ref, mu_ref, w0_ref, w2_ref, a0_ref, a2_ref, g2_ref, kk_ref, ka_ref, rk_ref,
                      ones_ref, tri_ref, rows):
    c = RWKV_CHUNK
    p = p_ref[0].astype(F32)
    prev_row = 0.0 if prev_ref is None else prev_ref[0, 15:16, :].astype(F32)
    first_row = lax.broadcasted_iota(jnp.int32, (rows, 1), 0) == 0
    y_prev = jnp.where(first_row, prev_row, pltpu.roll(p, 1, axis=0))
    xs = p + (y_prev - p) * mu_ref[...]
    yield
    r = xs[:, 0:GW]
    k = xs[:, GW:2 * GW]
    v = xs[:, 2 * GW:3 * GW]
    wd = xs[:, 3 * GW:3 * GW + LORA_PAD]
    ad = xs[:, 3 * GW + LORA_PAD:3 * GW + 2 * LORA_PAD]
    gd = xs[:, 3 * GW + 2 * LORA_PAD:]
    nz = -(w0_ref[...] + _dot(jnp.tanh(wd).astype(BF16), w2_ref[...]))
    yield
    softplus = jnp.maximum(nz, 0.0) + jnp.log(1.0 + jnp.exp(-jnp.abs(nz)))
    log_decay = -jnp.exp(-softplus - 0.5)
    yield
    a_sig = _sigmoid(a0_ref[...] + _dot(ad.astype(BF16), a2_ref[...]))
    yield
    g = _dot(_sigmoid(gd).astype(BF16), g2_ref[...])
    yield
    kk = k * kk_ref[...]
    ss = _head_sum(kk * kk, ones_ref[...])
    yield
    kk = kk / jnp.maximum(jnp.sqrt(ss), 1e-12)
    yield
    k_mod = k * (1.0 + (a_sig - 1.0) * ka_ref[...])
    kb = kk * a_sig
    yield
    hi = log_decay.astype(BF16)
    rem = log_decay - hi.astype(F32)
    mid = rem.astype(BF16)
    lo = (rem - mid.astype(F32)).astype(BF16)
    tri = tri_ref[...]
    yield
    cum = jnp.concatenate([_dot(tri, hi[rs]) + _dot(tri, mid[rs]) + _dot(tri, lo[rs])
                           for rs in (slice(i * c, (i + 1) * c) for i in range(rows // c))], axis=0)
    yield
    cum_end = jnp.concatenate(
        [jnp.broadcast_to(cum[(i + 1) * c - 1:(i + 1) * c, :], (c, GW)) for i in range(rows // c)], axis=0)
    e_cum = jnp.exp(cum)
    yield
    e_inv = jnp.exp(-cum)
    yield
    e_rem = jnp.exp(cum_end - cum)
    yield
    coef = _head_sum(r * k_mod * rk_ref[...], ones_ref[...])
    yield
    bf = lambda x: x.astype(BF16)
    at = bf(-kk * jnp.exp(cum - log_decay))
    yield
    rt, bt = bf(r * e_cum), bf(kb * e_inv)
    yield
    kt, b_end = bf(k_mod * e_inv), bf(kb * e_rem)
    yield
    k_end, e_end = bf(k_mod * e_rem), jnp.exp(cum_end)
    yield
    return at, rt, bt, kt, b_end, k_end, bf(v), e_end, g, coef * v * g


def _drain(gen):
    while True:
        try:
            next(gen)
        except StopIteration as stop:
            return stop.value


def _block_diag(y, first):
    zero = jnp.zeros_like(y)
    return jnp.concatenate([jnp.where(first, y, zero), jnp.where(first, zero, y)], axis=0)


def _pair_nn(x, y, first):
    return _dot(x, _block_diag(y, first))


def _pair_nt(x, y, first):
    return _dot_nt(x, _block_diag(y, first))


def _pair_tn(x, y, first):
    full = _dot_tn(x, y)
    return jnp.where(first, full[:HEAD_DIM], full[HEAD_DIM:])


def _rwkv_chunk_terms(probs, masks, tick):
    first, strict, incl, same_sub, eye = masks
    c = RWKV_CHUNK
    bf = lambda xs: [x.astype(BF16) for x in xs]

    def nn(xs, ys):
        out = [_pair_nn(x, y, first) for x, y in zip(xs, ys)]
        tick()
        return out

    ident = eye.astype(F32)
    at, rt, bt, kt, b_end, k_end, v, e_end = [list(x) for x in zip(*probs)]
    ar = [jnp.concatenate([a, r], axis=0) for a, r in zip(at, rt)]
    gb = [_pair_nt(x, y, first) for x, y in zip(ar, bt)]
    tick()
    gk = [_pair_nt(x, y, first) for x, y in zip(ar, kt)]
    tick()
    a_ab = [jnp.where(strict, g[:c], 0.0) for g in gb]
    a_rb = bf([jnp.where(incl, g[c:], 0.0) for g in gb])
    a_kr = bf([jnp.concatenate([jnp.where(strict, g[:c], 0.0), jnp.where(incl, g[c:], 0.0)], axis=0) for g in gk])
    ad = [jnp.where(same_sub, a, 0.0) for a in a_ab]
    an = bf([a - d for a, d in zip(a_ab, ad)])
    adb = bf(ad)
    p2 = nn(adb, adb)
    av = nn(a_kr, v)
    p2b = bf(p2)
    p4 = nn(p2b, p2b)
    x1 = nn(bf([ident + d for d in ad]), bf([ident + p for p in p2]))
    p4b = bf(p4)
    p8 = nn(p4b, p4b)
    x2 = nn(bf([ident + p for p in p4]), bf([ident + p for p in p8]))
    td = bf(nn(bf(x1), bf(x2)))
    m1 = nn(td, an)
    m1b = bf(m1)
    m2 = nn(m1b, m1b)
    x3 = nn(bf([ident + m for m in m1]), bf([ident + m for m in m2]))
    t_inv = bf(nn(bf(x3), td))
    a_hat = bf(nn(t_inv, at))
    u0 = bf(nn(t_inv, bf([w[:c] for w in av])))
    r_hat = [r.astype(F32) + x for r, x in zip(rt, nn(a_rb, a_hat))]
    o0 = [x + w[c:] for x, w in zip(nn(a_rb, u0), av)]
    p_mat = [jnp.where(eye, e, 0.0) + _pair_tn(h, b, first) for e, h, b in zip(e_end, a_hat, b_end)]
    z_mat = [_pair_tn(jnp.concatenate([u, w], axis=0), jnp.concatenate([b, k], axis=0), first)
             for u, w, b, k in zip(u0, v, b_end, k_end)]
    return list(zip(p_mat, z_mat, r_hat, o0))


def _rwkv_kernel(*refs, rows):
    (p0_ref, p_ref, prev_ref), par_refs, (lng_ref, lnb_ref, o_ref, state) = refs[:3], refs[3:14], refs[14:18]
    slots = refs[18:]
    c = RWKV_CHUNK
    j = pl.program_id(1)

    def stage(vals, slot):
        for ref, val in zip(slots, vals):
            ref[slot] = val

    @pl.when(j == 0)
    def _():
        state[...] = jnp.zeros_like(state)
        stage(_drain(_rwkv_token_terms(p0_ref, None, *par_refs, rows)), 0)

    nxt_gen = _rwkv_token_terms(p_ref, prev_ref, *par_refs, rows)
    nxt_vals = []

    def tick():
        if not nxt_vals:
            try:
                next(nxt_gen)
            except StopIteration as stop:
                nxt_vals.append(stop.value)

    cur = j % 2
    at, rt, bt, kt, b_end, k_end, v, e_end, g, bg = [ref.at[cur] for ref in slots]
    ri = lax.broadcasted_iota(jnp.int32, (c, LANES), 0)
    lane = lax.broadcasted_iota(jnp.int32, (c, LANES), 1)
    first = lane < HEAD_DIM
    ci = jnp.where(first, lane, lane - HEAD_DIM)
    masks = (first, ci < ri, ci <= ri, (ri // RWKV_SUB) == (ci // RWKV_SUB), ri == ci)

    def head_mean(x):
        lo = jnp.sum(jnp.where(first, x, 0.0), axis=-1, keepdims=True)
        hi = jnp.sum(jnp.where(first, 0.0, x), axis=-1, keepdims=True)
        return jnp.where(first, lo, hi) * (1.0 / HEAD_DIM)

    chunks = [slice(i * c, (i + 1) * c) for i in range(rows // c)]
    pair = lambda pr: slice(pr * LANES, (pr + 1) * LANES)
    probs = [tuple(x[rs, pair(pr)] for x in (at, rt, bt, kt, b_end, k_end, v))
             + (e_end[rs.start:rs.start + 1, pair(pr)],)
             for rs in chunks for pr in range(N_PAIRS)]
    terms = _rwkv_chunk_terms(probs, masks, tick)
    s_cur = [state[pr] for pr in range(N_PAIRS)]
    for i, rs in enumerate(chunks):
        for pr in range(N_PAIRS):
            p_mat, z_mat, r_hat, o0 = terms[i * N_PAIRS + pr]
            s_hi, s_lo = _split2(s_cur[pr])
            p_hi, p_lo = _split2(p_mat)
            o = _pair_nt(r_hat.astype(BF16), s_hi, first) + o0
            s_cur[pr] = (_pair_nn(s_hi, p_hi, first) + _pair_nn(s_hi, p_lo, first) + _pair_nn(s_lo, p_hi, first)
                         + z_mat)
            oc = o - head_mean(o)
            y = oc * lax.rsqrt(head_mean(oc * oc) + RWKV_LN_EPS) * lng_ref[pr:pr + 1, :] + lnb_ref[pr:pr + 1, :]
            o_ref[0, rs, pair(pr)] = (y * g[rs, pair(pr)] + bg[rs, pair(pr)]).astype(o_ref.dtype)
        tick()
    for pr in range(N_PAIRS):
        state[pr] = s_cur[pr]
    while not nxt_vals:
        tick()
    stage(nxt_vals[0], 1 - cur)


def _mixer_rwkv(proj3, mu, w0, w2, a0, a2, g2, k_k, k_a, r_k, lnx_g, lnx_b, rows=512):
    b, s, _ = proj3.shape
    rows = min(rows, s)
    pad = lambda w, n: jnp.concatenate([w, jnp.zeros((n - w.shape[0],) + w.shape[1:], w.dtype)], axis=0)
    head_of = np.arange(LANES) // HEAD_DIM
    ones_bd = jnp.asarray((head_of[:, None] == head_of[None, :]).astype(np.float32), dtype=BF16)
    tok = np.arange(RWKV_CHUNK)
    tri_bd = jnp.asarray((tok[None, :] <= tok[:, None]).astype(np.float32), dtype=BF16)
    vec = lambda a: a.reshape(1, -1)
    par = lambda a: a.reshape(N_PAIRS, LANES)
    full = lambda shape: pl.BlockSpec(shape, lambda i, j: (0,) * len(shape))
    col = COL_D // RWKV_W
    sub = rows // 16
    nt = s // rows
    nxt = lambda j: jnp.minimum(j + 1, nt - 1)
    slot = lambda dt: pltpu.VMEM((2, rows, GW), dt)
    return pl.pallas_call(
        functools.partial(_rwkv_kernel, rows=rows),
        out_shape=jax.ShapeDtypeStruct((b, s, GW), BF16),
        grid=(b, nt),
        in_specs=[pl.BlockSpec((1, rows, RWKV_W), lambda i, j: (i, 0, col)),
                  pl.BlockSpec((1, rows, RWKV_W), lambda i, j: (i, nxt(j), col)),
                  pl.BlockSpec((1, 16, RWKV_W), lambda i, j: (i, nxt(j) * sub - 1, col)),
                  full((1, RWKV_W)), full((1, GW)), full((LORA_PAD, GW)), full((1, GW)),
                  full((LORA_PAD, GW)), full((G_LORA, GW)), full((1, GW)), full((1, GW)), full((1, GW)),
                  full((LANES, LANES)), full((RWKV_CHUNK, RWKV_CHUNK)), full((N_PAIRS, LANES)),
                  full((N_PAIRS, LANES))],
        out_specs=pl.BlockSpec((1, rows, GW), lambda i, j: (i, j, 0)),
        scratch_shapes=[pltpu.VMEM((N_PAIRS, HEAD_DIM, LANES), F32)] + [slot(BF16)] * 7 + [slot(F32)] * 3,
        compiler_params=_cparams(("arbitrary", "arbitrary")),
        name="mixer_rwkv",
    )(proj3, proj3, proj3, vec(mu), vec(w0), pad(w2, LORA_PAD).astype(BF16), vec(a0), pad(a2, LORA_PAD).astype(BF16),
      g2.astype(BF16), vec(k_k), vec(k_a), vec(r_k), ones_bd, tri_bd, par(lnx_g), par(lnx_b))


def _outproj_kernel(ya_ref, yb_ref, yc_ref, yd_ref, g_ref, w_ref, x_ref, o_ref):
    acc = x_ref[...]
    for i, y_ref in enumerate((ya_ref, yb_ref, yc_ref, yd_ref)):
        yn = _rms(y_ref[...].astype(F32), g_ref[i:i + 1, :]).astype(BF16)
        acc = acc + _dot(yn, w_ref[i * GW:(i + 1) * GW, :])
    o_ref[...] = acc


def _outproj(ys, g, w_all, layer, x2d, tm=512):
    t, d = x2d.shape
    tm = min(tm, t)
    y_spec = pl.BlockSpec((tm, GW), lambda i: (i, 0))
    return pl.pallas_call(
        _outproj_kernel,
        out_shape=jax.ShapeDtypeStruct((t, d), F32),
        grid=(t // tm,),
        in_specs=[y_spec] * 4 + [pl.BlockSpec((4, GW), lambda i: (0, 0)),
                                 pl.BlockSpec((None, 4 * GW, d), lambda i: (layer, 0, 0)),
                                 pl.BlockSpec((tm, d), lambda i: (i, 0))],
        out_specs=pl.BlockSpec((tm, d), lambda i: (i, 0)),
        compiler_params=_cparams(("arbitrary",)),
        name="outproj",
    )(*[y.reshape(t, GW) for y in ys], g.reshape(4, GW), w_all, x2d)


def _ffn_kernel(x_ref, g_ref, wg_ref, wu_ref, wd_ref, gf_ref, o_ref, h_scr, *, final_norm):
    j = pl.program_id(1)

    @pl.when(j == 0)
    def _():
        x = x_ref[...]
        h_scr[...] = _rms(x, g_ref[...]).astype(BF16)
        o_ref[...] = x

    h = h_scr[...]
    gate = _dot(h, wg_ref[...])
    up = _dot(h, wu_ref[...])
    act = (gate * _sigmoid(gate) * up).astype(BF16)
    o_ref[...] += _dot(act, wd_ref[...])

    if final_norm:
        @pl.when(j == pl.num_programs(1) - 1)
        def _():
            o_ref[...] = _rms(o_ref[...], gf_ref[...])


def _ffn(x2d, g, wg, wu, wd, g_final, final_norm, tm=1024, tf=512):
    t, d = x2d.shape
    f = wg.shape[1]
    tm = min(tm, t)
    return pl.pallas_call(
        functools.partial(_ffn_kernel, final_norm=final_norm),
        out_shape=jax.ShapeDtypeStruct((t, d), F32),
        grid=(t // tm, f // tf),
        in_specs=[pl.BlockSpec((tm, d), lambda i, j: (i, 0)),
                  pl.BlockSpec((1, d), lambda i, j: (0, 0)),
                  pl.BlockSpec((d, tf), lambda i, j: (0, j)),
                  pl.BlockSpec((d, tf), lambda i, j: (0, j)),
                  pl.BlockSpec((tf, d), lambda i, j: (j, 0)),
                  pl.BlockSpec((1, d), lambda i, j: (0, 0))],
        out_specs=pl.BlockSpec((tm, d), lambda i, j: (i, 0)),
        scratch_shapes=[pltpu.VMEM((tm, d), BF16)],
        compiler_params=_cparams(("arbitrary", "arbitrary")),
        name="ffn",
    )(x2d, g.reshape(1, d), wg, wu, wd, g_final.reshape(1, d))


def _cast_kernel(x_ref, o_ref):
    o_ref[...] = x_ref[...].astype(o_ref.dtype)


def _to_bf16(w, rows=256):
    shape = w.shape
    w2 = w.reshape(-1, shape[-1])
    n, c = w2.shape
    out = pl.pallas_call(
        _cast_kernel,
        out_shape=jax.ShapeDtypeStruct((n, c), BF16),
        grid=(n // rows,),
        in_specs=[pl.BlockSpec((rows, c), lambda i: (i, 0))],
        out_specs=pl.BlockSpec((rows, c), lambda i: (i, 0)),
        compiler_params=_cparams(("arbitrary",)),
        name="cast_bf16",
    )(w2)
    return out.reshape(shape)


_RWKV_SRC = np.concatenate([[0], np.cumsum((GW, W_LORA, GW, GW, A_LORA, G_LORA))])
_RWKV_DST = (0, 3 * GW, GW, 2 * GW, 3 * GW + LORA_PAD, 3 * GW + 2 * LORA_PAD)


def _reorder_rwkv(a):
    out = jnp.zeros(a.shape[:-1] + (RWKV_W,), a.dtype)
    for i, dst in enumerate(_RWKV_DST):
        lo, hi = int(_RWKV_SRC[i]), int(_RWKV_SRC[i + 1])
        out = out.at[..., dst:dst + hi - lo].set(a[..., lo:hi])
    return out


def _prepare_w_in(w_in):
    wt = jnp.swapaxes(w_in, 1, 2)
    zeros = lambda n: jnp.zeros(wt.shape[:1] + (n, wt.shape[2]), BF16)
    rows = lambda lo, hi: wt[:, lo:hi].astype(BF16)
    piece = lambda i: rows(SRC_D + int(_RWKV_SRC[i]), SRC_D + int(_RWKV_SRC[i + 1]))
    c_q = (wt[:, SRC_C:SRC_C + GW] * (ATT_SCALE * LOG2E)).astype(BF16)
    return jnp.concatenate([rows(0, A_W),
                            piece(0), piece(2), piece(3), piece(1), zeros(LORA_PAD - W_LORA),
                            piece(4), zeros(LORA_PAD - A_LORA), piece(5),
                            rows(SRC_B, SRC_C), c_q, rows(SRC_C + GW, SRC_D)], axis=1)


def kernel(x, norm_mix_g, w_in, pos_bias, sgu_ln_g, sgu_w, sgu_b, rwkv_mu, rwkv_w0, rwkv_w2, rwkv_a0, rwkv_a2,
           rwkv_g2, rwkv_k_k, rwkv_k_a, rwkv_r_k, rwkv_lnx_g, rwkv_lnx_b, branch_norm_g, w_out, norm_ffn_g,
           w_gate, w_up, w_down, norm_final_g):
    b, s, d = x.shape
    depth = w_in.shape[0]
    assert s % (DIL_BLOCK * DIL_PATTERNS[-1][1]) == 0 and s % (2 * MOBA_BLOCK) == 0
    bias_a = _bias_tiles(pos_bias, _dil_bucket_tiles(), 0)
    bias_c = _bias_tiles(pos_bias, _moba_bucket_tiles(s // MOBA_BLOCK), N_HEADS)
    x2d = x.reshape(b * s, d)
    w_out_b = _to_bf16(w_out)
    w_in_b, mu_all = _prepare_w_in(w_in), _reorder_rwkv(rwkv_mu)
    for l in range(depth):
        mu_l = mu_all[l]
        proj_a, proj_bcd = _inproj(x2d, norm_mix_g[l], w_in_b, l)
        proj3 = proj_bcd.reshape(b, s, BCD_W)
        ya = _mixer_dilated(proj_a.reshape(b, s, A_W), bias_a)
        yb = _mixer_sgu(proj3, sgu_ln_g[l], sgu_w[l], sgu_b[l])
        yc, ffn_w = _mixer_moba(proj3, bias_c, (w_gate, w_up, w_down), l)
        yd = _mixer_rwkv(proj3, mu_l, rwkv_w0[l], rwkv_w2[l], rwkv_a0[l], rwkv_a2[l], rwkv_g2[l],
                         rwkv_k_k[l], rwkv_k_a[l], rwkv_r_k[l], rwkv_lnx_g[l], rwkv_lnx_b[l])
        x2d = _outproj((ya, yb, yc, yd), branch_norm_g[l], w_out_b, l, x2d)
        x2d = _ffn(x2d, norm_ffn_g[l], *ffn_w, norm_final_g,
                   final_norm=(l == depth - 1))
    return x2d.reshape(b, s, d)
```

```python
import functools
import math

import jax
import jax.numpy as jnp
import numpy as np
from jax import lax
from jax.experimental import pallas as pl
from jax.experimental.pallas import tpu as pltpu

F32 = jnp.float32
BF16 = jnp.bfloat16

HEAD_DIM = 64
N_HEADS = 8
GW = N_HEADS * HEAD_DIM
LANES = 128
N_PAIRS = GW // LANES
DIL_PATTERNS = ((128, 1), (512, 4), (2048, 16))
DIL_BLOCK = 128
SGU_CHUNK = 128
SGU_LN_EPS = 1e-5
MOBA_BLOCK = 256
MOBA_TOPK = 3
W_LORA = 96
A_LORA = 96
G_LORA = 256
LORA_PAD = 128
RWKV_LN_EPS = 64e-5
RWKV_CHUNK = 64
RWKV_SUB = 16
NUM_BUCKETS = 32
MAX_DISTANCE = 2048
NORM_EPS = 1e-6
NEG_INF = -1e30
ATT_SCALE = HEAD_DIM ** -0.5
LOG2E = math.log2(math.e)

RWKV_W = 3 * GW + 2 * LORA_PAD + G_LORA
A_W = 3 * GW
SRC_B = A_W
SRC_C = SRC_B + 2 * GW
SRC_D = SRC_C + 3 * GW
COL_D = 0
COL_B = RWKV_W
COL_C = COL_B + 2 * GW
BCD_W = COL_C + 3 * GW
D_PROJ = A_W + BCD_W

VMEM_LIMIT = 56 * 1024 * 1024

HIGHEST = lax.Precision.HIGHEST


def _cparams(sem):
    return pltpu.CompilerParams(dimension_semantics=sem, vmem_limit_bytes=VMEM_LIMIT)


def _dot(a, b, precision=None):
    return lax.dot_general(a, b, (((1,), (0,)), ((), ())), precision=precision,
                           preferred_element_type=F32)


def _dot_nt(a, b, precision=None):
    return lax.dot_general(a, b, (((1,), (1,)), ((), ())), precision=precision,
                           preferred_element_type=F32)


def _dot_tn(a, b, precision=None):
    return lax.dot_general(a, b, (((0,), (0,)), ((), ())), precision=precision,
                           preferred_element_type=F32)


def _rms(x, g):
    return x * lax.rsqrt(jnp.mean(x * x, axis=-1, keepdims=True) + NORM_EPS) * g


def _t5_bucket_np(dist):
    dist = np.maximum(dist, 0)
    max_exact = NUM_BUCKETS // 2
    d = np.maximum(dist, 1).astype(np.float32)
    large = max_exact + (np.log(d / np.float32(max_exact)) / np.float32(math.log(MAX_DISTANCE / max_exact))
                         * np.float32(NUM_BUCKETS - max_exact)).astype(np.int32)
    large = np.minimum(large, NUM_BUCKETS - 1)
    return np.where(dist < max_exact, dist, large).astype(np.int32)


def _dil_bucket_tiles():
    qa = np.arange(DIL_BLOCK)[:, None]
    kj = np.arange(DIL_BLOCK)[None, :]
    tiles = []
    for _, dil in DIL_PATTERNS:
        tiles.append(np.where(kj <= qa, _t5_bucket_np((qa - kj) * dil), -1))
        tiles.append(np.where(kj >= qa, _t5_bucket_np((qa + DIL_BLOCK - kj) * dil), -1))
    return np.stack(tiles).astype(np.int32)


def _moba_bucket_tiles(nblk):
    ki = np.arange(MOBA_BLOCK)[:, None]
    qi = np.arange(MOBA_BLOCK)[None, :]
    return np.stack([_t5_bucket_np(db * MOBA_BLOCK + qi - ki) for db in range(nblk)])


def _bias_tile_kernel(tbl_ref, idx_ref, o_ref, *, head_offset, buckets):
    h = pl.program_id(0) + head_offset
    for t, present in enumerate(buckets):
        idx = idx_ref[t]
        acc = jnp.full(idx.shape, NEG_INF, F32)
        for b in present:
            if b >= 0:
                acc = jnp.where(idx == b, tbl_ref[b, h] * LOG2E, acc)
        o_ref[0, t] = acc


def _bias_tiles(pos_bias, idx_np, head_offset):
    nt, r, c = idx_np.shape
    buckets = tuple(tuple(int(b) for b in np.unique(idx_np[t])) for t in range(nt))
    return pl.pallas_call(
        functools.partial(_bias_tile_kernel, head_offset=head_offset, buckets=buckets),
        out_shape=jax.ShapeDtypeStruct((N_HEADS, nt, r, c), F32),
        grid=(N_HEADS,),
        in_specs=[pl.BlockSpec(memory_space=pltpu.SMEM),
                  pl.BlockSpec((nt, r, c), lambda h: (0, 0, 0))],
        out_specs=pl.BlockSpec((1, nt, r, c), lambda h: (h, 0, 0, 0)),
        compiler_params=_cparams(("arbitrary",)),
        name="bias_tiles",
    )(pos_bias, jnp.asarray(idx_np))


def _inproj_kernel(x_ref, g_ref, w_ref, oa_ref, ob_ref, h_scr):
    j = pl.program_id(1)

    @pl.when(j == 0)
    def _():
        h_scr[...] = _rms(x_ref[...], g_ref[...]).astype(BF16)
        oa_ref[...] = _dot_nt(h_scr[...], w_ref[...])

    @pl.when(j > 0)
    def _():
        ob_ref[...] = _dot_nt(h_scr[...], w_ref[...]).astype(BF16)


def _inproj(x2d, g, wt_all, layer, tm=1024):
    t, d = x2d.shape
    tn = A_W
    tm = min(tm, t)
    return pl.pallas_call(
        _inproj_kernel,
        out_shape=[jax.ShapeDtypeStruct((t, A_W), F32), jax.ShapeDtypeStruct((t, BCD_W), BF16)],
        grid=(t // tm, D_PROJ // tn),
        in_specs=[pl.BlockSpec((tm, d), lambda i, j: (i, 0)),
                  pl.BlockSpec((1, d), lambda i, j: (0, 0)),
                  pl.BlockSpec((None, tn, d), lambda i, j: (layer, j, 0))],
        out_specs=[pl.BlockSpec((tm, tn), lambda i, j: (i, 0)),
                   pl.BlockSpec((tm, tn), lambda i, j: (i, jnp.maximum(j - 1, 0)))],
        scratch_shapes=[pltpu.VMEM((tm, d), BF16)],
        compiler_params=_cparams(("arbitrary", "arbitrary")),
        name="inproj",
    )(x2d, g.reshape(1, d), wt_all)


def _dilated_kernel(q_ref, k_ref, v_ref, bias_ref, w_ref, o_ref, wb_ref, m0, m1, l0, l1, acc, *, seq):
    wb_ref[...] = w_ref[...].astype(wb_ref.dtype)
    c = DIL_BLOCK
    lane = lax.broadcasted_iota(jnp.int32, (c, LANES), 1)
    head0 = lane < HEAD_DIM

    stats = ((m0, l0), (m1, l1))
    head_lanes = (head0, ~head0)

    def logits(pairs, pi, wide, first):
        bias = [jnp.concatenate([bias_ref[h, 2 * pi + 1], bias_ref[h, 2 * pi]], axis=1) if wide
                else bias_ref[h, 2 * pi] for h in range(2)]
        q = [q_ref[0, qsl, :] * (ATT_SCALE * LOG2E) for qsl, _ in pairs]
        kb = [k_ref[0, ksl, :].astype(BF16) for _, ksl in pairs]
        idx = [(h, i) for i in range(len(pairs)) for h in range(2)]
        s = {(h, i): _dot_nt(jnp.where(head_lanes[h], q[i], 0.0).astype(BF16), kb[i]) for h, i in idx}
        s = {hi: s[hi] + bias[hi[0]] for hi in idx}
        return pairs, wide, first, idx, s

    def softmax_update(pairs, wide, first, idx, s):
        rep = (lambda x: jnp.concatenate([x, x], axis=1)) if wide else (lambda x: x)
        ones = jnp.ones(((2 if wide else 1) * c, LANES), BF16)
        vb = [jnp.concatenate([v_ref[0, ksl, :].astype(BF16), ones], axis=1) for _, ksl in pairs]
        m_new = {hi: jnp.broadcast_to(jnp.max(s[hi], axis=1, keepdims=True), (c, LANES)) for hi in idx}
        if not first:
            m_old = {(h, i): stats[h][0][pairs[i][0], :] for h, i in idx}
            m_new = {hi: jnp.maximum(m_old[hi], m_new[hi]) for hi in idx}
            alpha = {hi: jnp.exp2(m_old[hi] - m_new[hi]) for hi in idx}
        p = {hi: jnp.exp2(s[hi] - rep(m_new[hi])).astype(BF16) for hi in idx}
        o = {(h, i): _dot(p[h, i], vb[i]) for h, i in idx}
        for h, i in idx:
            m_ref, l_ref = stats[h]
            qsl = pairs[i][0]
            l_new = o[h, i][:, LANES:]
            l_ref[qsl, :] = l_new if first else alpha[h, i] * l_ref[qsl, :] + l_new
            m_ref[qsl, :] = m_new[h, i]
        for i, (qsl, _) in enumerate(pairs):
            o_new = jnp.where(head0, o[0, i][:, :LANES], o[1, i][:, :LANES])
            acc[qsl, :] = o_new if first else acc[qsl, :] * jnp.where(head0, alpha[0, i], alpha[1, i]) + o_new

    def group_size(n, cap):
        return max(g for g in range(1, cap + 1) if n % g == 0)

    order = sorted(range(len(DIL_PATTERNS)), key=lambda i: -DIL_PATTERNS[i][1])
    groups = []
    for pi in order:
        dil = DIL_PATTERNS[pi][1]
        first_pass = pi == order[0]
        nb = seq // dil // c

        def sl(r, n, blocks, dil=dil):
            start = r + n * (c * dil)
            return pl.ds(start, blocks * c) if dil == 1 else pl.ds(start, blocks * c, stride=dil)

        heads = [(sl(r, 0, 1),) * 2 for r in range(dil)]
        body = [(sl(r, n, 1), sl(r, n - 1, 2)) for r in range(dil) for n in range(1, nb)]
        for tiles, wide, cap in ((heads, False, 2), (body, True, 1)):
            g = group_size(len(tiles), cap) if tiles else 1
            groups += [(tiles[i:i + g], pi, wide, first_pass) for i in range(0, len(tiles), g)]

    pending = logits(*groups[0])
    for grp in groups[1:]:
        nxt = logits(*grp)
        softmax_update(*pending)
        pending = nxt
    softmax_update(*pending)

    o_ref[0] = (acc[...] / jnp.where(head0[:1], l0[...], l1[...])).astype(o_ref.dtype)


def _mixer_dilated(proj_a, bias_a, w_out, layer):
    b, s, _ = proj_a.shape
    steps = N_PAIRS * b
    rows_w, cols_w = w_out.shape[1:]
    assert rows_w % (16 * steps) == 0
    blk = lambda off: pl.BlockSpec((1, s, LANES), lambda p, i, off=off: (i, 0, off + p))
    return pl.pallas_call(
        functools.partial(_dilated_kernel, seq=s),
        out_shape=[jax.ShapeDtypeStruct((b, s, GW), BF16), jax.ShapeDtypeStruct((rows_w, cols_w), BF16)],
        grid=(N_PAIRS, b),
        in_specs=[blk(0), blk(GW // LANES), blk(2 * GW // LANES),
                  pl.BlockSpec((2, 2 * len(DIL_PATTERNS), DIL_BLOCK, DIL_BLOCK),
                               lambda p, i: (p, 0, 0, 0)),
                  pl.BlockSpec((None, rows_w // steps, cols_w), lambda p, i: (layer, p * b + i, 0))],
        out_specs=[pl.BlockSpec((1, s, LANES), lambda p, i: (i, 0, p)),
                   pl.BlockSpec((rows_w // steps, cols_w), lambda p, i: (p * b + i, 0))],
        scratch_shapes=[pltpu.VMEM((s, LANES), F32)] * 5,
        compiler_params=_cparams(("arbitrary", "arbitrary")),
        name="mixer_dilated",
    )(proj_a, proj_a, proj_a, bias_a, w_out)


def _gelu_tanh(x):
    return 0.5 * x * (1.0 + jnp.tanh(math.sqrt(2.0 / math.pi) * (x + 0.044715 * (x * x * x))))


def _sgu_kernel(u_ref, v_ref, lng_ref, w_ref, bias_ref, o_ref, *, rows):
    t = SGU_CHUNK
    u = _gelu_tanh(u_ref[0].astype(F32))
    v = _gelu_tanh(v_ref[0].astype(F32))
    mu = jnp.mean(v, axis=-1, keepdims=True)
    vc = v - mu
    var = jnp.mean(vc * vc, axis=-1, keepdims=True)
    vn = (vc * lax.rsqrt(var + SGU_LN_EPS) * lng_ref[...]).astype(BF16)
    r2 = lax.broadcasted_iota(jnp.int32, (2 * t, t), 0)
    c2 = lax.broadcasted_iota(jnp.int32, (2 * t, t), 1)
    causal = c2 <= jnp.where(r2 >= t, r2 - t, r2)
    first_group = lax.broadcasted_iota(jnp.int32, (t, LANES), 1) < HEAD_DIM
    for p in range(N_PAIRS):
        wp = jnp.where(causal, w_ref[p], 0.0).astype(BF16)
        for ci in range(rows // t):
            rs = slice(ci * t, (ci + 1) * t)
            cs = slice(p * LANES, (p + 1) * LANES)
            res = _dot(wp, vn[rs, cs])
            mixed = jnp.where(first_group, res[:t], res[t:]) + bias_ref[:, cs]
            o_ref[0, rs, cs] = (u[rs, cs] * mixed).astype(o_ref.dtype)


def _mixer_sgu(proj3, ln_g, w_s, b_s, rows=512):
    b, s, _ = proj3.shape
    rows = min(rows, s)
    t = SGU_CHUNK
    bias_full = jnp.repeat(b_s.T, HEAD_DIM, axis=1)
    w_pairs = w_s.reshape(N_PAIRS, 2 * t, t)
    return pl.pallas_call(
        functools.partial(_sgu_kernel, rows=rows),
        out_shape=jax.ShapeDtypeStruct((b, s, GW), BF16),
        grid=(b, s // rows),
        in_specs=[pl.BlockSpec((1, rows, GW), lambda i, j: (i, j, COL_B // GW)),
                  pl.BlockSpec((1, rows, GW), lambda i, j: (i, j, COL_B // GW + 1)),
                  pl.BlockSpec((1, GW), lambda i, j: (0, 0)),
                  pl.BlockSpec((N_PAIRS, 2 * t, t), lambda i, j: (0, 0, 0)),
                  pl.BlockSpec((t, GW), lambda i, j: (0, 0))],
        out_specs=pl.BlockSpec((1, rows, GW), lambda i, j: (i, j, 0)),
        compiler_params=_cparams(("arbitrary", "arbitrary")),
        name="mixer_sgu",
    )(proj3, proj3, ln_g.reshape(1, GW), w_pairs, bias_full)


def _moba_kernel(q_ref, k_ref, v_ref, bias_ref, wg_ref, wu_ref, wd_ref, o_ref, wgb_ref, wub_ref, wdb_ref,
                 kh_scr, vt_scr, ot_scr, *, seq):
    for src, dst in ((wg_ref, wgb_ref), (wu_ref, wub_ref), (wd_ref, wdb_ref)):
        dst[...] = src[...].astype(dst.dtype)

    ones_rows = 16
    bs = MOBA_BLOCK
    nblk = seq // bs
    lane = lax.broadcasted_iota(jnp.int32, (1, LANES), 1)
    head_lanes = (lane < HEAD_DIM, lane >= HEAD_DIM)
    blk = lambda i: slice(i * bs, (i + 1) * bs)

    q_all = q_ref[0]
    kbar = jnp.concatenate(
        [jnp.mean(k_ref[0, blk(j), :].astype(F32), axis=0, keepdims=True) for j in range(nblk)], axis=0)
    for j in range(nblk):
        kj = k_ref[0, blk(j), :]
        for h in range(2):
            kh_scr[h, j] = jnp.where(head_lanes[h], kj, jnp.zeros_like(kj))
        vt = v_ref[0, blk(j), :].astype(F32).T.astype(BF16)
        for h in range(2):
            vt_scr[j, h] = jnp.concatenate(
                [vt[h * HEAD_DIM:(h + 1) * HEAD_DIM], jnp.ones((ones_rows, bs), BF16)], axis=0)

    jrow = lax.broadcasted_iota(jnp.int32, (nblk, seq), 0)
    own = lax.broadcasted_iota(jnp.int32, (nblk, seq), 1) // bs
    krow = lax.broadcasted_iota(jnp.int32, (bs, bs), 0)
    qcol = lax.broadcasted_iota(jnp.int32, (bs, bs), 1)
    causal = krow <= qcol

    kb2 = jnp.concatenate([jnp.where(head_lanes[0], kbar, 0.0), jnp.where(head_lanes[1], kbar, 0.0)], axis=0)
    kb_hi, kb_lo = _split2(kb2)
    gates = _dot_nt(kb_hi, q_all) + _dot_nt(kb_lo, q_all)
    sel = []
    for h in range(2):
        gate = jnp.where(jrow < own, gates[h * nblk:(h + 1) * nblk], NEG_INF)
        rank = jnp.zeros((nblk, seq), jnp.int32)
        for j2 in range(nblk):
            gj = gate[j2:j2 + 1, :]
            ahead = (gj > gate) | ((gj == gate) & (j2 < jrow))
            rank = rank + ahead.astype(jnp.int32)
        sel.append((rank < MOBA_TOPK) & (jrow < own))

    def logits(grp):
        probs = [(h, qb) for qb in (grp, nblk - 1 - grp) for h in range(2)]
        tiles = [(h, qb, j) for h, qb in probs for j in range(qb + 1)]
        qh = {qb: q_ref[0, blk(qb), :] for _, qb in probs}
        s = {(h, qb, j): _dot_nt(kh_scr[h, j], qh[qb]) + bias_ref[h, qb - j] for h, qb, j in tiles}
        for h, qb in probs:
            s[h, qb, qb] = jnp.where(causal, s[h, qb, qb], NEG_INF)
        return probs, tiles, s

    def softmax_pv(probs, tiles, s):
        picked = {(h, qb, j): sel[h][j:j + 1, blk(qb)] for h, qb, j in tiles if j != qb}
        cmax = {t: jnp.max(s[t], axis=0, keepdims=True) for t in tiles}
        m = {}
        for h, qb in probs:
            m[h, qb] = functools.reduce(
                jnp.maximum, [cmax[h, qb, qb]] + [jnp.where(picked[h, qb, j], cmax[h, qb, j], NEG_INF)
                                                  for j in range(qb)])
        shift = {t: m[t[0], t[1]] if t[2] == t[1] else jnp.where(picked[t], m[t[0], t[1]], -NEG_INF)
                 for t in tiles}
        p = {t: jnp.exp2(s[t] - shift[t]).astype(BF16) for t in tiles}
        for h, qb in probs:
            acc = sum(_dot(vt_scr[j, h], p[h, qb, j]) for j in range(qb + 1))
            ot_scr[qb, h * HEAD_DIM:(h + 1) * HEAD_DIM, :] = acc[:HEAD_DIM] / acc[HEAD_DIM:HEAD_DIM + 1]

    pending = logits(0)
    for grp in range(1, nblk // 2):
        nxt = logits(grp)
        softmax_pv(*pending)
        pending = nxt
    softmax_pv(*pending)

    for qb in range(nblk):
        o_ref[0, blk(qb), :] = ot_scr[qb].T.astype(o_ref.dtype)


def _mixer_moba(proj3, bias_c, ffn_w, layer):
    b, s, _ = proj3.shape
    nblk = s // MOBA_BLOCK
    steps = N_PAIRS * b
    blk = lambda off: pl.BlockSpec((1, s, LANES), lambda p, i, off=off: (i, 0, off + p))
    w_shapes = [w.shape[1:] for w in ffn_w]
    assert all(r % (16 * steps) == 0 for r, _ in w_shapes)
    w_in_specs = [pl.BlockSpec((None, r // steps, c), lambda p, i: (layer, p * b + i, 0)) for r, c in w_shapes]
    w_out_specs = [pl.BlockSpec((r // steps, c), lambda p, i: (p * b + i, 0)) for r, c in w_shapes]
    out = pl.pallas_call(
        functools.partial(_moba_kernel, seq=s),
        out_shape=[jax.ShapeDtypeStruct((b, s, GW), BF16)] + [jax.ShapeDtypeStruct(sh, BF16) for sh in w_shapes],
        grid=(N_PAIRS, b),
        in_specs=[blk(COL_C // LANES), blk((COL_C + GW) // LANES), blk((COL_C + 2 * GW) // LANES),
                  pl.BlockSpec((2, nblk, MOBA_BLOCK, MOBA_BLOCK), lambda p, i: (p, 0, 0, 0))] + w_in_specs,
        out_specs=[pl.BlockSpec((1, s, LANES), lambda p, i: (i, 0, p))] + w_out_specs,
        scratch_shapes=[pltpu.VMEM((2, nblk, MOBA_BLOCK, LANES), BF16),
                        pltpu.VMEM((nblk, 2, HEAD_DIM + 16, MOBA_BLOCK), BF16),
                        pltpu.VMEM((nblk, LANES, MOBA_BLOCK), F32)],
        compiler_params=_cparams(("arbitrary", "arbitrary")),
        name="mixer_moba",
    )(proj3, proj3, proj3, bias_c, *ffn_w)
    return out[0], tuple(out[1:])


def _sigmoid(x):
    return 1.0 / (1.0 + jnp.exp(-x))


def _split2(x):
    hi = x.astype(BF16)
    return hi, (x - hi.astype(F32)).astype(BF16)


def _head_sum(x, ones_bd):
    hi, lo = _split2(x)
    cols = [slice(i * LANES, (i + 1) * LANES) for i in range(x.shape[1] // LANES)]
    return jnp.concatenate([_dot(hi[:, cs], ones_bd) + _dot(lo[:, cs], ones_bd) for cs in cols], axis=1)


def _rwkv_token_terms(p_ref, prev_ref, mu_ref, w0_ref, w2_ref, a0_ref, a2_ref, g2_ref, kk_ref, ka_ref, rk_ref,
                      ones_ref, tri_ref, rows):
    c = RWKV_CHUNK
    p = p_ref[0].astype(F32)
    prev_row = 0.0 if prev_ref is None else prev_ref[0, 15:16, :].astype(F32)
    first_row = lax.broadcasted_iota(jnp.int32, (rows, 1), 0) == 0
    y_prev = jnp.where(first_row, prev_row, pltpu.roll(p, 1, axis=0))
    xs = p + (y_prev - p) * mu_ref[...]
    yield
    r = xs[:, 0:GW]
    k = xs[:, GW:2 * GW]
    v = xs[:, 2 * GW:3 * GW]
    wd = xs[:, 3 * GW:3 * GW + LORA_PAD]
    ad = xs[:, 3 * GW + LORA_PAD:3 * GW + 2 * LORA_PAD]
    gd = xs[:, 3 * GW + 2 * LORA_PAD:]
    nz = -(w0_ref[...] + _dot(jnp.tanh(wd).astype(BF16), w2_ref[...]))
    yield
    softplus = jnp.maximum(nz, 0.0) + jnp.log(1.0 + jnp.exp(-jnp.abs(nz)))
    log_decay = -jnp.exp(-softplus - 0.5)
    yield
    a_sig = _sigmoid(a0_ref[...] + _dot(ad.astype(BF16), a2_ref[...]))
    yield
    g = _dot(_sigmoid(gd).astype(BF16), g2_ref[...])
    yield
    kk = k * kk_ref[...]
    ss = _head_sum(kk * kk, ones_ref[...])
    yield
    kk = kk / jnp.maximum(jnp.sqrt(ss), 1e-12)
    yield
    k_mod = k * (1.0 + (a_sig - 1.0) * ka_ref[...])
    kb = kk * a_sig
    yield
    hi = log_decay.astype(BF16)
    rem = log_decay - hi.astype(F32)
    mid = rem.astype(BF16)
    lo = (rem - mid.astype(F32)).astype(BF16)
    tri = tri_ref[...]
    yield
    cum = jnp.concatenate([_dot(tri, hi[rs]) + _dot(tri, mid[rs]) + _dot(tri, lo[rs])
                           for rs in (slice(i * c, (i + 1) * c) for i in range(rows // c))], axis=0)
    yield
    cum_end = jnp.concatenate(
        [jnp.broadcast_to(cum[(i + 1) * c - 1:(i + 1) * c, :], (c, GW)) for i in range(rows // c)], axis=0)
    e_cum = jnp.exp(cum)
    yield
    e_inv = jnp.exp(-cum)
    yield
    e_rem = jnp.exp(cum_end - cum)
    yield
    coef = _head_sum(r * k_mod * rk_ref[...], ones_ref[...])
    yield
    bf = lambda x: x.astype(BF16)
    at = bf(-kk * jnp.exp(cum - log_decay))
    yield
    rt, bt = bf(r * e_cum), bf(kb * e_inv)
    yield
    kt, b_end = bf(k_mod * e_inv), bf(kb * e_rem)
    yield
    k_end, e_end = bf(k_mod * e_rem), jnp.exp(cum_end)
    yield
    return at, rt, bt, kt, b_end, k_end, bf(v), e_end, g, coef * v * g


def _drain(gen):
    while True:
        try:
            next(gen)
        except StopIteration as stop:
            return stop.value


def _block_diag(y, first):
    zero = jnp.zeros_like(y)
    return jnp.concatenate([jnp.where(first, y, zero), jnp.where(first, zero, y)], axis=0)


def _pair_nn(x, y, first):
    return _dot(x, _block_diag(y, first))


def _pair_nt(x, y, first):
    return _dot_nt(x, _block_diag(y, first))


def _pair_tn(x, y, first):
    full = _dot_tn(x, y)
    return jnp.where(first, full[:HEAD_DIM], full[HEAD_DIM:])


def _rwkv_chunk_terms(probs, masks, tick):
    first, strict, incl, same_sub, eye = masks
    c = RWKV_CHUNK
    bf = lambda xs: [x.astype(BF16) for x in xs]

    def nn(xs, ys):
        out = [_pair_nn(x, y, first) for x, y in zip(xs, ys)]
        tick()
        return out

    ident = eye.astype(F32)
    at, rt, bt, kt, b_end, k_end, v, e_end = [list(x) for x in zip(*probs)]
    ar = [jnp.concatenate([a, r], axis=0) for a, r in zip(at, rt)]
    gb = [_pair_nt(x, y, first) for x, y in zip(ar, bt)]
    tick()
    gk = [_pair_nt(x, y, first) for x, y in zip(ar, kt)]
    tick()
    a_ab = [jnp.where(strict, g[:c], 0.0) for g in gb]
    a_rb = bf([jnp.where(incl, g[c:], 0.0) for g in gb])
    a_kr = bf([jnp.concatenate([jnp.where(strict, g[:c], 0.0), jnp.where(incl, g[c:], 0.0)], axis=0) for g in gk])
    ad = [jnp.where(same_sub, a, 0.0) for a in a_ab]
    an = bf([a - d for a, d in zip(a_ab, ad)])
    adb = bf(ad)
    p2 = nn(adb, adb)
    av = nn(a_kr, v)
    p2b = bf(p2)
    p4 = nn(p2b, p2b)
    x1 = nn(bf([ident + d for d in ad]), bf([ident + p for p in p2]))
    p4b = bf(p4)
    p8 = nn(p4b, p4b)
    x2 = nn(bf([ident + p for p in p4]), bf([ident + p for p in p8]))
    td = bf(nn(bf(x1), bf(x2)))
    m1 = nn(td, an)
    m1b = bf(m1)
    m2 = nn(m1b, m1b)
    x3 = nn(bf([ident + m for m in m1]), bf([ident + m for m in m2]))
    t_inv = bf(nn(bf(x3), td))
    a_hat = bf(nn(t_inv, at))
    u0 = bf(nn(t_inv, bf([w[:c] for w in av])))
    r_hat = [r.astype(F32) + x for r, x in zip(rt, nn(a_rb, a_hat))]
    o0 = [x + w[c:] for x, w in zip(nn(a_rb, u0), av)]
    p_mat = [jnp.where(eye, e, 0.0) + _pair_tn(h, b, first) for e, h, b in zip(e_end, a_hat, b_end)]
    z_mat = [_pair_tn(jnp.concatenate([u, w], axis=0), jnp.concatenate([b, k], axis=0), first)
             for u, w, b, k in zip(u0, v, b_end, k_end)]
    return list(zip(p_mat, z_mat, r_hat, o0))


def _rwkv_kernel(*refs, rows):
    (p0_ref, p_ref, prev_ref), par_refs, (lng_ref, lnb_ref, o_ref, state) = refs[:3], refs[3:14], refs[14:18]
    slots = refs[18:]
    c = RWKV_CHUNK
    j = pl.program_id(1)

    def stage(vals, slot):
        for ref, val in zip(slots, vals):
            ref[slot] = val

    @pl.when(j == 0)
    def _():
        state[...] = jnp.zeros_like(state)
        stage(_drain(_rwkv_token_terms(p0_ref, None, *par_refs, rows)), 0)

    nxt_gen = _rwkv_token_terms(p_ref, prev_ref, *par_refs, rows)
    nxt_vals = []

    def tick():
        if not nxt_vals:
            try:
                next(nxt_gen)
            except StopIteration as stop:
                nxt_vals.append(stop.value)

    cur = j % 2
    at, rt, bt, kt, b_end, k_end, v, e_end, g, bg = [ref.at[cur] for ref in slots]
    ri = lax.broadcasted_iota(jnp.int32, (c, LANES), 0)
    lane = lax.broadcasted_iota(jnp.int32, (c, LANES), 1)
    first = lane < HEAD_DIM
    ci = jnp.where(first, lane, lane - HEAD_DIM)
    masks = (first, ci < ri, ci <= ri, (ri // RWKV_SUB) == (ci // RWKV_SUB), ri == ci)

    def head_mean(x):
        lo = jnp.sum(jnp.where(first, x, 0.0), axis=-1, keepdims=True)
        hi = jnp.sum(jnp.where(first, 0.0, x), axis=-1, keepdims=True)
        return jnp.where(first, lo, hi) * (1.0 / HEAD_DIM)

    chunks = [slice(i * c, (i + 1) * c) for i in range(rows // c)]
    pair = lambda pr: slice(pr * LANES, (pr + 1) * LANES)
    probs = [tuple(x[rs, pair(pr)] for x in (at, rt, bt, kt, b_end, k_end, v))
             + (e_end[rs.start:rs.start + 1, pair(pr)],)
             for rs in chunks for pr in range(N_PAIRS)]
    terms = _rwkv_chunk_terms(probs, masks, tick)
    s_cur = [state[pr] for pr in range(N_PAIRS)]
    for i, rs in enumerate(chunks):
        for pr in range(N_PAIRS):
            p_mat, z_mat, r_hat, o0 = terms[i * N_PAIRS + pr]
            s_hi, s_lo = _split2(s_cur[pr])
            p_hi, p_lo = _split2(p_mat)
            o = _pair_nt(r_hat.astype(BF16), s_hi, first) + o0
            s_cur[pr] = (_pair_nn(s_hi, p_hi, first) + _pair_nn(s_hi, p_lo, first) + _pair_nn(s_lo, p_hi, first)
                         + z_mat)
            oc = o - head_mean(o)
            y = oc * lax.rsqrt(head_mean(oc * oc) + RWKV_LN_EPS) * lng_ref[pr:pr + 1, :] + lnb_ref[pr:pr + 1, :]
            o_ref[0, rs, pair(pr)] = (y * g[rs, pair(pr)] + bg[rs, pair(pr)]).astype(o_ref.dtype)
        tick()
    for pr in range(N_PAIRS):
        state[pr] = s_cur[pr]
    while not nxt_vals:
        tick()
    stage(nxt_vals[0], 1 - cur)


def _mixer_rwkv(proj3, mu, w0, w2, a0, a2, g2, k_k, k_a, r_k, lnx_g, lnx_b, rows=512):
    b, s, _ = proj3.shape
    rows = min(rows, s)
    pad = lambda w, n: jnp.concatenate([w, jnp.zeros((n - w.shape[0],) + w.shape[1:], w.dtype)], axis=0)
    head_of = np.arange(LANES) // HEAD_DIM
    ones_bd = jnp.asarray((head_of[:, None] == head_of[None, :]).astype(np.float32), dtype=BF16)
    tok = np.arange(RWKV_CHUNK)
    tri_bd = jnp.asarray((tok[None, :] <= tok[:, None]).astype(np.float32), dtype=BF16)
    vec = lambda a: a.reshape(1, -1)
    par = lambda a: a.reshape(N_PAIRS, LANES)
    full = lambda shape: pl.BlockSpec(shape, lambda i, j: (0,) * len(shape))
    col = COL_D // RWKV_W
    sub = rows // 16
    nt = s // rows
    nxt = lambda j: jnp.minimum(j + 1, nt - 1)
    slot = lambda dt: pltpu.VMEM((2, rows, GW), dt)
    return pl.pallas_call(
        functools.partial(_rwkv_kernel, rows=rows),
        out_shape=jax.ShapeDtypeStruct((b, s, GW), BF16),
        grid=(b, nt),
        in_specs=[pl.BlockSpec((1, rows, RWKV_W), lambda i, j: (i, 0, col)),
                  pl.BlockSpec((1, rows, RWKV_W), lambda i, j: (i, nxt(j), col)),
                  pl.BlockSpec((1, 16, RWKV_W), lambda i, j: (i, nxt(j) * sub - 1, col)),
                  full((1, RWKV_W)), full((1, GW)), full((LORA_PAD, GW)), full((1, GW)),
                  full((LORA_PAD, GW)), full((G_LORA, GW)), full((1, GW)), full((1, GW)), full((1, GW)),
                  full((LANES, LANES)), full((RWKV_CHUNK, RWKV_CHUNK)), full((N_PAIRS, LANES)),
                  full((N_PAIRS, LANES))],
        out_specs=pl.BlockSpec((1, rows, GW), lambda i, j: (i, j, 0)),
        scratch_shapes=[pltpu.VMEM((N_PAIRS, HEAD_DIM, LANES), F32)] + [slot(BF16)] * 7 + [slot(F32)] * 3,
        compiler_params=_cparams(("arbitrary", "arbitrary")),
        name="mixer_rwkv",
    )(proj3, proj3, proj3, vec(mu), vec(w0), pad(w2, LORA_PAD).astype(BF16), vec(a0), pad(a2, LORA_PAD).astype(BF16),
      g2.astype(BF16), vec(k_k), vec(k_a), vec(r_k), ones_bd, tri_bd, par(lnx_g), par(lnx_b))


def _outproj_kernel(ya_ref, yb_ref, yc_ref, yd_ref, g_ref, w_ref, x_ref, o_ref):
    acc = x_ref[...]
    for i, y_ref in enumerate((ya_ref, yb_ref, yc_ref, yd_ref)):
        yn = _rms(y_ref[...].astype(F32), g_ref[i:i + 1, :]).astype(BF16)
        acc = acc + _dot(yn, w_ref[i * GW:(i + 1) * GW, :])
    o_ref[...] = acc


def _outproj(ys, g, w, x2d, tm=512):
    t, d = x2d.shape
    tm = min(tm, t)
    y_spec = pl.BlockSpec((tm, GW), lambda i: (i, 0))
    return pl.pallas_call(
        _outproj_kernel,
        out_shape=jax.ShapeDtypeStruct((t, d), F32),
        grid=(t // tm,),
        in_specs=[y_spec] * 4 + [pl.BlockSpec((4, GW), lambda i: (0, 0)),
                                 pl.BlockSpec((4 * GW, d), lambda i: (0, 0)),
                                 pl.BlockSpec((tm, d), lambda i: (i, 0))],
        out_specs=pl.BlockSpec((tm, d), lambda i: (i, 0)),
        compiler_params=_cparams(("arbitrary",)),
        name="outproj",
    )(*[y.reshape(t, GW) for y in ys], g.reshape(4, GW), w, x2d)


def _ffn_kernel(x_ref, g_ref, wg_ref, wu_ref, wd_ref, gf_ref, o_ref, h_scr, *, final_norm):
    j = pl.program_id(1)

    @pl.when(j == 0)
    def _():
        x = x_ref[...]
        h_scr[...] = _rms(x, g_ref[...]).astype(BF16)
        o_ref[...] = x

    h = h_scr[...]
    gate = _dot(h, wg_ref[...])
    up = _dot(h, wu_ref[...])
    act = (gate * _sigmoid(gate) * up).astype(BF16)
    o_ref[...] += _dot(act, wd_ref[...])

    if final_norm:
        @pl.when(j == pl.num_programs(1) - 1)
        def _():
            o_ref[...] = _rms(o_ref[...], gf_ref[...])


def _ffn(x2d, g, wg, wu, wd, g_final, final_norm, tm=1024, tf=512):
    t, d = x2d.shape
    f = wg.shape[1]
    tm = min(tm, t)
    return pl.pallas_call(
        functools.partial(_ffn_kernel, final_norm=final_norm),
        out_shape=jax.ShapeDtypeStruct((t, d), F32),
        grid=(t // tm, f // tf),
        in_specs=[pl.BlockSpec((tm, d), lambda i, j: (i, 0)),
                  pl.BlockSpec((1, d), lambda i, j: (0, 0)),
                  pl.BlockSpec((d, tf), lambda i, j: (0, j)),
                  pl.BlockSpec((d, tf), lambda i, j: (0, j)),
                  pl.BlockSpec((tf, d), lambda i, j: (j, 0)),
                  pl.BlockSpec((1, d), lambda i, j: (0, 0))],
        out_specs=pl.BlockSpec((tm, d), lambda i, j: (i, 0)),
        scratch_shapes=[pltpu.VMEM((tm, d), BF16)],
        compiler_params=_cparams(("arbitrary", "arbitrary")),
        name="ffn",
    )(x2d, g.reshape(1, d), wg, wu, wd, g_final.reshape(1, d))


_RWKV_SRC = np.concatenate([[0], np.cumsum((GW, W_LORA, GW, GW, A_LORA, G_LORA))])
_RWKV_DST = (0, 3 * GW, GW, 2 * GW, 3 * GW + LORA_PAD, 3 * GW + 2 * LORA_PAD)


def _reorder_rwkv(a):
    out = jnp.zeros(a.shape[:-1] + (RWKV_W,), a.dtype)
    for i, dst in enumerate(_RWKV_DST):
        lo, hi = int(_RWKV_SRC[i]), int(_RWKV_SRC[i + 1])
        out = out.at[..., dst:dst + hi - lo].set(a[..., lo:hi])
    return out


def _prepare_w_in(w_in):
    wt = jnp.swapaxes(w_in, 1, 2)
    zeros = lambda n: jnp.zeros(wt.shape[:1] + (n, wt.shape[2]), BF16)
    rows = lambda lo, hi: wt[:, lo:hi].astype(BF16)
    piece = lambda i: rows(SRC_D + int(_RWKV_SRC[i]), SRC_D + int(_RWKV_SRC[i + 1]))
    c_q = (wt[:, SRC_C:SRC_C + GW] * (ATT_SCALE * LOG2E)).astype(BF16)
    return jnp.concatenate([rows(0, A_W),
                            piece(0), piece(2), piece(3), piece(1), zeros(LORA_PAD - W_LORA),
                            piece(4), zeros(LORA_PAD - A_LORA), piece(5),
                            rows(SRC_B, SRC_C), c_q, rows(SRC_C + GW, SRC_D)], axis=1)


def kernel(x, norm_mix_g, w_in, pos_bias, sgu_ln_g, sgu_w, sgu_b, rwkv_mu, rwkv_w0, rwkv_w2, rwkv_a0, rwkv_a2,
           rwkv_g2, rwkv_k_k, rwkv_k_a, rwkv_r_k, rwkv_lnx_g, rwkv_lnx_b, branch_norm_g, w_out, norm_ffn_g,
           w_gate, w_up, w_down, norm_final_g):
    b, s, d = x.shape
    depth = w_in.shape[0]
    assert s % (DIL_BLOCK * DIL_PATTERNS[-1][1]) == 0 and s % (2 * MOBA_BLOCK) == 0
    bias_a = _bias_tiles(pos_bias, _dil_bucket_tiles(), 0)
    bias_c = _bias_tiles(pos_bias, _moba_bucket_tiles(s // MOBA_BLOCK), N_HEADS)
    x2d = x.reshape(b * s, d)
    w_in_b, mu_all = _prepare_w_in(w_in), _reorder_rwkv(rwkv_mu)
    for l in range(depth):
        mu_l = mu_all[l]
        proj_a, proj_bcd = _inproj(x2d, norm_mix_g[l], w_in_b, l)
        proj3 = proj_bcd.reshape(b, s, BCD_W)
        ya, w_out_b = _mixer_dilated(proj_a.reshape(b, s, A_W), bias_a, w_out, l)
        yb = _mixer_sgu(proj3, sgu_ln_g[l], sgu_w[l], sgu_b[l])
        yc, ffn_w = _mixer_moba(proj3, bias_c, (w_gate, w_up, w_down), l)
        yd = _mixer_rwkv(proj3, mu_l, rwkv_w0[l], rwkv_w2[l], rwkv_a0[l], rwkv_a2[l], rwkv_g2[l],
                         rwkv_k_k[l], rwkv_k_a[l], rwkv_r_k[l], rwkv_lnx_g[l], rwkv_lnx_b[l])
        x2d = _outproj((ya, yb, yc, yd), branch_norm_g[l], w_out_b, x2d)
        x2d = _ffn(x2d, norm_ffn_g[l], *ffn_w, norm_final_g,
                   final_norm=(l == depth - 1))
    return x2d.reshape(b, s, d)
```

```python
import functools
import math

import jax
import jax.numpy as jnp
import numpy as np
from jax import lax
from jax.experimental import pallas as pl
from jax.experimental.pallas import tpu as pltpu

F32 = jnp.float32
BF16 = jnp.bfloat16

HEAD_DIM = 64
N_HEADS = 8
GW = N_HEADS * HEAD_DIM
LANES = 128
N_PAIRS = GW // LANES
DIL_PATTERNS = ((128, 1), (512, 4), (2048, 16))
DIL_BLOCK = 128
SGU_CHUNK = 128
SGU_LN_EPS = 1e-5
MOBA_BLOCK = 256
MOBA_TOPK = 3
W_LORA = 96
A_LORA = 96
G_LORA = 256
LORA_PAD = 128
RWKV_LN_EPS = 64e-5
RWKV_CHUNK = 64
RWKV_SUB = 16
NUM_BUCKETS = 32
MAX_DISTANCE = 2048
NORM_EPS = 1e-6
NEG_INF = -1e30
ATT_SCALE = HEAD_DIM ** -0.5
LOG2E = math.log2(math.e)

RWKV_W = 3 * GW + 2 * LORA_PAD + G_LORA
A_W = 3 * GW
SRC_B = A_W
SRC_C = SRC_B + 2 * GW
SRC_D = SRC_C + 3 * GW
COL_D = 0
COL_B = RWKV_W
COL_C = COL_B + 2 * GW
BCD_W = COL_C + 3 * GW
D_PROJ = A_W + BCD_W

VMEM_LIMIT = 56 * 1024 * 1024

HIGHEST = lax.Precision.HIGHEST


def _cparams(sem):
    return pltpu.CompilerParams(dimension_semantics=sem, vmem_limit_bytes=VMEM_LIMIT)


def _dot(a, b, precision=None):
    return lax.dot_general(a, b, (((1,), (0,)), ((), ())), precision=precision,
                           preferred_element_type=F32)


def _dot_nt(a, b, precision=None):
    return lax.dot_general(a, b, (((1,), (1,)), ((), ())), precision=precision,
                           preferred_element_type=F32)


def _dot_tn(a, b, precision=None):
    return lax.dot_general(a, b, (((0,), (0,)), ((), ())), precision=precision,
                           preferred_element_type=F32)


def _rms(x, g):
    return x * lax.rsqrt(jnp.mean(x * x, axis=-1, keepdims=True) + NORM_EPS) * g


def _t5_bucket_np(dist):
    dist = np.maximum(dist, 0)
    max_exact = NUM_BUCKETS // 2
    d = np.maximum(dist, 1).astype(np.float32)
    large = max_exact + (np.log(d / np.float32(max_exact)) / np.float32(math.log(MAX_DISTANCE / max_exact))
                         * np.float32(NUM_BUCKETS - max_exact)).astype(np.int32)
    large = np.minimum(large, NUM_BUCKETS - 1)
    return np.where(dist < max_exact, dist, large).astype(np.int32)


def _dil_bucket_tiles():
    qa = np.arange(DIL_BLOCK)[:, None]
    kj = np.arange(DIL_BLOCK)[None, :]
    tiles = []
    for _, dil in DIL_PATTERNS:
        tiles.append(np.where(kj <= qa, _t5_bucket_np((qa - kj) * dil), -1))
        tiles.append(np.where(kj >= qa, _t5_bucket_np((qa + DIL_BLOCK - kj) * dil), -1))
    return np.stack(tiles).astype(np.int32)


def _moba_bucket_tiles(nblk):
    ki = np.arange(MOBA_BLOCK)[:, None]
    qi = np.arange(MOBA_BLOCK)[None, :]
    return np.stack([_t5_bucket_np(db * MOBA_BLOCK + qi - ki) for db in range(nblk)])


def _bias_tile_kernel(tbl_ref, idx_ref, o_ref, *, head_offset, buckets):
    h = pl.program_id(0) + head_offset
    for t, present in enumerate(buckets):
        idx = idx_ref[t]
        acc = jnp.full(idx.shape, NEG_INF, F32)
        for b in present:
            if b >= 0:
                acc = jnp.where(idx == b, tbl_ref[b, h] * LOG2E, acc)
        o_ref[0, t] = acc


def _bias_tiles(pos_bias, idx_np, head_offset):
    nt, r, c = idx_np.shape
    buckets = tuple(tuple(int(b) for b in np.unique(idx_np[t])) for t in range(nt))
    return pl.pallas_call(
        functools.partial(_bias_tile_kernel, head_offset=head_offset, buckets=buckets),
        out_shape=jax.ShapeDtypeStruct((N_HEADS, nt, r, c), F32),
        grid=(N_HEADS,),
        in_specs=[pl.BlockSpec(memory_space=pltpu.SMEM),
                  pl.BlockSpec((nt, r, c), lambda h: (0, 0, 0))],
        out_specs=pl.BlockSpec((1, nt, r, c), lambda h: (h, 0, 0, 0)),
        compiler_params=_cparams(("arbitrary",)),
        name="bias_tiles",
    )(pos_bias, jnp.asarray(idx_np))


def _inproj_kernel(x_ref, g_ref, w_ref, oa_ref, ob_ref, h_scr):
    j = pl.program_id(1)

    @pl.when(j == 0)
    def _():
        h_scr[...] = _rms(x_ref[...], g_ref[...]).astype(BF16)
        oa_ref[...] = _dot_nt(h_scr[...], w_ref[...])

    @pl.when(j > 0)
    def _():
        ob_ref[...] = _dot_nt(h_scr[...], w_ref[...]).astype(BF16)


def _inproj(x2d, g, wt_all, layer, tm=1024):
    t, d = x2d.shape
    tn = A_W
    tm = min(tm, t)
    return pl.pallas_call(
        _inproj_kernel,
        out_shape=[jax.ShapeDtypeStruct((t, A_W), F32), jax.ShapeDtypeStruct((t, BCD_W), BF16)],
        grid=(t // tm, D_PROJ // tn),
        in_specs=[pl.BlockSpec((tm, d), lambda i, j: (i, 0)),
                  pl.BlockSpec((1, d), lambda i, j: (0, 0)),
                  pl.BlockSpec((None, tn, d), lambda i, j: (layer, j, 0))],
        out_specs=[pl.BlockSpec((tm, tn), lambda i, j: (i, 0)),
                   pl.BlockSpec((tm, tn), lambda i, j: (i, jnp.maximum(j - 1, 0)))],
        scratch_shapes=[pltpu.VMEM((tm, d), BF16)],
        compiler_params=_cparams(("arbitrary", "arbitrary")),
        name="inproj",
    )(x2d, g.reshape(1, d), wt_all)


def _dilated_kernel(q_ref, k_ref, v_ref, bias_ref, w_ref, o_ref, wb_ref, m0, m1, l0, l1, acc, *, seq):
    wb_ref[...] = w_ref[...].astype(wb_ref.dtype)
    c = DIL_BLOCK
    lane = lax.broadcasted_iota(jnp.int32, (c, LANES), 1)
    head0 = lane < HEAD_DIM

    stats = ((m0, l0), (m1, l1))
    head_lanes = (head0, ~head0)

    def logits(pairs, pi, wide, first):
        bias = [jnp.concatenate([bias_ref[h, 2 * pi + 1], bias_ref[h, 2 * pi]], axis=1) if wide
                else bias_ref[h, 2 * pi] for h in range(2)]
        q = [q_ref[0, qsl, :] * (ATT_SCALE * LOG2E) for qsl, _ in pairs]
        kb = [k_ref[0, ksl, :].astype(BF16) for _, ksl in pairs]
        idx = [(h, i) for i in range(len(pairs)) for h in range(2)]
        s = {(h, i): _dot_nt(jnp.where(head_lanes[h], q[i], 0.0).astype(BF16), kb[i]) for h, i in idx}
        s = {hi: s[hi] + bias[hi[0]] for hi in idx}
        return pairs, wide, first, idx, s

    def softmax_update(pairs, wide, first, idx, s):
        rep = (lambda x: jnp.concatenate([x, x], axis=1)) if wide else (lambda x: x)
        ones = jnp.ones(((2 if wide else 1) * c, LANES), BF16)
        vb = [jnp.concatenate([v_ref[0, ksl, :].astype(BF16), ones], axis=1) for _, ksl in pairs]
        m_new = {hi: jnp.broadcast_to(jnp.max(s[hi], axis=1, keepdims=True), (c, LANES)) for hi in idx}
        if not first:
            m_old = {(h, i): stats[h][0][pairs[i][0], :] for h, i in idx}
            m_new = {hi: jnp.maximum(m_old[hi], m_new[hi]) for hi in idx}
            alpha = {hi: jnp.exp2(m_old[hi] - m_new[hi]) for hi in idx}
        p = {hi: jnp.exp2(s[hi] - rep(m_new[hi])).astype(BF16) for hi in idx}
        o = {(h, i): _dot(p[h, i], vb[i]) for h, i in idx}
        for h, i in idx:
            m_ref, l_ref = stats[h]
            qsl = pairs[i][0]
            l_new = o[h, i][:, LANES:]
            l_ref[qsl, :] = l_new if first else alpha[h, i] * l_ref[qsl, :] + l_new
            m_ref[qsl, :] = m_new[h, i]
        for i, (qsl, _) in enumerate(pairs):
            o_new = jnp.where(head0, o[0, i][:, :LANES], o[1, i][:, :LANES])
            acc[qsl, :] = o_new if first else acc[qsl, :] * jnp.where(head0, alpha[0, i], alpha[1, i]) + o_new

    def group_size(n, cap):
        return max(g for g in range(1, cap + 1) if n % g == 0)

    order = sorted(range(len(DIL_PATTERNS)), key=lambda i: -DIL_PATTERNS[i][1])
    groups = []
    for pi in order:
        dil = DIL_PATTERNS[pi][1]
        first_pass = pi == order[0]
        nb = seq // dil // c

        def sl(r, n, blocks, dil=dil):
            start = r + n * (c * dil)
            return pl.ds(start, blocks * c) if dil == 1 else pl.ds(start, blocks * c, stride=dil)

        heads = [(sl(r, 0, 1),) * 2 for r in range(dil)]
        body = [(sl(r, n, 1), sl(r, n - 1, 2)) for r in range(dil) for n in range(1, nb)]
        for tiles, wide, cap in ((heads, False, 2), (body, True, 1)):
            g = group_size(len(tiles), cap) if tiles else 1
            groups += [(tiles[i:i + g], pi, wide, first_pass) for i in range(0, len(tiles), g)]

    pending = logits(*groups[0])
    for grp in groups[1:]:
        nxt = logits(*grp)
        softmax_update(*pending)
        pending = nxt
    softmax_update(*pending)

    o_ref[0] = (acc[...] / jnp.where(head0[:1], l0[...], l1[...])).astype(o_ref.dtype)


def _mixer_dilated(proj_a, bias_a, w_out, layer):
    b, s, _ = proj_a.shape
    steps = N_PAIRS * b
    rows_w, cols_w = w_out.shape[1:]
    assert rows_w % (16 * steps) == 0
    blk = lambda off: pl.BlockSpec((1, s, LANES), lambda p, i, off=off: (i, 0, off + p))
    return pl.pallas_call(
        functools.partial(_dilated_kernel, seq=s),
        out_shape=[jax.ShapeDtypeStruct((b, s, GW), BF16), jax.ShapeDtypeStruct((rows_w, cols_w), BF16)],
        grid=(N_PAIRS, b),
        in_specs=[blk(0), blk(GW // LANES), blk(2 * GW // LANES),
                  pl.BlockSpec((2, 2 * len(DIL_PATTERNS), DIL_BLOCK, DIL_BLOCK),
                               lambda p, i: (p, 0, 0, 0)),
                  pl.BlockSpec((None, rows_w // steps, cols_w), lambda p, i: (layer, p * b + i, 0))],
        out_specs=[pl.BlockSpec((1, s, LANES), lambda p, i: (i, 0, p)),
                   pl.BlockSpec((rows_w // steps, cols_w), lambda p, i: (p * b + i, 0))],
        scratch_shapes=[pltpu.VMEM((s, LANES), F32)] * 5,
        compiler_params=_cparams(("arbitrary", "arbitrary")),
        name="mixer_dilated",
    )(proj_a, proj_a, proj_a, bias_a, w_out)


def _gelu_tanh(x):
    return 0.5 * x * (1.0 + jnp.tanh(math.sqrt(2.0 / math.pi) * (x + 0.044715 * (x * x * x))))


def _sgu_kernel(u_ref, v_ref, lng_ref, w_ref, bias_ref, o_ref, *, rows):
    t = SGU_CHUNK
    u = _gelu_tanh(u_ref[0].astype(F32))
    v = _gelu_tanh(v_ref[0].astype(F32))
    mu = jnp.mean(v, axis=-1, keepdims=True)
    vc = v - mu
    var = jnp.mean(vc * vc, axis=-1, keepdims=True)
    vn = (vc * lax.rsqrt(var + SGU_LN_EPS) * lng_ref[...]).astype(BF16)
    r2 = lax.broadcasted_iota(jnp.int32, (2 * t, t), 0)
    c2 = lax.broadcasted_iota(jnp.int32, (2 * t, t), 1)
    causal = c2 <= jnp.where(r2 >= t, r2 - t, r2)
    first_group = lax.broadcasted_iota(jnp.int32, (t, LANES), 1) < HEAD_DIM
    for p in range(N_PAIRS):
        wp = jnp.where(causal, w_ref[p], 0.0).astype(BF16)
        for ci in range(rows // t):
            rs = slice(ci * t, (ci + 1) * t)
            cs = slice(p * LANES, (p + 1) * LANES)
            res = _dot(wp, vn[rs, cs])
            mixed = jnp.where(first_group, res[:t], res[t:]) + bias_ref[:, cs]
            o_ref[0, rs, cs] = (u[rs, cs] * mixed).astype(o_ref.dtype)


def _mixer_sgu(proj3, ln_g, w_s, b_s, rows=512):
    b, s, _ = proj3.shape
    rows = min(rows, s)
    t = SGU_CHUNK
    bias_full = jnp.repeat(b_s.T, HEAD_DIM, axis=1)
    w_pairs = w_s.reshape(N_PAIRS, 2 * t, t)
    return pl.pallas_call(
        functools.partial(_sgu_kernel, rows=rows),
        out_shape=jax.ShapeDtypeStruct((b, s, GW), BF16),
        grid=(b, s // rows),
        in_specs=[pl.BlockSpec((1, rows, GW), lambda i, j: (i, j, COL_B // GW)),
                  pl.BlockSpec((1, rows, GW), lambda i, j: (i, j, COL_B // GW + 1)),
                  pl.BlockSpec((1, GW), lambda i, j: (0, 0)),
                  pl.BlockSpec((N_PAIRS, 2 * t, t), lambda i, j: (0, 0, 0)),
                  pl.BlockSpec((t, GW), lambda i, j: (0, 0))],
        out_specs=pl.BlockSpec((1, rows, GW), lambda i, j: (i, j, 0)),
        compiler_params=_cparams(("arbitrary", "arbitrary")),
        name="mixer_sgu",
    )(proj3, proj3, ln_g.reshape(1, GW), w_pairs, bias_full)


def _moba_kernel(q_ref, k_ref, v_ref, bias_ref, wg_ref, wu_ref, wd_ref, o_ref, wgb_ref, wub_ref, wdb_ref,
                 kh_scr, vt_scr, ot_scr, *, seq):
    for src, dst in ((wg_ref, wgb_ref), (wu_ref, wub_ref), (wd_ref, wdb_ref)):
        dst[...] = src[...].astype(dst.dtype)

    ones_rows = 16
    bs = MOBA_BLOCK
    nblk = seq // bs
    lane = lax.broadcasted_iota(jnp.int32, (1, LANES), 1)
    head_lanes = (lane < HEAD_DIM, lane >= HEAD_DIM)
    blk = lambda i: slice(i * bs, (i + 1) * bs)

    q_all = q_ref[0]
    kbar = jnp.concatenate(
        [jnp.mean(k_ref[0, blk(j), :].astype(F32), axis=0, keepdims=True) for j in range(nblk)], axis=0)
    for j in range(nblk):
        kj = k_ref[0, blk(j), :]
        for h in range(2):
            kh_scr[h, j] = jnp.where(head_lanes[h], kj, jnp.zeros_like(kj))
        vt = v_ref[0, blk(j), :].astype(F32).T.astype(BF16)
        for h in range(2):
            vt_scr[j, h] = jnp.concatenate(
                [vt[h * HEAD_DIM:(h + 1) * HEAD_DIM], jnp.ones((ones_rows, bs), BF16)], axis=0)

    jrow = lax.broadcasted_iota(jnp.int32, (nblk, seq), 0)
    own = lax.broadcasted_iota(jnp.int32, (nblk, seq), 1) // bs
    krow = lax.broadcasted_iota(jnp.int32, (bs, bs), 0)
    qcol = lax.broadcasted_iota(jnp.int32, (bs, bs), 1)
    causal = krow <= qcol

    kb2 = jnp.concatenate([jnp.where(head_lanes[0], kbar, 0.0), jnp.where(head_lanes[1], kbar, 0.0)], axis=0)
    kb_hi, kb_lo = _split2(kb2)
    gates = _dot_nt(kb_hi, q_all) + _dot_nt(kb_lo, q_all)
    sel = []
    for h in range(2):
        gate = jnp.where(jrow < own, gates[h * nblk:(h + 1) * nblk], NEG_INF)
        rank = jnp.zeros((nblk, seq), jnp.int32)
        for j2 in range(nblk):
            gj = gate[j2:j2 + 1, :]
            ahead = (gj > gate) | ((gj == gate) & (j2 < jrow))
            rank = rank + ahead.astype(jnp.int32)
        sel.append((rank < MOBA_TOPK) & (jrow < own))

    def logits(grp):
        probs = [(h, qb) for qb in (grp, nblk - 1 - grp) for h in range(2)]
        tiles = [(h, qb, j) for h, qb in probs for j in range(qb + 1)]
        qh = {qb: q_ref[0, blk(qb), :] for _, qb in probs}
        s = {(h, qb, j): _dot_nt(kh_scr[h, j], qh[qb]) + bias_ref[h, qb - j] for h, qb, j in tiles}
        for h, qb in probs:
            s[h, qb, qb] = jnp.where(causal, s[h, qb, qb], NEG_INF)
        return probs, tiles, s

    def softmax_pv(probs, tiles, s):
        picked = {(h, qb, j): sel[h][j:j + 1, blk(qb)] for h, qb, j in tiles if j != qb}
        cmax = {t: jnp.max(s[t], axis=0, keepdims=True) for t in tiles}
        m = {}
        for h, qb in probs:
            m[h, qb] = functools.reduce(
                jnp.maximum, [cmax[h, qb, qb]] + [jnp.where(picked[h, qb, j], cmax[h, qb, j], NEG_INF)
                                                  for j in range(qb)])
        shift = {t: m[t[0], t[1]] if t[2] == t[1] else jnp.where(picked[t], m[t[0], t[1]], -NEG_INF)
                 for t in tiles}
        p = {t: jnp.exp2(s[t] - shift[t]).astype(BF16) for t in tiles}
        for h, qb in probs:
            acc = sum(_dot(vt_scr[j, h], p[h, qb, j]) for j in range(qb + 1))
            ot_scr[qb, h * HEAD_DIM:(h + 1) * HEAD_DIM, :] = acc[:HEAD_DIM] / acc[HEAD_DIM:HEAD_DIM + 1]

    pending = logits(0)
    for grp in range(1, nblk // 2):
        nxt = logits(grp)
        softmax_pv(*pending)
        pending = nxt
    softmax_pv(*pending)

    for qb in range(nblk):
        o_ref[0, blk(qb), :] = ot_scr[qb].T.astype(o_ref.dtype)


def _mixer_moba(proj3, bias_c, ffn_w, layer):
    b, s, _ = proj3.shape
    nblk = s // MOBA_BLOCK
    steps = N_PAIRS * b
    blk = lambda off: pl.BlockSpec((1, s, LANES), lambda p, i, off=off: (i, 0, off + p))
    w_shapes = [w.shape[1:] for w in ffn_w]
    assert all(r % (16 * steps) == 0 for r, _ in w_shapes)
    w_in_specs = [pl.BlockSpec((None, r // steps, c), lambda p, i: (layer, p * b + i, 0)) for r, c in w_shapes]
    w_out_specs = [pl.BlockSpec((r // steps, c), lambda p, i: (p * b + i, 0)) for r, c in w_shapes]
    out = pl.pallas_call(
        functools.partial(_moba_kernel, seq=s),
        out_shape=[jax.ShapeDtypeStruct((b, s, GW), BF16)] + [jax.ShapeDtypeStruct(sh, BF16) for sh in w_shapes],
        grid=(N_PAIRS, b),
        in_specs=[blk(COL_C // LANES), blk((COL_C + GW) // LANES), blk((COL_C + 2 * GW) // LANES),
                  pl.BlockSpec((2, nblk, MOBA_BLOCK, MOBA_BLOCK), lambda p, i: (p, 0, 0, 0))] + w_in_specs,
        out_specs=[pl.BlockSpec((1, s, LANES), lambda p, i: (i, 0, p))] + w_out_specs,
        scratch_shapes=[pltpu.VMEM((2, nblk, MOBA_BLOCK, LANES), BF16),
                        pltpu.VMEM((nblk, 2, HEAD_DIM + 16, MOBA_BLOCK), BF16),
                        pltpu.VMEM((nblk, LANES, MOBA_BLOCK), F32)],
        compiler_params=_cparams(("arbitrary", "arbitrary")),
        name="mixer_moba",
    )(proj3, proj3, proj3, bias_c, *ffn_w)
    return out[0], tuple(out[1:])


def _sigmoid(x):
    return 1.0 / (1.0 + jnp.exp(-x))


def _split2(x):
    hi = x.astype(BF16)
    return hi, (x - hi.astype(F32)).astype(BF16)


def _head_sum(x, ones_bd):
    hi, lo = _split2(x)
    cols = [slice(i * LANES, (i + 1) * LANES) for i in range(x.shape[1] // LANES)]
    return jnp.concatenate([_dot(hi[:, cs], ones_bd) + _dot(lo[:, cs], ones_bd) for cs in cols], axis=1)


def _rwkv_token_terms(p_ref, prev_ref, mu_ref, w0_ref, w2_ref, a0_ref, a2_ref, g2_ref, kk_ref, ka_ref, rk_ref,
                      ones_ref, tri_ref, rows):
    c = RWKV_CHUNK
    p = p_ref[0].astype(F32)
    prev_row = 0.0 if prev_ref is None else prev_ref[0, 15:16, :].astype(F32)
    first_row = lax.broadcasted_iota(jnp.int32, (rows, 1), 0) == 0
    y_prev = jnp.where(first_row, prev_row, pltpu.roll(p, 1, axis=0))
    xs = p + (y_prev - p) * mu_ref[...]
    yield
    r = xs[:, 0:GW]
    k = xs[:, GW:2 * GW]
    v = xs[:, 2 * GW:3 * GW]
    wd = xs[:, 3 * GW:3 * GW + LORA_PAD]
    ad = xs[:, 3 * GW + LORA_PAD:3 * GW + 2 * LORA_PAD]
    gd = xs[:, 3 * GW + 2 * LORA_PAD:]
    nz = -(w0_ref[...] + _dot(jnp.tanh(wd).astype(BF16), w2_ref[...]))
    yield
    softplus = jnp.maximum(nz, 0.0) + jnp.log(1.0 + jnp.exp(-jnp.abs(nz)))
    log_decay = -jnp.exp(-softplus - 0.5)
    yield
    a_sig = _sigmoid(a0_ref[...] + _dot(ad.astype(BF16), a2_ref[...]))
    yield
    g = _dot(_sigmoid(gd).astype(BF16), g2_ref[...])
    yield
    kk = k * kk_ref[...]
    ss = _head_sum(kk * kk, ones_ref[...])
    yield
    kk = kk / jnp.maximum(jnp.sqrt(ss), 1e-12)
    yield
    k_mod = k * (1.0 + (a_sig - 1.0) * ka_ref[...])
    kb = kk * a_sig
    yield
    hi = log_decay.astype(BF16)
    rem = log_decay - hi.astype(F32)
    mid = rem.astype(BF16)
    lo = (rem - mid.astype(F32)).astype(BF16)
    tri = tri_ref[...]
    yield
    cum = jnp.concatenate([_dot(tri, hi[rs]) + _dot(tri, mid[rs]) + _dot(tri, lo[rs])
                           for rs in (slice(i * c, (i + 1) * c) for i in range(rows // c))], axis=0)
    yield
    cum_end = jnp.concatenate(
        [jnp.broadcast_to(cum[(i + 1) * c - 1:(i + 1) * c, :], (c, GW)) for i in range(rows // c)], axis=0)
    e_cum = jnp.exp(cum)
    yield
    e_inv = jnp.exp(-cum)
    yield
    e_rem = jnp.exp(cum_end - cum)
    yield
    coef = _head_sum(r * k_mod * rk_ref[...], ones_ref[...])
    yield
    bf = lambda x: x.astype(BF16)
    at = bf(-kk * jnp.exp(cum - log_decay))
    yield
    rt, bt = bf(r * e_cum), bf(kb * e_inv)
    yield
    kt, b_end = bf(k_mod * e_inv), bf(kb * e_rem)
    yield
    k_end, e_end = bf(k_mod * e_rem), jnp.exp(cum_end)
    yield
    return at, rt, bt, kt, b_end, k_end, bf(v), e_end, g, coef * v * g


def _drain(gen):
    while True:
        try:
            next(gen)
        except StopIteration as stop:
            return stop.value


def _block_diag(y, first):
    zero = jnp.zeros_like(y)
    return jnp.concatenate([jnp.where(first, y, zero), jnp.where(first, zero, y)], axis=0)


def _pair_nn(x, y, first):
    return _dot(x, _block_diag(y, first))


def _pair_nt(x, y, first):
    return _dot_nt(x, _block_diag(y, first))


def _pair_tn(x, y, first):
    full = _dot_tn(x, y)
    return jnp.where(first, full[:HEAD_DIM], full[HEAD_DIM:])


def _rwkv_chunk_terms(probs, masks, tick):
    first, strict, incl, same_sub, eye = masks
    c = RWKV_CHUNK
    bf = lambda xs: [x.astype(BF16) for x in xs]

    def nn(xs, ys):
        out = [_pair_nn(x, y, first) for x, y in zip(xs, ys)]
        tick()
        return out

    ident = eye.astype(F32)
    at, rt, bt, kt, b_end, k_end, v, e_end = [list(x) for x in zip(*probs)]
    ar = [jnp.concatenate([a, r], axis=0) for a, r in zip(at, rt)]
    g2 = [_dot_nt(x, jnp.concatenate([_block_diag(y, first), _block_diag(z, first)], axis=0))
          for x, y, z in zip(ar, bt, kt)]
    tick()
    gb = [g[:, :LANES] for g in g2]
    gk = [g[:, LANES:] for g in g2]
    tick()
    a_ab = [jnp.where(strict, g[:c], 0.0) for g in gb]
    a_rb = bf([jnp.where(incl, g[c:], 0.0) for g in gb])
    a_kr = bf([jnp.concatenate([jnp.where(strict, g[:c], 0.0), jnp.where(incl, g[c:], 0.0)], axis=0) for g in gk])
    ad = [jnp.where(same_sub, a, 0.0) for a in a_ab]
    an = bf([a - d for a, d in zip(a_ab, ad)])
    adb = bf(ad)
    p2 = nn(adb, adb)
    av = nn(a_kr, v)
    p2b = bf(p2)
    p4 = nn(p2b, p2b)
    x1 = nn(bf([ident + d for d in ad]), bf([ident + p for p in p2]))
    p4b = bf(p4)
    p8 = nn(p4b, p4b)
    x2 = nn(bf([ident + p for p in p4]), bf([ident + p for p in p8]))
    td = bf(nn(bf(x1), bf(x2)))
    m1 = nn(td, an)
    m1b = bf(m1)
    m2 = nn(m1b, m1b)
    x3 = nn(bf([ident + m for m in m1]), bf([ident + m for m in m2]))
    t_inv = bf(nn(bf(x3), td))
    def nn2(xs, ys, zs):
        out = [_dot(x, jnp.concatenate([_block_diag(y, first), _block_diag(z, first)], axis=1))
               for x, y, z in zip(xs, ys, zs)]
        tick()
        return [o[:, :LANES] for o in out], [o[:, LANES:] for o in out]

    a_hat, u0 = nn2(t_inv, at, bf([w[:c] for w in av]))
    a_hat, u0 = bf(a_hat), bf(u0)
    r_add, o_add = nn2(a_rb, a_hat, u0)
    r_hat = [r.astype(F32) + x for r, x in zip(rt, r_add)]
    o0 = [x + w[c:] for x, w in zip(o_add, av)]
    p_mat = [jnp.where(eye, e, 0.0) + _pair_tn(h, b, first) for e, h, b in zip(e_end, a_hat, b_end)]
    z_mat = [_pair_tn(jnp.concatenate([u, w], axis=0), jnp.concatenate([b, k], axis=0), first)
             for u, w, b, k in zip(u0, v, b_end, k_end)]
    return list(zip(p_mat, z_mat, r_hat, o0))


def _rwkv_kernel(*refs, rows):
    (p0_ref, p_ref, prev_ref), par_refs, (lng_ref, lnb_ref, o_ref, state) = refs[:3], refs[3:14], refs[14:18]
    slots = refs[18:]
    c = RWKV_CHUNK
    j = pl.program_id(1)

    def stage(vals, slot):
        for ref, val in zip(slots, vals):
            ref[slot] = val

    @pl.when(j == 0)
    def _():
        state[...] = jnp.zeros_like(state)
        stage(_drain(_rwkv_token_terms(p0_ref, None, *par_refs, rows)), 0)

    nxt_gen = _rwkv_token_terms(p_ref, prev_ref, *par_refs, rows)
    nxt_vals = []

    def tick():
        if not nxt_vals:
            try:
                next(nxt_gen)
            except StopIteration as stop:
                nxt_vals.append(stop.value)

    cur = j % 2
    at, rt, bt, kt, b_end, k_end, v, e_end, g, bg = [ref.at[cur] for ref in slots]
    ri = lax.broadcasted_iota(jnp.int32, (c, LANES), 0)
    lane = lax.broadcasted_iota(jnp.int32, (c, LANES), 1)
    first = lane < HEAD_DIM
    ci = jnp.where(first, lane, lane - HEAD_DIM)
    masks = (first, ci < ri, ci <= ri, (ri // RWKV_SUB) == (ci // RWKV_SUB), ri == ci)

    def head_mean(x):
        lo = jnp.sum(jnp.where(first, x, 0.0), axis=-1, keepdims=True)
        hi = jnp.sum(jnp.where(first, 0.0, x), axis=-1, keepdims=True)
        return jnp.where(first, lo, hi) * (1.0 / HEAD_DIM)

    chunks = [slice(i * c, (i + 1) * c) for i in range(rows // c)]
    pair = lambda pr: slice(pr * LANES, (pr + 1) * LANES)
    probs = [tuple(x[rs, pair(pr)] for x in (at, rt, bt, kt, b_end, k_end, v))
             + (e_end[rs.start:rs.start + 1, pair(pr)],)
             for rs in chunks for pr in range(N_PAIRS)]
    terms = _rwkv_chunk_terms(probs, masks, tick)
    s_cur = [state[pr] for pr in range(N_PAIRS)]
    for i, rs in enumerate(chunks):
        for pr in range(N_PAIRS):
            p_mat, z_mat, r_hat, o0 = terms[i * N_PAIRS + pr]
            s_hi, s_lo = _split2(s_cur[pr])
            p_hi, p_lo = _split2(p_mat)
            o = _pair_nt(r_hat.astype(BF16), s_hi, first) + o0
            s_cur[pr] = (_pair_nn(s_hi, p_hi, first) + _pair_nn(s_hi, p_lo, first) + _pair_nn(s_lo, p_hi, first)
                         + z_mat)
            oc = o - head_mean(o)
            y = oc * lax.rsqrt(head_mean(oc * oc) + RWKV_LN_EPS) * lng_ref[pr:pr + 1, :] + lnb_ref[pr:pr + 1, :]
            o_ref[0, rs, pair(pr)] = (y * g[rs, pair(pr)] + bg[rs, pair(pr)]).astype(o_ref.dtype)
        tick()
    for pr in range(N_PAIRS):
        state[pr] = s_cur[pr]
    while not nxt_vals:
        tick()
    stage(nxt_vals[0], 1 - cur)


def _mixer_rwkv(proj3, mu, w0, w2, a0, a2, g2, k_k, k_a, r_k, lnx_g, lnx_b, rows=512):
    b, s, _ = proj3.shape
    rows = min(rows, s)
    pad = lambda w, n: jnp.concatenate([w, jnp.zeros((n - w.shape[0],) + w.shape[1:], w.dtype)], axis=0)
    head_of = np.arange(LANES) // HEAD_DIM
    ones_bd = jnp.asarray((head_of[:, None] == head_of[None, :]).astype(np.float32), dtype=BF16)
    tok = np.arange(RWKV_CHUNK)
    tri_bd = jnp.asarray((tok[None, :] <= tok[:, None]).astype(np.float32), dtype=BF16)
    vec = lambda a: a.reshape(1, -1)
    par = lambda a: a.reshape(N_PAIRS, LANES)
    full = lambda shape: pl.BlockSpec(shape, lambda i, j: (0,) * len(shape))
    col = COL_D // RWKV_W
    sub = rows // 16
    nt = s // rows
    nxt = lambda j: jnp.minimum(j + 1, nt - 1)
    slot = lambda dt: pltpu.VMEM((2, rows, GW), dt)
    return pl.pallas_call(
        functools.partial(_rwkv_kernel, rows=rows),
        out_shape=jax.ShapeDtypeStruct((b, s, GW), BF16),
        grid=(b, nt),
        in_specs=[pl.BlockSpec((1, rows, RWKV_W), lambda i, j: (i, 0, col)),
                  pl.BlockSpec((1, rows, RWKV_W), lambda i, j: (i, nxt(j), col)),
                  pl.BlockSpec((1, 16, RWKV_W), lambda i, j: (i, nxt(j) * sub - 1, col)),
                  full((1, RWKV_W)), full((1, GW)), full((LORA_PAD, GW)), full((1, GW)),
                  full((LORA_PAD, GW)), full((G_LORA, GW)), full((1, GW)), full((1, GW)), full((1, GW)),
                  full((LANES, LANES)), full((RWKV_CHUNK, RWKV_CHUNK)), full((N_PAIRS, LANES)),
                  full((N_PAIRS, LANES))],
        out_specs=pl.BlockSpec((1, rows, GW), lambda i, j: (i, j, 0)),
        scratch_shapes=[pltpu.VMEM((N_PAIRS, HEAD_DIM, LANES), F32)] + [slot(BF16)] * 7 + [slot(F32)] * 3,
        compiler_params=_cparams(("arbitrary", "arbitrary")),
        name="mixer_rwkv",
    )(proj3, proj3, proj3, vec(mu), vec(w0), pad(w2, LORA_PAD).astype(BF16), vec(a0), pad(a2, LORA_PAD).astype(BF16),
      g2.astype(BF16), vec(k_k), vec(k_a), vec(r_k), ones_bd, tri_bd, par(lnx_g), par(lnx_b))


def _outproj_kernel(ya_ref, yb_ref, yc_ref, yd_ref, g_ref, w_ref, x_ref, o_ref):
    acc = x_ref[...]
    for i, y_ref in enumerate((ya_ref, yb_ref, yc_ref, yd_ref)):
        yn = _rms(y_ref[...].astype(F32), g_ref[i:i + 1, :]).astype(BF16)
        acc = acc + _dot(yn, w_ref[i * GW:(i + 1) * GW, :])
    o_ref[...] = acc


def _outproj(ys, g, w, x2d, tm=512):
    t, d = x2d.shape
    tm = min(tm, t)
    y_spec = pl.BlockSpec((tm, GW), lambda i: (i, 0))
    return pl.pallas_call(
        _outproj_kernel,
        out_shape=jax.ShapeDtypeStruct((t, d), F32),
        grid=(t // tm,),
        in_specs=[y_spec] * 4 + [pl.BlockSpec((4, GW), lambda i: (0, 0)),
                                 pl.BlockSpec((4 * GW, d), lambda i: (0, 0)),
                                 pl.BlockSpec((tm, d), lambda i: (i, 0))],
        out_specs=pl.BlockSpec((tm, d), lambda i: (i, 0)),
        compiler_params=_cparams(("arbitrary",)),
        name="outproj",
    )(*[y.reshape(t, GW) for y in ys], g.reshape(4, GW), w, x2d)


def _ffn_kernel(x_ref, g_ref, wg_ref, wu_ref, wd_ref, gf_ref, o_ref, h_scr, *, final_norm):
    j = pl.program_id(1)

    @pl.when(j == 0)
    def _():
        x = x_ref[...]
        h_scr[...] = _rms(x, g_ref[...]).astype(BF16)
        o_ref[...] = x

    h = h_scr[...]
    gate = _dot(h, wg_ref[...])
    up = _dot(h, wu_ref[...])
    act = (gate * _sigmoid(gate) * up).astype(BF16)
    o_ref[...] += _dot(act, wd_ref[...])

    if final_norm:
        @pl.when(j == pl.num_programs(1) - 1)
        def _():
            o_ref[...] = _rms(o_ref[...], gf_ref[...])


def _ffn(x2d, g, wg, wu, wd, g_final, final_norm, tm=1024, tf=512):
    t, d = x2d.shape
    f = wg.shape[1]
    tm = min(tm, t)
    return pl.pallas_call(
        functools.partial(_ffn_kernel, final_norm=final_norm),
        out_shape=jax.ShapeDtypeStruct((t, d), F32),
        grid=(t // tm, f // tf),
        in_specs=[pl.BlockSpec((tm, d), lambda i, j: (i, 0)),
                  pl.BlockSpec((1, d), lambda i, j: (0, 0)),
                  pl.BlockSpec((d, tf), lambda i, j: (0, j)),
                  pl.BlockSpec((d, tf), lambda i, j: (0, j)),
                  pl.BlockSpec((tf, d), lambda i, j: (j, 0)),
                  pl.BlockSpec((1, d), lambda i, j: (0, 0))],
        out_specs=pl.BlockSpec((tm, d), lambda i, j: (i, 0)),
        scratch_shapes=[pltpu.VMEM((tm, d), BF16)],
        compiler_params=_cparams(("arbitrary", "arbitrary")),
        name="ffn",
    )(x2d, g.reshape(1, d), wg, wu, wd, g_final.reshape(1, d))


_RWKV_SRC = np.concatenate([[0], np.cumsum((GW, W_LORA, GW, GW, A_LORA, G_LORA))])
_RWKV_DST = (0, 3 * GW, GW, 2 * GW, 3 * GW + LORA_PAD, 3 * GW + 2 * LORA_PAD)


def _reorder_rwkv(a):
    out = jnp.zeros(a.shape[:-1] + (RWKV_W,), a.dtype)
    for i, dst in enumerate(_RWKV_DST):
        lo, hi = int(_RWKV_SRC[i]), int(_RWKV_SRC[i + 1])
        out = out.at[..., dst:dst + hi - lo].set(a[..., lo:hi])
    return out


def _prepare_w_in(w_in):
    wt = jnp.swapaxes(w_in, 1, 2)
    zeros = lambda n: jnp.zeros(wt.shape[:1] + (n, wt.shape[2]), BF16)
    rows = lambda lo, hi: wt[:, lo:hi].astype(BF16)
    piece = lambda i: rows(SRC_D + int(_RWKV_SRC[i]), SRC_D + int(_RWKV_SRC[i + 1]))
    c_q = (wt[:, SRC_C:SRC_C + GW] * (ATT_SCALE * LOG2E)).astype(BF16)
    return jnp.concatenate([rows(0, A_W),
                            piece(0), piece(2), piece(3), piece(1), zeros(LORA_PAD - W_LORA),
                            piece(4), zeros(LORA_PAD - A_LORA), piece(5),
                            rows(SRC_B, SRC_C), c_q, rows(SRC_C + GW, SRC_D)], axis=1)


def kernel(x, norm_mix_g, w_in, pos_bias, sgu_ln_g, sgu_w, sgu_b, rwkv_mu, rwkv_w0, rwkv_w2, rwkv_a0, rwkv_a2,
           rwkv_g2, rwkv_k_k, rwkv_k_a, rwkv_r_k, rwkv_lnx_g, rwkv_lnx_b, branch_norm_g, w_out, norm_ffn_g,
           w_gate, w_up, w_down, norm_final_g):
    b, s, d = x.shape
    depth = w_in.shape[0]
    assert s % (DIL_BLOCK * DIL_PATTERNS[-1][1]) == 0 and s % (2 * MOBA_BLOCK) == 0
    bias_a = _bias_tiles(pos_bias, _dil_bucket_tiles(), 0)
    bias_c = _bias_tiles(pos_bias, _moba_bucket_tiles(s // MOBA_BLOCK), N_HEADS)
    x2d = x.reshape(b * s, d)
    w_in_b, mu_all = _prepare_w_in(w_in), _reorder_rwkv(rwkv_mu)
    for l in range(depth):
        mu_l = mu_all[l]
        proj_a, proj_bcd = _inproj(x2d, norm_mix_g[l], w_in_b, l)
        proj3 = proj_bcd.reshape(b, s, BCD_W)
        ya, w_out_b = _mixer_dilated(proj_a.reshape(b, s, A_W), bias_a, w_out, l)
        yb = _mixer_sgu(proj3, sgu_ln_g[l], sgu_w[l], sgu_b[l])
        yc, ffn_w = _mixer_moba(proj3, bias_c, (w_gate, w_up, w_down), l)
        yd = _mixer_rwkv(proj3, mu_l, rwkv_w0[l], rwkv_w2[l], rwkv_a0[l], rwkv_a2[l], rwkv_g2[l],
                         rwkv_k_k[l], rwkv_k_a[l], rwkv_r_k[l], rwkv_lnx_g[l], rwkv_lnx_b[l])
        x2d = _outproj((ya, yb, yc, yd), branch_norm_g[l], w_out_b, x2d)
        x2d = _ffn(x2d, norm_ffn_g[l], *ffn_w, norm_final_g,
                   final_norm=(l == depth - 1))
    return x2d.reshape(b, s, d)
```
